```python
import jax, jax.numpy as jnp
from jax import lax
import numpy as np

D_MODEL = 4096
BATCH = 4
SEQ = 2048
DEPTH = 1

GRID_W = 64

NA_HEADS = 16
NA_HEAD_DIM = 128
NA_KH_MAX = 8
NA_KW = 16
NA_QBLOCK_W = 16
NA_KBLOCK_W = 32

MLA_HEADS = 16
MLA_Q_RANK = 1024
MLA_KV_RANK = 512
MLA_NOPE_DIM = 128
MLA_ROPE_DIM = 64
MLA_V_DIM = 128
ROPE_THETA = 10000.0
Q_BLOCK = 128

MEM_LEN = 256
MEM_HEADS = 4
MEM_HEAD_DIM = 128

D_FF = 256 * ((8 * D_MODEL // 3 + 255) // 256)
FFN_RES_WEIGHT = 0.5

NORM_EPS = 1e-6
NEG_INF = -1e30

NA_WIDTH = NA_HEADS * NA_HEAD_DIM
MLA_WIDTH = MLA_HEADS * MLA_V_DIM
MIX_WIDTH = NA_WIDTH + MLA_WIDTH
IN_PROJ_WIDTH = 3 * NA_WIDTH + MLA_Q_RANK + MLA_KV_RANK + MLA_ROPE_DIM
MEM_WIDTH = MEM_HEADS * MEM_HEAD_DIM

kernel_name = "hybrid_natten_mla_macaron_layer"


def rms_norm(x, g):
    xf = x.astype(jnp.float32)
    y = xf * lax.rsqrt(jnp.mean(xf * xf, axis=-1, keepdims=True) + NORM_EPS)
    return (y * g.astype(jnp.float32)).astype(x.dtype)


def swiglu(x, w_gate, w_up, w_down):
    return (jax.nn.silu(x @ w_gate) * (x @ w_up)) @ w_down


def rope(x, positions):
    half = MLA_ROPE_DIM // 2
    inv_freq = 1.0 / (ROPE_THETA ** (jnp.arange(half, dtype=jnp.float32) / half))
    ang = positions.astype(jnp.float32)[..., None] * inv_freq
    ang = ang.reshape(ang.shape[:2] + (1,) * (x.ndim - 3) + (half,))
    cos, sin = jnp.cos(ang), jnp.sin(ang)
    xf = x.astype(jnp.float32)
    x1, x2 = xf[..., :half], xf[..., half:]
    out = jnp.concatenate([x1 * cos - x2 * sin, x2 * cos + x1 * sin], axis=-1)
    return out.astype(x.dtype)


def neighborhood_attention(q, k, v, rpb):
    B, S, H, Dh = q.shape
    rows = S // GRID_W
    kh = min(NA_KH_MAX, rows)
    n_cb = GRID_W // NA_QBLOCK_W
    scale = Dh ** -0.5

    qg = jnp.moveaxis(q.reshape(B, rows, n_cb, NA_QBLOCK_W, H, Dh), 1, 0)
    kg = k.reshape(B, rows, GRID_W, H, Dh)
    vg = v.reshape(B, rows, GRID_W, H, Dh)

    qcol = np.arange(GRID_W).reshape(n_cb, NA_QBLOCK_W)
    cstart = np.clip(qcol - NA_KW // 2, 0, GRID_W - NA_KW)
    kb_start = np.clip(np.arange(n_cb) * NA_QBLOCK_W - NA_KW // 2, 0, GRID_W - NA_KBLOCK_W)
    kcol = kb_start[:, None] + np.arange(NA_KBLOCK_W)
    col_in = (kcol[:, None, :] >= cstart[:, :, None]) & (kcol[:, None, :] < cstart[:, :, None] + NA_KW)
    col_idx = np.clip(kcol[:, None, :] - qcol[:, :, None] + NA_KW - 1, 0, 2 * NA_KW - 2)
    mask = np.broadcast_to(col_in[:, :, None, :], (n_cb, NA_QBLOCK_W, kh, NA_KBLOCK_W))
    mask = jnp.asarray(mask.reshape(n_cb, NA_QBLOCK_W, kh * NA_KBLOCK_W))

    def row_block(args):
        r, q_r = args
        rs = jnp.clip(r - kh // 2, 0, rows - kh)
        k_rows = lax.dynamic_slice_in_dim(kg, rs, kh, axis=1)
        v_rows = lax.dynamic_slice_in_dim(vg, rs, kh, axis=1)
        k_blk = jnp.transpose(k_rows[:, :, kcol], (0, 2, 1, 3, 4, 5)).reshape(B, n_cb, kh * NA_KBLOCK_W, H, Dh)
        v_blk = jnp.transpose(v_rows[:, :, kcol], (0, 2, 1, 3, 4, 5)).reshape(B, n_cb, kh * NA_KBLOCK_W, H, Dh)
        s = jnp.einsum('bjqhd,bjkhd->bhjqk', q_r, k_blk).astype(jnp.float32) * scale
        row_idx = rs + jnp.arange(kh) - r + NA_KH_MAX - 1
        bias = rpb[:, row_idx][:, :, col_idx]
        bias = jnp.transpose(bias, (0, 2, 3, 1, 4)).reshape(H, n_cb, NA_QBLOCK_W, kh * NA_KBLOCK_W)
        s = jnp.where(mask, s + bias.astype(jnp.float32), NEG_INF)
        p = jax.nn.softmax(s, axis=-1).astype(v.dtype)
        o = jnp.einsum('bhjqk,bjkhd->bjqhd', p, v_blk)
        return o.reshape(B, GRID_W, H, Dh)

    out = lax.map(row_block, (jnp.arange(rows), qg))
    return jnp.moveaxis(out, 0, 1).reshape(B, S, H * Dh)


def latent_attention(c_q, c_kv, k_rope_in, positions, g_q_a, w_q_b, g_kv_a, w_kv_b):
    B, S, _ = c_q.shape
    H = MLA_HEADS
    q = (rms_norm(c_q, g_q_a) @ w_q_b).reshape(B, S, H, MLA_NOPE_DIM + MLA_ROPE_DIM)
    q_nope, q_pe = q[..., :MLA_NOPE_DIM], rope(q[..., MLA_NOPE_DIM:], positions)
    kv = (rms_norm(c_kv, g_kv_a) @ w_kv_b).reshape(B, S, H, MLA_NOPE_DIM + MLA_V_DIM)
    k_nope, v = kv[..., :MLA_NOPE_DIM], kv[..., MLA_NOPE_DIM:]
    k_pe = rope(k_rope_in, positions)
    scale = (MLA_NOPE_DIM + MLA_ROPE_DIM) ** -0.5
    nb = S // Q_BLOCK
    qn_b = jnp.moveaxis(q_nope.reshape(B, nb, Q_BLOCK, H, MLA_NOPE_DIM), 1, 0)
    qp_b = jnp.moveaxis(q_pe.reshape(B, nb, Q_BLOCK, H, MLA_ROPE_DIM), 1, 0)

    def q_block(args):
        qn, qp = args
        s = (jnp.einsum('bqhd,bkhd->bhqk', qn, k_nope)
             + jnp.einsum('bqhr,bkr->bhqk', qp, k_pe)).astype(jnp.float32) * scale
        p = jax.nn.softmax(s, axis=-1).astype(v.dtype)
        return jnp.einsum('bhqk,bkhd->bqhd', p, v)

    out = lax.map(q_block, (qn_b, qp_b))
    return jnp.moveaxis(out, 0, 1).reshape(B, S, MLA_WIDTH)


def memory_cross_attention(h, mem_n, w_q, w_kv, w_o):
    B, S, _ = h.shape
    M = mem_n.shape[1]
    q = (h @ w_q).reshape(B, S, MEM_HEADS, MEM_HEAD_DIM)
    kv = (mem_n @ w_kv).reshape(B, M, MEM_HEADS, 2 * MEM_HEAD_DIM)
    k, v = kv[..., :MEM_HEAD_DIM], kv[..., MEM_HEAD_DIM:]
    s = jnp.einsum('bqhd,bkhd->bhqk', q, k).astype(jnp.float32) * (MEM_HEAD_DIM ** -0.5)
    p = jax.nn.softmax(s, axis=-1).astype(v.dtype)
    o = jnp.einsum('bhqk,bkhd->bqhd', p, v).reshape(B, S, MEM_WIDTH)
    return o @ w_o


def setup_inputs(seed: int = 0) -> dict:
    key = jax.random.key(seed)
    ks = jax.random.split(key, 32)

    def w(k, shape, fan_in):
        return jax.random.normal(k, shape, jnp.float32) * (fan_in ** -0.5)

    def gain(k, shape):
        return 1.0 + 0.01 * jax.random.normal(k, shape, jnp.float32)

    L, D = DEPTH, D_MODEL
    return {
        "x": jax.random.normal(ks[0], (BATCH, SEQ, D), jnp.float32),
        "mem": jax.random.normal(ks[1], (BATCH, MEM_LEN, D), jnp.float32),
        "positions": jnp.broadcast_to(jnp.arange(SEQ, dtype=jnp.int32), (BATCH, SEQ)),
        "ffn1_w_gate": w(ks[2], (L, D, D_FF), D),
        "ffn1_w_up": w(ks[3], (L, D, D_FF), D),
        "ffn1_w_down": w(ks[4], (L, D_FF, D), D_FF),
        "g_ffn1": gain(ks[5], (L, 2, D)),
        "w_in": w(ks[6], (L, D, IN_PROJ_WIDTH), D),
        "g_q_a": gain(ks[7], (L, MLA_Q_RANK)),
        "w_q_b": w(ks[8], (L, MLA_Q_RANK, MLA_HEADS * (MLA_NOPE_DIM + MLA_ROPE_DIM)), MLA_Q_RANK),
        "g_kv_a": gain(ks[9], (L, MLA_KV_RANK)),
        "w_kv_b": w(ks[10], (L, MLA_KV_RANK, MLA_HEADS * (MLA_NOPE_DIM + MLA_V_DIM)), MLA_KV_RANK),
        "na_rpb": 0.02 * jax.random.normal(ks[11], (L, NA_HEADS, 2 * NA_KH_MAX - 1, 2 * NA_KW - 1), jnp.float32),
        "w_out": w(ks[12], (L, MIX_WIDTH, D), MIX_WIDTH),
        "g_mix": gain(ks[13], (L, 2, D)),
        "g_mem_in": gain(ks[14], (L, D)),
        "w_mem_q": w(ks[15], (L, D, MEM_WIDTH), D),
        "w_mem_kv": w(ks[16], (L, D, 2 * MEM_WIDTH), D),
        "w_mem_o": w(ks[17], (L, MEM_WIDTH, D), MEM_WIDTH),
        "g_mem_attn": gain(ks[18], (L, 2, D)),
        "ffn2_w_gate": w(ks[19], (L, D, D_FF), D),
        "ffn2_w_up": w(ks[20], (L, D, D_FF), D),
        "ffn2_w_down": w(ks[21], (L, D_FF, D), D_FF),
        "g_ffn2": gain(ks[22], (L, 2, D)),
        "g_final": gain(ks[23], (L, D)),
    }


def reference(x, mem, positions,
              ffn1_w_gate, ffn1_w_up, ffn1_w_down, g_ffn1,
              w_in, g_q_a, w_q_b, g_kv_a, w_kv_b, na_rpb, w_out, g_mix,
              g_mem_in, w_mem_q, w_mem_kv, w_mem_o, g_mem_attn,
              ffn2_w_gate, ffn2_w_up, ffn2_w_down, g_ffn2,
              g_final):
    B, S, _ = x.shape
    splits = list(np.cumsum([NA_WIDTH, NA_WIDTH, NA_WIDTH, MLA_Q_RANK, MLA_KV_RANK]))
    h = x
    for l in range(DEPTH):
        f = swiglu(rms_norm(h, g_ffn1[l, 0]), ffn1_w_gate[l], ffn1_w_up[l], ffn1_w_down[l])
        h = h + FFN_RES_WEIGHT * rms_norm(f, g_ffn1[l, 1])

        u = rms_norm(h, g_mix[l, 0])
        q_na, k_na, v_na, c_q, c_kv, k_rope_in = jnp.split(u @ w_in[l], splits, axis=-1)
        hd = (B, S, NA_HEADS, NA_HEAD_DIM)
        o_na = neighborhood_attention(q_na.reshape(hd), k_na.reshape(hd), v_na.reshape(hd), na_rpb[l])
        o_mla = latent_attention(c_q, c_kv, k_rope_in, positions, g_q_a[l], w_q_b[l], g_kv_a[l], w_kv_b[l])
        o = jnp.concatenate([o_na, o_mla], axis=-1) @ w_out[l]
        h = h + rms_norm(o, g_mix[l, 1])

        a = memory_cross_attention(rms_norm(h, g_mem_attn[l, 0]), rms_norm(mem, g_mem_in[l]),
                                   w_mem_q[l], w_mem_kv[l], w_mem_o[l])
        h = h + rms_norm(a, g_mem_attn[l, 1])

        f = swiglu(rms_norm(h, g_ffn2[l, 0]), ffn2_w_gate[l], ffn2_w_up[l], ffn2_w_down[l])
        h = h + FFN_RES_WEIGHT * rms_norm(f, g_ffn2[l, 1])

        h = rms_norm(h, g_final[l])
    return h
```

```python
import functools

import jax
import jax.numpy as jnp
from jax import lax
from jax.experimental import pallas as pl
from jax.experimental.pallas import tpu as pltpu

F32 = jnp.float32
BF16 = jnp.bfloat16

GRID_W = 64
NA_HEADS = 16
NA_HEAD_DIM = 128
NA_KH = 8
NA_KW = 16
MLA_HEADS = 16
MLA_Q_RANK = 1024
MLA_KV_RANK = 512
MLA_NOPE_DIM = 128
MLA_ROPE_DIM = 64
MLA_V_DIM = 128
MLA_QK_PAD = 256
ROPE_THETA = 10000.0
MEM_HEADS = 4
MEM_HEAD_DIM = 128
FFN_RES_WEIGHT = 0.5
NORM_EPS = 1e-6
NEG_INF = -1e30
NA_WIDTH = NA_HEADS * NA_HEAD_DIM
MLA_WIDTH = MLA_HEADS * MLA_V_DIM
IN_PROJ_MAIN = 3 * NA_WIDTH + MLA_Q_RANK + MLA_KV_RANK
LANES = 128
MIB = 1024 * 1024


def _cparams(n_axes, vmem_mib):
    return pltpu.CompilerParams(dimension_semantics=("parallel",) * n_axes,
                                vmem_limit_bytes=vmem_mib * MIB)


def _rms(x, g):
    return x * lax.rsqrt(jnp.mean(x * x, axis=-1, keepdims=True) + NORM_EPS) * g


def _row(v):
    return v.reshape(1, -1).astype(F32)


def _norm_cast_kernel(x_ref, g_ref, o_ref):
    o_ref[...] = _rms(x_ref[...], g_ref[...]).astype(o_ref.dtype)


def norm_cast(x, g, bm=256):
    m, d = x.shape
    bm = min(bm, m)
    return pl.pallas_call(
        _norm_cast_kernel,
        grid=(m // bm,),
        in_specs=[pl.BlockSpec((bm, d), lambda i: (i, 0)), pl.BlockSpec((1, d), lambda i: (0, 0))],
        out_specs=pl.BlockSpec((bm, d), lambda i: (i, 0)),
        out_shape=jax.ShapeDtypeStruct((m, d), BF16),
        compiler_params=_cparams(1, 32),
        name="norm_cast",
    )(x, _row(g))


def _resid_norm_kernel(h_ref, f_ref, gp_ref, gn_ref, ho_ref, uo_ref, *, weight):
    h = h_ref[...] + weight * _rms(f_ref[...], gp_ref[...])
    ho_ref[...] = h
    uo_ref[...] = _rms(h, gn_ref[...]).astype(uo_ref.dtype)


def resid_norm(h, f, g_post, weight, g_next, bm=256):
    m, d = h.shape
    bm = min(bm, m)
    row = pl.BlockSpec((bm, d), lambda i: (i, 0))
    vec = pl.BlockSpec((1, d), lambda i: (0, 0))
    return pl.pallas_call(
        functools.partial(_resid_norm_kernel, weight=weight),
        grid=(m // bm,),
        in_specs=[row, row, vec, vec],
        out_specs=[row, row],
        out_shape=[jax.ShapeDtypeStruct((m, d), F32), jax.ShapeDtypeStruct((m, d), BF16)],
        compiler_params=_cparams(1, 48),
        name="resid_norm",
    )(h, f, _row(g_post), _row(g_next))


def _resid_final_kernel(h_ref, f_ref, gp_ref, gn_ref, o_ref, *, weight):
    h = h_ref[...] + weight * _rms(f_ref[...], gp_ref[...])
    o_ref[...] = _rms(h, gn_ref[...])


def resid_final(h, f, g_post, weight, g_final, bm=256):
    m, d = h.shape
    bm = min(bm, m)
    row = pl.BlockSpec((bm, d), lambda i: (i, 0))
    vec = pl.BlockSpec((1, d), lambda i: (0, 0))
    return pl.pallas_call(
        functools.partial(_resid_final_kernel, weight=weight),
        grid=(m // bm,),
        in_specs=[row, row, vec, vec],
        out_specs=row,
        out_shape=jax.ShapeDtypeStruct((m, d), F32),
        compiler_params=_cparams(1, 48),
        name="resid_final",
    )(h, f, _row(g_post), _row(g_final))


def _mm_kernel(x_ref, w_ref, o_ref):
    o_ref[...] = jnp.dot(x_ref[...], w_ref[...].astype(BF16),
                         preferred_element_type=F32).astype(o_ref.dtype)


def mm(x, w, out_dtype, bm, bn, n_cols=None, vmem_mib=56, name="mm"):
    m, k = x.shape
    n = w.shape[1] if n_cols is None else n_cols
    bm, bn = min(bm, m), min(bn, n)
    assert m % bm == 0 and n % bn == 0
    return pl.pallas_call(
        _mm_kernel,
        grid=(m // bm, n // bn),
        in_specs=[pl.BlockSpec((bm, k), lambda i, j: (i, 0)), pl.BlockSpec((k, bn), lambda i, j: (0, j))],
        out_specs=pl.BlockSpec((bm, bn), lambda i, j: (i, j)),
        out_shape=jax.ShapeDtypeStruct((m, n), out_dtype),
        compiler_params=_cparams(2, vmem_mib),
        name=name,
    )(x, w)


def _mm2_kernel(xa_ref, xb_ref, w_ref, o_ref):
    ka = xa_ref.shape[1]
    w = w_ref[...].astype(BF16)
    acc = jnp.dot(xa_ref[...], w[:ka], preferred_element_type=F32)
    acc = acc + jnp.dot(xb_ref[...], w[ka:], preferred_element_type=F32)
    o_ref[...] = acc.astype(o_ref.dtype)


def mm2(xa, xb, w, out_dtype, bm, bn, vmem_mib=56, name="mm2"):
    m, ka = xa.shape
    kb = xb.shape[1]
    n = w.shape[1]
    bm, bn = min(bm, m), min(bn, n)
    assert m % bm == 0 and n % bn == 0 and w.shape[0] == ka + kb
    return pl.pallas_call(
        _mm2_kernel,
        grid=(m // bm, n // bn),
        in_specs=[pl.BlockSpec((bm, ka), lambda i, j: (i, 0)), pl.BlockSpec((bm, kb), lambda i, j: (i, 0)),
                  pl.BlockSpec((ka + kb, bn), lambda i, j: (0, j))],
        out_specs=pl.BlockSpec((bm, bn), lambda i, j: (i, j)),
        out_shape=jax.ShapeDtypeStruct((m, n), out_dtype),
        compiler_params=_cparams(2, vmem_mib),
        name=name,
    )(xa, xb, w)


def _gateup_kernel(x_ref, wg_ref, wu_ref, o_ref):
    x = x_ref[...]
    g = jnp.dot(x, wg_ref[...].astype(BF16), preferred_element_type=F32)
    u = jnp.dot(x, wu_ref[...].astype(BF16), preferred_element_type=F32)
    o_ref[...] = (g * jax.nn.sigmoid(g) * u).astype(o_ref.dtype)


def gateup(x, w_gate, w_up, bm=1024, bf=256):
    m, k = x.shape
    f = w_gate.shape[1]
    bm, bf = min(bm, m), min(bf, f)
    assert m % bm == 0 and f % bf == 0
    wspec = pl.BlockSpec((k, bf), lambda i, j: (0, j))
    return pl.pallas_call(
        _gateup_kernel,
        grid=(m // bm, f // bf),
        in_specs=[pl.BlockSpec((bm, k), lambda i, j: (i, 0)), wspec, wspec],
        out_specs=pl.BlockSpec((bm, bf), lambda i, j: (i, j)),
        out_shape=jax.ShapeDtypeStruct((m, f), BF16),
        compiler_params=_cparams(2, 56),
        name="ffn_gateup",
    )(x, w_gate, w_up)


def ffn(u, w_gate, w_up, w_down):
    hidden = gateup(u, w_gate, w_up)
    return mm(hidden, w_down.astype(BF16), F32, bm=512, bn=512, name="ffn_down")


def _rope_tables(pos_ref, invf_ref):
    ang = pos_ref[...].astype(F32) * invf_ref[...]
    lane = lax.broadcasted_iota(jnp.int32, ang.shape, 1)
    half = MLA_ROPE_DIM // 2
    cos_t = jnp.where(lane < MLA_ROPE_DIM, jnp.cos(ang), 0.0)
    sin = jnp.sin(ang)
    sin_t = jnp.where(lane < half, -sin, jnp.where(lane < MLA_ROPE_DIM, sin, 0.0))
    return cos_t, sin_t, lane < half


def _rope_apply(t, cos_t, sin_t, first_half):
    half = MLA_ROPE_DIM // 2
    partner = jnp.where(first_half, pltpu.roll(t, LANES - half, 1), pltpu.roll(t, half, 1))
    return t * cos_t + partner * sin_t


def _inv_freq_lanes():
    half = MLA_ROPE_DIM // 2
    inv_freq = 1.0 / (ROPE_THETA ** (jnp.arange(half, dtype=F32) / half))
    return jnp.concatenate([inv_freq, inv_freq, jnp.zeros((LANES - MLA_ROPE_DIM,), F32)]).reshape(1, LANES)


def _qb_kernel(cq_ref, g_ref, w_ref, pos_ref, invf_ref, o_ref, *, scale):
    cn = _rms(cq_ref[...].astype(F32), g_ref[...]).astype(BF16)
    q = jnp.dot(cn, w_ref[...], preferred_element_type=F32)
    cos_t, sin_t, first_half = _rope_tables(pos_ref, invf_ref)
    for h in range(MLA_HEADS):
        c0 = h * MLA_QK_PAD
        o_ref[:, c0:c0 + MLA_NOPE_DIM] = (q[:, c0:c0 + MLA_NOPE_DIM] * scale).astype(o_ref.dtype)
        t = q[:, c0 + MLA_NOPE_DIM:c0 + MLA_QK_PAD]
        o_ref[:, c0 + MLA_NOPE_DIM:c0 + MLA_QK_PAD] = (
            _rope_apply(t, cos_t, sin_t, first_half) * scale).astype(o_ref.dtype)


def q_b_proj(qkv, g_q_a, w_q_b, pos, invf, bm=512):
    m = qkv.shape[0]
    bm = min(bm, m)
    h, dq = MLA_HEADS, MLA_NOPE_DIM + MLA_ROPE_DIM
    w = w_q_b.reshape(MLA_Q_RANK, h, dq)
    w = jnp.pad(w, ((0, 0), (0, 0), (0, MLA_QK_PAD - dq))).reshape(MLA_Q_RANK, h * MLA_QK_PAD).astype(BF16)
    cq_block = (3 * NA_WIDTH) // MLA_Q_RANK
    return pl.pallas_call(
        functools.partial(_qb_kernel, scale=float(dq) ** -0.5),
        grid=(m // bm,),
        in_specs=[pl.BlockSpec((bm, MLA_Q_RANK), lambda i: (i, cq_block)),
                  pl.BlockSpec((1, MLA_Q_RANK), lambda i: (0, 0)),
                  pl.BlockSpec((MLA_Q_RANK, h * MLA_QK_PAD), lambda i: (0, 0)),
                  pl.BlockSpec((bm, 1), lambda i: (i, 0)),
                  pl.BlockSpec((1, LANES), lambda i: (0, 0))],
        out_specs=pl.BlockSpec((bm, h * MLA_QK_PAD), lambda i: (i, 0)),
        out_shape=jax.ShapeDtypeStruct((m, h * MLA_QK_PAD), BF16),
        compiler_params=_cparams(1, 56),
        name="mla_q_proj",
    )(qkv, _row(g_q_a), w, pos, invf)


def _kvb_kernel(ckv_ref, g_ref, w_ref, u_ref, wt_ref, pos_ref, invf_ref, k_ref, v_ref):
    cn = _rms(ckv_ref[...].astype(F32), g_ref[...]).astype(BF16)
    kv = jnp.dot(cn, w_ref[...], preferred_element_type=F32)
    kr = jnp.dot(u_ref[...], wt_ref[...], preferred_element_type=F32)
    cos_t, sin_t, first_half = _rope_tables(pos_ref, invf_ref)
    kpe = _rope_apply(kr, cos_t, sin_t, first_half).astype(k_ref.dtype)
    for h in range(MLA_HEADS):
        c0 = h * (MLA_NOPE_DIM + MLA_V_DIM)
        k_ref[:, h * MLA_QK_PAD:h * MLA_QK_PAD + MLA_NOPE_DIM] = kv[:, c0:c0 + MLA_NOPE_DIM].astype(k_ref.dtype)
        k_ref[:, h * MLA_QK_PAD + MLA_NOPE_DIM:(h + 1) * MLA_QK_PAD] = kpe
        v_ref[:, h * MLA_V_DIM:(h + 1) * MLA_V_DIM] = kv[:, c0 + MLA_NOPE_DIM:c0 + MLA_NOPE_DIM + MLA_V_DIM].astype(v_ref.dtype)


def kv_b_proj(qkv, g_kv_a, w_kv_b, u, w_rope_in, pos, invf, bm=512):
    m, d = u.shape
    bm = min(bm, m)
    h = MLA_HEADS
    ckv_block = (3 * NA_WIDTH + MLA_Q_RANK) // MLA_KV_RANK
    wt = jnp.pad(w_rope_in, ((0, 0), (0, LANES - MLA_ROPE_DIM))).astype(BF16)
    kv_w = h * (MLA_NOPE_DIM + MLA_V_DIM)
    return pl.pallas_call(
        _kvb_kernel,
        grid=(m // bm,),
        in_specs=[pl.BlockSpec((bm, MLA_KV_RANK), lambda i: (i, ckv_block)),
                  pl.BlockSpec((1, MLA_KV_RANK), lambda i: (0, 0)),
                  pl.BlockSpec((MLA_KV_RANK, kv_w), lambda i: (0, 0)),
                  pl.BlockSpec((bm, d), lambda i: (i, 0)),
                  pl.BlockSpec((d, LANES), lambda i: (0, 0)),
                  pl.BlockSpec((bm, 1), lambda i: (i, 0)),
                  pl.BlockSpec((1, LANES), lambda i: (0, 0))],
        out_specs=[pl.BlockSpec((bm, h * MLA_QK_PAD), lambda i: (i, 0)),
                   pl.BlockSpec((bm, h * MLA_V_DIM), lambda i: (i, 0))],
        out_shape=[jax.ShapeDtypeStruct((m, h * MLA_QK_PAD), BF16),
                   jax.ShapeDtypeStruct((m, h * MLA_V_DIM), BF16)],
        compiler_params=_cparams(1, 56),
        name="mla_kv_proj",
    )(qkv, _row(g_kv_a), w_kv_b.astype(BF16), u, wt, pos, invf)


def _softmax_pv(s, v):
    m = jnp.max(s, axis=-1, keepdims=True)
    p = jnp.exp(s - m)
    l = jnp.sum(p, axis=-1, keepdims=True)
    return jnp.dot(p.astype(BF16), v, preferred_element_type=F32) / l


def _mla_kernel(q_ref, k_ref, v_ref, o_ref):
    s = lax.dot_general(q_ref[...], k_ref[...], (((1,), (1,)), ((), ())), preferred_element_type=F32)
    o_ref[...] = _softmax_pv(s, v_ref[...]).astype(o_ref.dtype)


def mla_attention(q, k, v, batch, seq, bq=512):
    bq = min(bq, seq)
    nq = seq // bq
    return pl.pallas_call(
        _mla_kernel,
        grid=(batch, MLA_HEADS, nq),
        in_specs=[pl.BlockSpec((bq, MLA_QK_PAD), lambda b, h, i: (b * nq + i, h)),
                  pl.BlockSpec((seq, MLA_QK_PAD), lambda b, h, i: (b, h)),
                  pl.BlockSpec((seq, MLA_V_DIM), lambda b, h, i: (b, h))],
        out_specs=pl.BlockSpec((bq, MLA_V_DIM), lambda b, h, i: (b * nq + i, h)),
        out_shape=jax.ShapeDtypeStruct((batch * seq, MLA_WIDTH), BF16),
        compiler_params=_cparams(3, 40),
        name="mla_attention",
    )(q, k, v)


NA_BIAS_ROWS = 2 * NA_KH - 1
NA_BIAS_COLS = 2 * NA_KW - 1
NA_BIAS_PAIRS = NA_BIAS_ROWS - 1


def _na_bias_kernel(rpb_ref, o_ref):
    h = pl.program_id(0)
    row = lax.broadcasted_iota(jnp.int32, (GRID_W, LANES), 0)
    lane = lax.broadcasted_iota(jnp.int32, (GRID_W, LANES), 1)
    col_idx = (lane & (GRID_W - 1)) - row + (NA_KW - 1)
    left = lane < GRID_W

    def body(d, carry):
        base = (h * NA_BIAS_ROWS + d) * NA_BIAS_COLS
        acc = jnp.zeros((GRID_W, LANES), F32)
        for j in range(NA_BIAS_COLS):
            val = jnp.where(left, rpb_ref[base + j], rpb_ref[base + NA_BIAS_COLS + j])
            acc = acc + jnp.where(col_idx == j, val, 0.0)
        o_ref[0, d] = acc
        return carry

    lax.fori_loop(0, NA_BIAS_PAIRS, body, 0)


def na_bias_table(rpb):
    return pl.pallas_call(
        _na_bias_kernel,
        grid=(NA_HEADS,),
        in_specs=[pl.BlockSpec(memory_space=pltpu.SMEM)],
        out_specs=pl.BlockSpec((1, NA_BIAS_PAIRS, GRID_W, LANES), lambda h: (h, 0, 0, 0)),
        out_shape=jax.ShapeDtypeStruct((NA_HEADS, NA_BIAS_PAIRS, GRID_W, LANES), F32),
        compiler_params=_cparams(1, 16),
        name="na_bias_table",
    )(rpb.reshape(-1).astype(F32))


def _na_kernel(q_ref, k_ref, v_ref, bias_ref, o_ref, *, rows, heads, scale):
    nk = NA_KH * GRID_W
    qc = lax.broadcasted_iota(jnp.int32, (GRID_W, nk), 0)
    kc = lax.broadcasted_iota(jnp.int32, (GRID_W, nk), 1) & (GRID_W - 1)
    cstart = jnp.clip(qc - NA_KW // 2, 0, GRID_W - NA_KW)
    in_window = (kc >= cstart) & (kc < cstart + NA_KW)

    def body(r, carry):
        rs = jnp.clip(r - NA_KH // 2, 0, rows - NA_KH)
        d0 = rs - r + (NA_KH - 1)
        q0 = pl.multiple_of(r * GRID_W, GRID_W)
        k0 = pl.multiple_of(rs * GRID_W, GRID_W)
        for h in range(heads):
            cols = slice(h * NA_HEAD_DIM, (h + 1) * NA_HEAD_DIM)
            q = q_ref[pl.ds(q0, GRID_W), cols]
            k = k_ref[pl.ds(k0, nk), cols]
            v = v_ref[pl.ds(k0, nk), cols]
            s = lax.dot_general(q, k, (((1,), (1,)), ((), ())), preferred_element_type=F32) * scale
            bias = jnp.concatenate([bias_ref[h, d0 + 2 * p] for p in range(NA_KH // 2)], axis=1)
            s = jnp.where(in_window, s + bias, NEG_INF)
            o_ref[pl.ds(q0, GRID_W), cols] = _softmax_pv(s, v).astype(o_ref.dtype)
        return carry

    lax.fori_loop(0, rows, body, 0)


def na_attention(qkv, bias, batch, seq, heads_per_step=2):
    g = heads_per_step
    w = g * NA_HEAD_DIM
    groups = NA_HEADS // g
    rows = seq // GRID_W
    return pl.pallas_call(
        functools.partial(_na_kernel, rows=rows, heads=g, scale=float(NA_HEAD_DIM) ** -0.5),
        grid=(batch, groups),
        in_specs=[pl.BlockSpec((seq, w), lambda b, j: (b, j)),
                  pl.BlockSpec((seq, w), lambda b, j: (b, groups + j)),
                  pl.BlockSpec((seq, w), lambda b, j: (b, 2 * groups + j)),
                  pl.BlockSpec((g, NA_BIAS_PAIRS, GRID_W, LANES), lambda b, j: (j, 0, 0, 0))],
        out_specs=pl.BlockSpec((seq, w), lambda b, j: (b, j)),
        out_shape=jax.ShapeDtypeStruct((batch * seq, NA_WIDTH), BF16),
        compiler_params=_cparams(2, 32),
        name="na_attention",
    )(qkv, qkv, qkv, bias)


def _mem_attn_kernel(q_ref, kv_ref, wo_ref, o_ref, *, scale):
    outs = []
    for h in range(MEM_HEADS):
        q = q_ref[:, h * MEM_HEAD_DIM:(h + 1) * MEM_HEAD_DIM]
        k = kv_ref[:, 2 * h * MEM_HEAD_DIM:(2 * h + 1) * MEM_HEAD_DIM]
        v = kv_ref[:, (2 * h + 1) * MEM_HEAD_DIM:(2 * h + 2) * MEM_HEAD_DIM]
        s = lax.dot_general(q, k, (((1,), (1,)), ((), ())), preferred_element_type=F32) * scale
        outs.append(_softmax_pv(s, v).astype(BF16))
    o = jnp.concatenate(outs, axis=1)
    o_ref[...] = jnp.dot(o, wo_ref[...].astype(BF16), preferred_element_type=F32)


def mem_attention(q, kv, w_o, batch, seq, bq=512):
    mem_len = kv.shape[0] // batch
    d = w_o.shape[1]
    bq = min(bq, seq)
    nq = seq // bq
    width = MEM_HEADS * MEM_HEAD_DIM
    return pl.pallas_call(
        functools.partial(_mem_attn_kernel, scale=float(MEM_HEAD_DIM) ** -0.5),
        grid=(batch, nq),
        in_specs=[pl.BlockSpec((bq, width), lambda b, i: (b * nq + i, 0)),
                  pl.BlockSpec((mem_len, 2 * width), lambda b, i: (b, 0)),
                  pl.BlockSpec((width, d), lambda b, i: (0, 0))],
        out_specs=pl.BlockSpec((bq, d), lambda b, i: (b * nq + i, 0)),
        out_shape=jax.ShapeDtypeStruct((batch * seq, d), F32),
        compiler_params=_cparams(2, 56),
        name="mem_attention",
    )(q, kv, w_o)


def kernel(x, mem, positions, ffn1_w_gate, ffn1_w_up, ffn1_w_down, g_ffn1, w_in, g_q_a, w_q_b, g_kv_a, w_kv_b, na_rpb, w_out, g_mix, g_mem_in, w_mem_q, w_mem_kv, w_mem_o, g_mem_attn, ffn2_w_gate, ffn2_w_up, ffn2_w_down, g_ffn2, g_final):
    batch, seq, d = x.shape
    m = batch * seq
    depth = ffn1_w_gate.shape[0]
    pos = positions.reshape(m, 1).astype(jnp.int32)
    invf = _inv_freq_lanes()
    mem2 = mem.reshape(-1, d)

    h = x.reshape(m, d)
    u = norm_cast(h, g_ffn1[0, 0])
    out = None
    for l in range(depth):
        f = ffn(u, ffn1_w_gate[l], ffn1_w_up[l], ffn1_w_down[l])
        h, u = resid_norm(h, f, g_ffn1[l, 1], FFN_RES_WEIGHT, g_mix[l, 0])

        qkv = mm(u, w_in[l], BF16, bm=1024, bn=512, n_cols=IN_PROJ_MAIN, name="in_proj")
        q_cat = q_b_proj(qkv, g_q_a[l], w_q_b[l], pos, invf)
        k_cat, v_mla = kv_b_proj(qkv, g_kv_a[l], w_kv_b[l], u, w_in[l][:, IN_PROJ_MAIN:], pos, invf)
        o_mla = mla_attention(q_cat, k_cat, v_mla, batch, seq)
        o_na = na_attention(qkv, na_bias_table(na_rpb[l]), batch, seq)
        o = mm2(o_na, o_mla, w_out[l], F32, bm=1024, bn=512, name="out_proj")
        h, u = resid_norm(h, o, g_mix[l, 1], 1.0, g_mem_attn[l, 0])

        mem_n = norm_cast(mem2, g_mem_in[l])
        kv_mem = mm(mem_n, w_mem_kv[l], BF16, bm=1024, bn=512, name="mem_kv_proj")
        q_mem = mm(u, w_mem_q[l], BF16, bm=1024, bn=512, name="mem_q_proj")
        a = mem_attention(q_mem, kv_mem, w_mem_o[l], batch, seq)
        h, u = resid_norm(h, a, g_mem_attn[l, 1], 1.0, g_ffn2[l, 0])

        f = ffn(u, ffn2_w_gate[l], ffn2_w_up[l], ffn2_w_down[l])
        if l + 1 < depth:
            h = resid_final(h, f, g_ffn2[l, 1], FFN_RES_WEIGHT, g_final[l])
            u = norm_cast(h, g_ffn1[l + 1, 0])
        else:
            out = resid_final(h, f, g_ffn2[l, 1], FFN_RES_WEIGHT, g_final[l])
    return out.reshape(batch, seq, d)
```

```python
import functools

import jax
import jax.numpy as jnp
from jax import lax
from jax.experimental import pallas as pl
from jax.experimental.pallas import tpu as pltpu

F32 = jnp.float32
BF16 = jnp.bfloat16

GRID_W = 64
NA_HEADS = 16
NA_HEAD_DIM = 128
NA_KH = 8
NA_KW = 16
MLA_HEADS = 16
MLA_Q_RANK = 1024
MLA_KV_RANK = 512
MLA_NOPE_DIM = 128
MLA_ROPE_DIM = 64
MLA_V_DIM = 128
MLA_QK_PAD = 256
ROPE_THETA = 10000.0
MEM_HEADS = 4
MEM_HEAD_DIM = 128
FFN_RES_WEIGHT = 0.5
NORM_EPS = 1e-6
NEG_INF = -1e30
NA_WIDTH = NA_HEADS * NA_HEAD_DIM
MLA_WIDTH = MLA_HEADS * MLA_V_DIM
IN_PROJ_MAIN = 3 * NA_WIDTH + MLA_Q_RANK + MLA_KV_RANK
LANES = 128
MIB = 1024 * 1024


def _cparams(n_axes, vmem_mib):
    return pltpu.CompilerParams(dimension_semantics=("parallel",) * n_axes,
                                vmem_limit_bytes=vmem_mib * MIB)


def _rms(x, g):
    return x * lax.rsqrt(jnp.mean(x * x, axis=-1, keepdims=True) + NORM_EPS) * g


def _row(v):
    return v.reshape(1, -1).astype(F32)


def _norm_cast_kernel(x_ref, g_ref, o_ref):
    o_ref[...] = _rms(x_ref[...], g_ref[...]).astype(o_ref.dtype)


def norm_cast(x, g, bm=256):
    m, d = x.shape
    bm = min(bm, m)
    return pl.pallas_call(
        _norm_cast_kernel,
        grid=(m // bm,),
        in_specs=[pl.BlockSpec((bm, d), lambda i: (i, 0)), pl.BlockSpec((1, d), lambda i: (0, 0))],
        out_specs=pl.BlockSpec((bm, d), lambda i: (i, 0)),
        out_shape=jax.ShapeDtypeStruct((m, d), BF16),
        compiler_params=_cparams(1, 32),
        name="norm_cast",
    )(x, _row(g))


def _resid_norm_kernel(h_ref, f_ref, gp_ref, gn_ref, ho_ref, uo_ref, *, weight):
    h = h_ref[...] + weight * _rms(f_ref[...], gp_ref[...])
    ho_ref[...] = h
    uo_ref[...] = _rms(h, gn_ref[...]).astype(uo_ref.dtype)


def resid_norm(h, f, g_post, weight, g_next, bm=256):
    m, d = h.shape
    bm = min(bm, m)
    row = pl.BlockSpec((bm, d), lambda i: (i, 0))
    vec = pl.BlockSpec((1, d), lambda i: (0, 0))
    return pl.pallas_call(
        functools.partial(_resid_norm_kernel, weight=weight),
        grid=(m // bm,),
        in_specs=[row, row, vec, vec],
        out_specs=[row, row],
        out_shape=[jax.ShapeDtypeStruct((m, d), F32), jax.ShapeDtypeStruct((m, d), BF16)],
        compiler_params=_cparams(1, 48),
        name="resid_norm",
    )(h, f, _row(g_post), _row(g_next))


def _resid_final_kernel(h_ref, f_ref, gp_ref, gn_ref, o_ref, *, weight):
    h = h_ref[...] + weight * _rms(f_ref[...], gp_ref[...])
    o_ref[...] = _rms(h, gn_ref[...])


def resid_final(h, f, g_post, weight, g_final, bm=256):
    m, d = h.shape
    bm = min(bm, m)
    row = pl.BlockSpec((bm, d), lambda i: (i, 0))
    vec = pl.BlockSpec((1, d), lambda i: (0, 0))
    return pl.pallas_call(
        functools.partial(_resid_final_kernel, weight=weight),
        grid=(m // bm,),
        in_specs=[row, row, vec, vec],
        out_specs=row,
        out_shape=jax.ShapeDtypeStruct((m, d), F32),
        compiler_params=_cparams(1, 48),
        name="resid_final",
    )(h, f, _row(g_post), _row(g_final))


def _mm_kernel(x_ref, w_ref, o_ref):
    o_ref[...] = jnp.dot(x_ref[...], w_ref[...].astype(BF16),
                         preferred_element_type=F32).astype(o_ref.dtype)


def mm(x, w, out_dtype, bm, bn, n_cols=None, vmem_mib=56, name="mm"):
    m, k = x.shape
    n = w.shape[1] if n_cols is None else n_cols
    bm, bn = min(bm, m), min(bn, n)
    assert m % bm == 0 and n % bn == 0
    return pl.pallas_call(
        _mm_kernel,
        grid=(m // bm, n // bn),
        in_specs=[pl.BlockSpec((bm, k), lambda i, j: (i, 0)), pl.BlockSpec((k, bn), lambda i, j: (0, j))],
        out_specs=pl.BlockSpec((bm, bn), lambda i, j: (i, j)),
        out_shape=jax.ShapeDtypeStruct((m, n), out_dtype),
        compiler_params=_cparams(2, vmem_mib),
        name=name,
    )(x, w)


def _mm2_kernel(xa_ref, xb_ref, w_ref, o_ref):
    ka = xa_ref.shape[1]
    w = w_ref[...].astype(BF16)
    acc = jnp.dot(xa_ref[...], w[:ka], preferred_element_type=F32)
    acc = acc + jnp.dot(xb_ref[...], w[ka:], preferred_element_type=F32)
    o_ref[...] = acc.astype(o_ref.dtype)


def mm2(xa, xb, w, out_dtype, bm, bn, vmem_mib=56, name="mm2"):
    m, ka = xa.shape
    kb = xb.shape[1]
    n = w.shape[1]
    bm, bn = min(bm, m), min(bn, n)
    assert m % bm == 0 and n % bn == 0 and w.shape[0] == ka + kb
    return pl.pallas_call(
        _mm2_kernel,
        grid=(m // bm, n // bn),
        in_specs=[pl.BlockSpec((bm, ka), lambda i, j: (i, 0)), pl.BlockSpec((bm, kb), lambda i, j: (i, 0)),
                  pl.BlockSpec((ka + kb, bn), lambda i, j: (0, j))],
        out_specs=pl.BlockSpec((bm, bn), lambda i, j: (i, j)),
        out_shape=jax.ShapeDtypeStruct((m, n), out_dtype),
        compiler_params=_cparams(2, vmem_mib),
        name=name,
    )(xa, xb, w)


def _gateup_kernel(x_ref, wg_ref, wu_ref, o_ref):
    x = x_ref[...]
    g = jnp.dot(x, wg_ref[...].astype(BF16), preferred_element_type=F32)
    u = jnp.dot(x, wu_ref[...].astype(BF16), preferred_element_type=F32)
    o_ref[...] = (g * jax.nn.sigmoid(g) * u).astype(o_ref.dtype)


def gateup(x, w_gate, w_up, bm=1024, bf=256):
    m, k = x.shape
    f = w_gate.shape[1]
    bm, bf = min(bm, m), min(bf, f)
    assert m % bm == 0 and f % bf == 0
    wspec = pl.BlockSpec((k, bf), lambda i, j: (0, j))
    return pl.pallas_call(
        _gateup_kernel,
        grid=(m // bm, f // bf),
        in_specs=[pl.BlockSpec((bm, k), lambda i, j: (i, 0)), wspec, wspec],
        out_specs=pl.BlockSpec((bm, bf), lambda i, j: (i, j)),
        out_shape=jax.ShapeDtypeStruct((m, f), BF16),
        compiler_params=_cparams(2, 56),
        name="ffn_gateup",
    )(x, w_gate, w_up)


def ffn(u, w_gate, w_up, w_down):
    hidden = gateup(u, w_gate, w_up)
    return mm(hidden, w_down.astype(BF16), F32, bm=512, bn=512, name="ffn_down")


def _rope_tables(pos_ref, invf_ref):
    ang = pos_ref[...].astype(F32) * invf_ref[...]
    lane = lax.broadcasted_iota(jnp.int32, ang.shape, 1)
    half = MLA_ROPE_DIM // 2
    cos_t = jnp.where(lane < MLA_ROPE_DIM, jnp.cos(ang), 0.0)
    sin = jnp.sin(ang)
    sin_t = jnp.where(lane < half, -sin, jnp.where(lane < MLA_ROPE_DIM, sin, 0.0))
    return cos_t, sin_t, lane < half


def _rope_apply(t, cos_t, sin_t, first_half):
    half = MLA_ROPE_DIM // 2
    partner = jnp.where(first_half, pltpu.roll(t, LANES - half, 1), pltpu.roll(t, half, 1))
    return t * cos_t + partner * sin_t


def _inv_freq_lanes():
    half = MLA_ROPE_DIM // 2
    inv_freq = 1.0 / (ROPE_THETA ** (jnp.arange(half, dtype=F32) / half))
    return jnp.concatenate([inv_freq, inv_freq, jnp.zeros((LANES - MLA_ROPE_DIM,), F32)]).reshape(1, LANES)


def _qb_kernel(cq_ref, g_ref, w_ref, pos_ref, invf_ref, o_ref, *, scale):
    cn = _rms(cq_ref[...].astype(F32), g_ref[...]).astype(BF16)
    q = jnp.dot(cn, w_ref[...], preferred_element_type=F32)
    cos_t, sin_t, first_half = _rope_tables(pos_ref, invf_ref)
    for h in range(MLA_HEADS):
        c0 = h * MLA_QK_PAD
        o_ref[:, c0:c0 + MLA_NOPE_DIM] = (q[:, c0:c0 + MLA_NOPE_DIM] * scale).astype(o_ref.dtype)
        t = q[:, c0 + MLA_NOPE_DIM:c0 + MLA_QK_PAD]
        o_ref[:, c0 + MLA_NOPE_DIM:c0 + MLA_QK_PAD] = (
            _rope_apply(t, cos_t, sin_t, first_half) * scale).astype(o_ref.dtype)


def q_b_proj(qkv, g_q_a, w_q_b, pos, invf, bm=512):
    m = qkv.shape[0]
    bm = min(bm, m)
    h, dq = MLA_HEADS, MLA_NOPE_DIM + MLA_ROPE_DIM
    w = w_q_b.reshape(MLA_Q_RANK, h, dq)
    w = jnp.pad(w, ((0, 0), (0, 0), (0, MLA_QK_PAD - dq))).reshape(MLA_Q_RANK, h * MLA_QK_PAD).astype(BF16)
    cq_block = (3 * NA_WIDTH) // MLA_Q_RANK
    return pl.pallas_call(
        functools.partial(_qb_kernel, scale=float(dq) ** -0.5),
        grid=(m // bm,),
        in_specs=[pl.BlockSpec((bm, MLA_Q_RANK), lambda i: (i, cq_block)),
                  pl.BlockSpec((1, MLA_Q_RANK), lambda i: (0, 0)),
                  pl.BlockSpec((MLA_Q_RANK, h * MLA_QK_PAD), lambda i: (0, 0)),
                  pl.BlockSpec((bm, 1), lambda i: (i, 0)),
                  pl.BlockSpec((1, LANES), lambda i: (0, 0))],
        out_specs=pl.BlockSpec((bm, h * MLA_QK_PAD), lambda i: (i, 0)),
        out_shape=jax.ShapeDtypeStruct((m, h * MLA_QK_PAD), BF16),
        compiler_params=_cparams(1, 56),
        name="mla_q_proj",
    )(qkv, _row(g_q_a), w, pos, invf)


def _kvb_kernel(ckv_ref, g_ref, w_ref, u_ref, wt_ref, pos_ref, invf_ref, k_ref, v_ref):
    cn = _rms(ckv_ref[...].astype(F32), g_ref[...]).astype(BF16)
    kv = jnp.dot(cn, w_ref[...], preferred_element_type=F32)
    kr = jnp.dot(u_ref[...], wt_ref[...], preferred_element_type=F32)
    cos_t, sin_t, first_half = _rope_tables(pos_ref, invf_ref)
    kpe = _rope_apply(kr, cos_t, sin_t, first_half).astype(k_ref.dtype)
    for h in range(MLA_HEADS):
        c0 = h * (MLA_NOPE_DIM + MLA_V_DIM)
        k_ref[:, h * MLA_QK_PAD:h * MLA_QK_PAD + MLA_NOPE_DIM] = kv[:, c0:c0 + MLA_NOPE_DIM].astype(k_ref.dtype)
        k_ref[:, h * MLA_QK_PAD + MLA_NOPE_DIM:(h + 1) * MLA_QK_PAD] = kpe
        v_ref[:, h * MLA_V_DIM:(h + 1) * MLA_V_DIM] = kv[:, c0 + MLA_NOPE_DIM:c0 + MLA_NOPE_DIM + MLA_V_DIM].astype(v_ref.dtype)


def kv_b_proj(qkv, g_kv_a, w_kv_b, u, w_rope_in, pos, invf, bm=512):
    m, d = u.shape
    bm = min(bm, m)
    h = MLA_HEADS
    ckv_block = (3 * NA_WIDTH + MLA_Q_RANK) // MLA_KV_RANK
    wt = jnp.pad(w_rope_in, ((0, 0), (0, LANES - MLA_ROPE_DIM))).astype(BF16)
    kv_w = h * (MLA_NOPE_DIM + MLA_V_DIM)
    return pl.pallas_call(
        _kvb_kernel,
        grid=(m // bm,),
        in_specs=[pl.BlockSpec((bm, MLA_KV_RANK), lambda i: (i, ckv_block)),
                  pl.BlockSpec((1, MLA_KV_RANK), lambda i: (0, 0)),
                  pl.BlockSpec((MLA_KV_RANK, kv_w), lambda i: (0, 0)),
                  pl.BlockSpec((bm, d), lambda i: (i, 0)),
                  pl.BlockSpec((d, LANES), lambda i: (0, 0)),
                  pl.BlockSpec((bm, 1), lambda i: (i, 0)),
                  pl.BlockSpec((1, LANES), lambda i: (0, 0))],
        out_specs=[pl.BlockSpec((bm, h * MLA_QK_PAD), lambda i: (i, 0)),
                   pl.BlockSpec((bm, h * MLA_V_DIM), lambda i: (i, 0))],
        out_shape=[jax.ShapeDtypeStruct((m, h * MLA_QK_PAD), BF16),
                   jax.ShapeDtypeStruct((m, h * MLA_V_DIM), BF16)],
        compiler_params=_cparams(1, 56),
        name="mla_kv_proj",
    )(qkv, _row(g_kv_a), w_kv_b.astype(BF16), u, wt, pos, invf)


def _softmax_pv(s, v):
    m = jnp.max(s, axis=-1, keepdims=True)
    p = jnp.exp(s - m)
    l = jnp.sum(p, axis=-1, keepdims=True)
    return jnp.dot(p.astype(BF16), v, preferred_element_type=F32) / l


def _mla_kernel(q_ref, k_ref, v_ref, o_ref, *, heads):
    scores = []
    for h in range(heads):
        qk = slice(h * MLA_QK_PAD, (h + 1) * MLA_QK_PAD)
        scores.append(lax.dot_general(q_ref[:, qk], k_ref[:, qk], (((1,), (1,)), ((), ())),
                                      preferred_element_type=F32))
    for h in range(heads):
        vs = slice(h * MLA_V_DIM, (h + 1) * MLA_V_DIM)
        o_ref[:, vs] = _softmax_pv(scores[h], v_ref[:, vs]).astype(o_ref.dtype)


def mla_attention(q, k, v, batch, seq, bq=512, heads_per_step=2):
    bq = min(bq, seq)
    nq = seq // bq
    g = heads_per_step
    return pl.pallas_call(
        functools.partial(_mla_kernel, heads=g),
        grid=(batch, MLA_HEADS // g, nq),
        in_specs=[pl.BlockSpec((bq, g * MLA_QK_PAD), lambda b, h, i: (b * nq + i, h)),
                  pl.BlockSpec((seq, g * MLA_QK_PAD), lambda b, h, i: (b, h)),
                  pl.BlockSpec((seq, g * MLA_V_DIM), lambda b, h, i: (b, h))],
        out_specs=pl.BlockSpec((bq, g * MLA_V_DIM), lambda b, h, i: (b * nq + i, h)),
        out_shape=jax.ShapeDtypeStruct((batch * seq, MLA_WIDTH), BF16),
        compiler_params=_cparams(3, 48),
        name="mla_attention",
    )(q, k, v)


NA_BIAS_ROWS = 2 * NA_KH - 1
NA_BIAS_COLS = 2 * NA_KW - 1
NA_BIAS_PAIRS = NA_BIAS_ROWS - 1


def _na_bias_kernel(rpb_ref, o_ref):
    h = pl.program_id(0)
    row = lax.broadcasted_iota(jnp.int32, (GRID_W, LANES), 0)
    lane = lax.broadcasted_iota(jnp.int32, (GRID_W, LANES), 1)
    col_idx = (lane & (GRID_W - 1)) - row + (NA_KW - 1)
    left = lane < GRID_W

    def body(d, carry):
        base = (h * NA_BIAS_ROWS + d) * NA_BIAS_COLS
        acc = jnp.zeros((GRID_W, LANES), F32)
        for j in range(NA_BIAS_COLS):
            val = jnp.where(left, rpb_ref[base + j], rpb_ref[base + NA_BIAS_COLS + j])
            acc = acc + jnp.where(col_idx == j, val, 0.0)
        o_ref[0, d] = acc
        return carry

    lax.fori_loop(0, NA_BIAS_PAIRS, body, 0)


def na_bias_table(rpb):
    return pl.pallas_call(
        _na_bias_kernel,
        grid=(NA_HEADS,),
        in_specs=[pl.BlockSpec(memory_space=pltpu.SMEM)],
        out_specs=pl.BlockSpec((1, NA_BIAS_PAIRS, GRID_W, LANES), lambda h: (h, 0, 0, 0)),
        out_shape=jax.ShapeDtypeStruct((NA_HEADS, NA_BIAS_PAIRS, GRID_W, LANES), F32),
        compiler_params=_cparams(1, 16),
        name="na_bias_table",
    )(rpb.reshape(-1).astype(F32))


def _na_kernel(q_ref, k_ref, v_ref, bias_ref, o_ref, *, rows, heads, rows_per_step, scale):
    nk = NA_KH * GRID_W
    qc = lax.broadcasted_iota(jnp.int32, (GRID_W, nk), 0)
    kc = lax.broadcasted_iota(jnp.int32, (GRID_W, nk), 1) & (GRID_W - 1)
    cstart = jnp.clip(qc - NA_KW // 2, 0, GRID_W - NA_KW)
    in_window = (kc >= cstart) & (kc < cstart + NA_KW)

    def body(step, carry):
        chains = []
        for rr in range(rows_per_step):
            r = step * rows_per_step + rr
            rs = jnp.clip(r - NA_KH // 2, 0, rows - NA_KH)
            d0 = rs - r + (NA_KH - 1)
            q0 = pl.multiple_of(r * GRID_W, GRID_W)
            k0 = pl.multiple_of(rs * GRID_W, GRID_W)
            for h in range(heads):
                cols = slice(h * NA_HEAD_DIM, (h + 1) * NA_HEAD_DIM)
                q = q_ref[pl.ds(q0, GRID_W), cols]
                k = k_ref[pl.ds(k0, nk), cols]
                s = lax.dot_general(q, k, (((1,), (1,)), ((), ())), preferred_element_type=F32) * scale
                bias = jnp.concatenate([bias_ref[h, d0 + 2 * p] for p in range(NA_KH // 2)], axis=1)
                chains.append((q0, k0, cols, jnp.where(in_window, s + bias, NEG_INF)))
        for q0, k0, cols, s in chains:
            v = v_ref[pl.ds(k0, nk), cols]
            o_ref[pl.ds(q0, GRID_W), cols] = _softmax_pv(s, v).astype(o_ref.dtype)
        return carry

    lax.fori_loop(0, rows // rows_per_step, body, 0)


def na_attention(qkv, bias, batch, seq, heads_per_step=4):
    g = heads_per_step
    w = g * NA_HEAD_DIM
    groups = NA_HEADS // g
    rows = seq // GRID_W
    return pl.pallas_call(
        functools.partial(_na_kernel, rows=rows, heads=g, rows_per_step=2, scale=float(NA_HEAD_DIM) ** -0.5),
        grid=(batch, groups),
        in_specs=[pl.BlockSpec((seq, w), lambda b, j: (b, j)),
                  pl.BlockSpec((seq, w), lambda b, j: (b, groups + j)),
                  pl.BlockSpec((seq, w), lambda b, j: (b, 2 * groups + j)),
                  pl.BlockSpec((g, NA_BIAS_PAIRS, GRID_W, LANES), lambda b, j: (j, 0, 0, 0))],
        out_specs=pl.BlockSpec((seq, w), lambda b, j: (b, j)),
        out_shape=jax.ShapeDtypeStruct((batch * seq, NA_WIDTH), BF16),
        compiler_params=_cparams(2, 32),
        name="na_attention",
    )(qkv, qkv, qkv, bias)


def _mem_attn_kernel(q_ref, kv_ref, wo_ref, o_ref, *, scale):
    outs = []
    for h in range(MEM_HEADS):
        q = q_ref[:, h * MEM_HEAD_DIM:(h + 1) * MEM_HEAD_DIM]
        k = kv_ref[:, 2 * h * MEM_HEAD_DIM:(2 * h + 1) * MEM_HEAD_DIM]
        v = kv_ref[:, (2 * h + 1) * MEM_HEAD_DIM:(2 * h + 2) * MEM_HEAD_DIM]
        s = lax.dot_general(q, k, (((1,), (1,)), ((), ())), preferred_element_type=F32) * scale
        outs.append(_softmax_pv(s, v).astype(BF16))
    o = jnp.concatenate(outs, axis=1)
    o_ref[...] = jnp.dot(o, wo_ref[...].astype(BF16), preferred_element_type=F32)


def mem_attention(q, kv, w_o, batch, seq, bq=512):
    mem_len = kv.shape[0] // batch
    d = w_o.shape[1]
    bq = min(bq, seq)
    nq = seq // bq
    width = MEM_HEADS * MEM_HEAD_DIM
    return pl.pallas_call(
        functools.partial(_mem_attn_kernel, scale=float(MEM_HEAD_DIM) ** -0.5),
        grid=(batch, nq),
        in_specs=[pl.BlockSpec((bq, width), lambda b, i: (b * nq + i, 0)),
                  pl.BlockSpec((mem_len, 2 * width), lambda b, i: (b, 0)),
                  pl.BlockSpec((width, d), lambda b, i: (0, 0))],
        out_specs=pl.BlockSpec((bq, d), lambda b, i: (b * nq + i, 0)),
        out_shape=jax.ShapeDtypeStruct((batch * seq, d), F32),
        compiler_params=_cparams(2, 56),
        name="mem_attention",
    )(q, kv, w_o)


def kernel(x, mem, positions, ffn1_w_gate, ffn1_w_up, ffn1_w_down, g_ffn1, w_in, g_q_a, w_q_b, g_kv_a, w_kv_b, na_rpb, w_out, g_mix, g_mem_in, w_mem_q, w_mem_kv, w_mem_o, g_mem_attn, ffn2_w_gate, ffn2_w_up, ffn2_w_down, g_ffn2, g_final):
    batch, seq, d = x.shape
    m = batch * seq
    depth = ffn1_w_gate.shape[0]
    pos = positions.reshape(m, 1).astype(jnp.int32)
    invf = _inv_freq_lanes()
    mem2 = mem.reshape(-1, d)

    h = x.reshape(m, d)
    u = norm_cast(h, g_ffn1[0, 0])
    out = None
    for l in range(depth):
        f = ffn(u, ffn1_w_gate[l], ffn1_w_up[l], ffn1_w_down[l])
        h, u = resid_norm(h, f, g_ffn1[l, 1], FFN_RES_WEIGHT, g_mix[l, 0])

        qkv = mm(u, w_in[l], BF16, bm=1024, bn=512, n_cols=IN_PROJ_MAIN, name="in_proj")
        q_cat = q_b_proj(qkv, g_q_a[l], w_q_b[l], pos, invf)
        k_cat, v_mla = kv_b_proj(qkv, g_kv_a[l], w_kv_b[l], u, w_in[l][:, IN_PROJ_MAIN:], pos, invf)
        o_mla = mla_attention(q_cat, k_cat, v_mla, batch, seq)
        o_na = na_attention(qkv, na_bias_table(na_rpb[l]), batch, seq)
        o = mm2(o_na, o_mla, w_out[l], F32, bm=1024, bn=512, name="out_proj")
        h, u = resid_norm(h, o, g_mix[l, 1], 1.0, g_mem_attn[l, 0])

        mem_n = norm_cast(mem2, g_mem_in[l])
        kv_mem = mm(mem_n, w_mem_kv[l], BF16, bm=1024, bn=512, name="mem_kv_proj")
        q_mem = mm(u, w_mem_q[l], BF16, bm=1024, bn=512, name="mem_q_proj")
        a = mem_attention(q_mem, kv_mem, w_mem_o[l], batch, seq)
        h, u = resid_norm(h, a, g_mem_attn[l, 1], 1.0, g_ffn2[l, 0])

        f = ffn(u, ffn2_w_gate[l], ffn2_w_up[l], ffn2_w_down[l])
        if l + 1 < depth:
            h = resid_final(h, f, g_ffn2[l, 1], FFN_RES_WEIGHT, g_final[l])
            u = norm_cast(h, g_ffn1[l + 1, 0])
        else:
            out = resid_final(h, f, g_ffn2[l, 1], FFN_RES_WEIGHT, g_final[l])
    return out.reshape(batch, seq, d)
```

```python
import functools

import jax
import jax.numpy as jnp
from jax import lax
from jax.experimental import pallas as pl
from jax.experimental.pallas import tpu as pltpu

F32 = jnp.float32
BF16 = jnp.bfloat16

GRID_W = 64
NA_HEADS = 16
NA_HEAD_DIM = 128
NA_KH = 8
NA_KW = 16
MLA_HEADS = 16
MLA_Q_RANK = 1024
MLA_KV_RANK = 512
MLA_NOPE_DIM = 128
MLA_ROPE_DIM = 64
MLA_V_DIM = 128
MLA_QK_PAD = 256
ROPE_THETA = 10000.0
MEM_HEADS = 4
MEM_HEAD_DIM = 128
FFN_RES_WEIGHT = 0.5
NORM_EPS = 1e-6
NEG_INF = -1e30
NA_WIDTH = NA_HEADS * NA_HEAD_DIM
MLA_WIDTH = MLA_HEADS * MLA_V_DIM
IN_PROJ_MAIN = 3 * NA_WIDTH + MLA_Q_RANK + MLA_KV_RANK
LANES = 128
MIB = 1024 * 1024


def _cparams(n_axes, vmem_mib):
    return pltpu.CompilerParams(dimension_semantics=("parallel",) * n_axes,
                                vmem_limit_bytes=vmem_mib * MIB)


def _rms(x, g):
    return x * lax.rsqrt(jnp.mean(x * x, axis=-1, keepdims=True) + NORM_EPS) * g


def _row(v):
    return v.reshape(1, -1).astype(F32)


def _norm_cast_kernel(x_ref, g_ref, o_ref):
    o_ref[...] = _rms(x_ref[...], g_ref[...]).astype(o_ref.dtype)


def norm_cast(x, g, bm=256):
    m, d = x.shape
    bm = min(bm, m)
    return pl.pallas_call(
        _norm_cast_kernel,
        grid=(m // bm,),
        in_specs=[pl.BlockSpec((bm, d), lambda i: (i, 0)), pl.BlockSpec((1, d), lambda i: (0, 0))],
        out_specs=pl.BlockSpec((bm, d), lambda i: (i, 0)),
        out_shape=jax.ShapeDtypeStruct((m, d), BF16),
        compiler_params=_cparams(1, 32),
        name="norm_cast",
    )(x, _row(g))


def _resid_norm_kernel(h_ref, f_ref, gp_ref, gn_ref, ho_ref, uo_ref, *, weight):
    h = h_ref[...] + weight * _rms(f_ref[...].astype(F32), gp_ref[...])
    ho_ref[...] = h
    uo_ref[...] = _rms(h, gn_ref[...]).astype(uo_ref.dtype)


def resid_norm(h, f, g_post, weight, g_next, bm=256):
    m, d = h.shape
    bm = min(bm, m)
    row = pl.BlockSpec((bm, d), lambda i: (i, 0))
    vec = pl.BlockSpec((1, d), lambda i: (0, 0))
    return pl.pallas_call(
        functools.partial(_resid_norm_kernel, weight=weight),
        grid=(m // bm,),
        in_specs=[row, row, vec, vec],
        out_specs=[row, row],
        out_shape=[jax.ShapeDtypeStruct((m, d), F32), jax.ShapeDtypeStruct((m, d), BF16)],
        compiler_params=_cparams(1, 48),
        name="resid_norm",
    )(h, f, _row(g_post), _row(g_next))


def _resid_final_kernel(h_ref, f_ref, gp_ref, gn_ref, o_ref, *, weight):
    h = h_ref[...] + weight * _rms(f_ref[...].astype(F32), gp_ref[...])
    o_ref[...] = _rms(h, gn_ref[...])


def resid_final(h, f, g_post, weight, g_final, bm=256):
    m, d = h.shape
    bm = min(bm, m)
    row = pl.BlockSpec((bm, d), lambda i: (i, 0))
    vec = pl.BlockSpec((1, d), lambda i: (0, 0))
    return pl.pallas_call(
        functools.partial(_resid_final_kernel, weight=weight),
        grid=(m // bm,),
        in_specs=[row, row, vec, vec],
        out_specs=row,
        out_shape=jax.ShapeDtypeStruct((m, d), F32),
        compiler_params=_cparams(1, 48),
        name="resid_final",
    )(h, f, _row(g_post), _row(g_final))


def _x_spec(bm, k, single_buffer):
    if single_buffer:
        return pl.BlockSpec((bm, k), lambda i, j: (i, 0), pipeline_mode=pl.Buffered(1))
    return pl.BlockSpec((bm, k), lambda i, j: (i, 0))


def _w_spec(k, bn, layer):
    return pl.BlockSpec((None, k, bn), lambda i, j: (layer, 0, j))


def _mm_kernel(x_ref, w_ref, o_ref):
    o_ref[...] = jnp.dot(x_ref[...], w_ref[...].astype(BF16),
                         preferred_element_type=F32).astype(o_ref.dtype)


def mm(x, w, layer, out_dtype, bm, bn, n_cols=None, single_buffer_x=False, vmem_mib=56, name="mm"):
    m, k = x.shape
    n = w.shape[2] if n_cols is None else n_cols
    bm, bn = min(bm, m), min(bn, n)
    assert m % bm == 0 and n % bn == 0 and w.shape[1] == k
    return pl.pallas_call(
        _mm_kernel,
        grid=(m // bm, n // bn),
        in_specs=[_x_spec(bm, k, single_buffer_x), _w_spec(k, bn, layer)],
        out_specs=pl.BlockSpec((bm, bn), lambda i, j: (i, j)),
        out_shape=jax.ShapeDtypeStruct((m, n), out_dtype),
        compiler_params=_cparams(2, vmem_mib),
        name=name,
    )(x, w)


def _mm2_kernel(xa_ref, xb_ref, w_ref, o_ref):
    ka = xa_ref.shape[1]
    w = w_ref[...].astype(BF16)
    acc = jnp.dot(xa_ref[...], w[:ka], preferred_element_type=F32)
    acc = acc + jnp.dot(xb_ref[...], w[ka:], preferred_element_type=F32)
    o_ref[...] = acc.astype(o_ref.dtype)


def mm2(xa, xb, w, layer, out_dtype, bm, bn, single_buffer_x=False, vmem_mib=56, name="mm2"):
    m, ka = xa.shape
    kb = xb.shape[1]
    n = w.shape[2]
    bm, bn = min(bm, m), min(bn, n)
    assert m % bm == 0 and n % bn == 0 and w.shape[1] == ka + kb
    return pl.pallas_call(
        _mm2_kernel,
        grid=(m // bm, n // bn),
        in_specs=[_x_spec(bm, ka, single_buffer_x), _x_spec(bm, kb, single_buffer_x),
                  _w_spec(ka + kb, bn, layer)],
        out_specs=pl.BlockSpec((bm, bn), lambda i, j: (i, j)),
        out_shape=jax.ShapeDtypeStruct((m, n), out_dtype),
        compiler_params=_cparams(2, vmem_mib),
        name=name,
    )(xa, xb, w)


def _gateup_kernel(x_ref, wg_ref, wu_ref, o_ref):
    x = x_ref[...]
    g = jnp.dot(x, wg_ref[...].astype(BF16), preferred_element_type=F32)
    u = jnp.dot(x, wu_ref[...].astype(BF16), preferred_element_type=F32)
    o_ref[...] = (g * jax.nn.sigmoid(g) * u).astype(o_ref.dtype)


def gateup(x, w_gate, w_up, layer, bm=2048, bf=256):
    m, k = x.shape
    f = w_gate.shape[2]
    bm, bf = min(bm, m), min(bf, f)
    assert m % bm == 0 and f % bf == 0
    return pl.pallas_call(
        _gateup_kernel,
        grid=(m // bm, f // bf),
        in_specs=[_x_spec(bm, k, True), _w_spec(k, bf, layer), _w_spec(k, bf, layer)],
        out_specs=pl.BlockSpec((bm, bf), lambda i, j: (i, j)),
        out_shape=jax.ShapeDtypeStruct((m, f), BF16),
        compiler_params=_cparams(2, 56),
        name="ffn_gateup",
    )(x, w_gate, w_up)


def ffn(u, w_gate, w_up, w_down, layer):
    hidden = gateup(u, w_gate, w_up, layer)
    return mm(hidden, w_down.astype(BF16), layer, BF16, bm=512, bn=512, name="ffn_down")


def _rope_tables(pos_ref, invf_ref):
    ang = pos_ref[...].astype(F32) * invf_ref[...]
    lane = lax.broadcasted_iota(jnp.int32, ang.shape, 1)
    half = MLA_ROPE_DIM // 2
    cos_t = jnp.where(lane < MLA_ROPE_DIM, jnp.cos(ang), 0.0)
    sin = jnp.sin(ang)
    sin_t = jnp.where(lane < half, -sin, jnp.where(lane < MLA_ROPE_DIM, sin, 0.0))
    return cos_t, sin_t, lane < half


def _rope_apply(t, cos_t, sin_t, first_half):
    half = MLA_ROPE_DIM // 2
    partner = jnp.where(first_half, pltpu.roll(t, LANES - half, 1), pltpu.roll(t, half, 1))
    return t * cos_t + partner * sin_t


def _inv_freq_lanes():
    half = MLA_ROPE_DIM // 2
    inv_freq = 1.0 / (ROPE_THETA ** (jnp.arange(half, dtype=F32) / half))
    return jnp.concatenate([inv_freq, inv_freq, jnp.zeros((LANES - MLA_ROPE_DIM,), F32)]).reshape(1, LANES)


def _qb_kernel(cq_ref, g_ref, w_ref, pos_ref, invf_ref, o_ref, *, scale):
    cn = _rms(cq_ref[...].astype(F32), g_ref[...]).astype(BF16)
    q = jnp.dot(cn, w_ref[...], preferred_element_type=F32)
    cos_t, sin_t, first_half = _rope_tables(pos_ref, invf_ref)
    for h in range(MLA_HEADS):
        c0 = h * MLA_QK_PAD
        o_ref[:, c0:c0 + MLA_NOPE_DIM] = (q[:, c0:c0 + MLA_NOPE_DIM] * scale).astype(o_ref.dtype)
        t = q[:, c0 + MLA_NOPE_DIM:c0 + MLA_QK_PAD]
        o_ref[:, c0 + MLA_NOPE_DIM:c0 + MLA_QK_PAD] = (
            _rope_apply(t, cos_t, sin_t, first_half) * scale).astype(o_ref.dtype)


def q_b_proj(qkv, g_q_a, w_q_b, pos, invf, bm=512):
    m = qkv.shape[0]
    bm = min(bm, m)
    h, dq = MLA_HEADS, MLA_NOPE_DIM + MLA_ROPE_DIM
    w = w_q_b.reshape(MLA_Q_RANK, h, dq)
    w = jnp.pad(w, ((0, 0), (0, 0), (0, MLA_QK_PAD - dq))).reshape(MLA_Q_RANK, h * MLA_QK_PAD).astype(BF16)
    cq_block = (3 * NA_WIDTH) // MLA_Q_RANK
    return pl.pallas_call(
        functools.partial(_qb_kernel, scale=float(dq) ** -0.5),
        grid=(m // bm,),
        in_specs=[pl.BlockSpec((bm, MLA_Q_RANK), lambda i: (i, cq_block)),
                  pl.BlockSpec((1, MLA_Q_RANK), lambda i: (0, 0)),
                  pl.BlockSpec((MLA_Q_RANK, h * MLA_QK_PAD), lambda i: (0, 0)),
                  pl.BlockSpec((bm, 1), lambda i: (i, 0)),
                  pl.BlockSpec((1, LANES), lambda i: (0, 0))],
        out_specs=pl.BlockSpec((bm, h * MLA_QK_PAD), lambda i: (i, 0)),
        out_shape=jax.ShapeDtypeStruct((m, h * MLA_QK_PAD), BF16),
        compiler_params=_cparams(1, 56),
        name="mla_q_proj",
    )(qkv, _row(g_q_a), w, pos, invf)


def _kvb_kernel(ckv_ref, g_ref, w_ref, u_ref, wt_ref, pos_ref, invf_ref, k_ref, v_ref):
    cn = _rms(ckv_ref[...].astype(F32), g_ref[...]).astype(BF16)
    kv = jnp.dot(cn, w_ref[...], preferred_element_type=F32)
    kr = jnp.dot(u_ref[...], wt_ref[...], preferred_element_type=F32)
    cos_t, sin_t, first_half = _rope_tables(pos_ref, invf_ref)
    kpe = _rope_apply(kr, cos_t, sin_t, first_half).astype(k_ref.dtype)
    for h in range(MLA_HEADS):
        c0 = h * (MLA_NOPE_DIM + MLA_V_DIM)
        k_ref[:, h * MLA_QK_PAD:h * MLA_QK_PAD + MLA_NOPE_DIM] = kv[:, c0:c0 + MLA_NOPE_DIM].astype(k_ref.dtype)
        k_ref[:, h * MLA_QK_PAD + MLA_NOPE_DIM:(h + 1) * MLA_QK_PAD] = kpe
        v_ref[:, h * MLA_V_DIM:(h + 1) * MLA_V_DIM] = kv[:, c0 + MLA_NOPE_DIM:c0 + MLA_NOPE_DIM + MLA_V_DIM].astype(v_ref.dtype)


def kv_b_proj(qkv, g_kv_a, w_kv_b, u, w_rope_in, pos, invf, bm=512):
    m, d = u.shape
    bm = min(bm, m)
    h = MLA_HEADS
    ckv_block = (3 * NA_WIDTH + MLA_Q_RANK) // MLA_KV_RANK
    wt = jnp.pad(w_rope_in, ((0, 0), (0, LANES - MLA_ROPE_DIM))).astype(BF16)
    kv_w = h * (MLA_NOPE_DIM + MLA_V_DIM)
    return pl.pallas_call(
        _kvb_kernel,
        grid=(m // bm,),
        in_specs=[pl.BlockSpec((bm, MLA_KV_RANK), lambda i: (i, ckv_block)),
                  pl.BlockSpec((1, MLA_KV_RANK), lambda i: (0, 0)),
                  pl.BlockSpec((MLA_KV_RANK, kv_w), lambda i: (0, 0)),
                  pl.BlockSpec((bm, d), lambda i: (i, 0)),
                  pl.BlockSpec((d, LANES), lambda i: (0, 0)),
                  pl.BlockSpec((bm, 1), lambda i: (i, 0)),
                  pl.BlockSpec((1, LANES), lambda i: (0, 0))],
        out_specs=[pl.BlockSpec((bm, h * MLA_QK_PAD), lambda i: (i, 0)),
                   pl.BlockSpec((bm, h * MLA_V_DIM), lambda i: (i, 0))],
        out_shape=[jax.ShapeDtypeStruct((m, h * MLA_QK_PAD), BF16),
                   jax.ShapeDtypeStruct((m, h * MLA_V_DIM), BF16)],
        compiler_params=_cparams(1, 56),
        name="mla_kv_proj",
    )(qkv, _row(g_kv_a), w_kv_b.astype(BF16), u, wt, pos, invf)


def _softmax_pv(s, v):
    m = jnp.max(s, axis=-1, keepdims=True)
    p = jnp.exp(s - m)
    l = jnp.sum(p, axis=-1, keepdims=True)
    return jnp.dot(p.astype(BF16), v, preferred_element_type=F32) / l


def _mla_kernel(q_ref, k_ref, v_ref, o_ref, *, heads):
    scores = []
    for h in range(heads):
        qk = slice(h * MLA_QK_PAD, (h + 1) * MLA_QK_PAD)
        scores.append(lax.dot_general(q_ref[:, qk], k_ref[:, qk], (((1,), (1,)), ((), ())),
                                      preferred_element_type=F32))
    for h in range(heads):
        vs = slice(h * MLA_V_DIM, (h + 1) * MLA_V_DIM)
        o_ref[:, vs] = _softmax_pv(scores[h], v_ref[:, vs]).astype(o_ref.dtype)


def mla_attention(q, k, v, batch, seq, bq=512, heads_per_step=4):
    bq = min(bq, seq)
    nq = seq // bq
    g = heads_per_step
    return pl.pallas_call(
        functools.partial(_mla_kernel, heads=g),
        grid=(batch, MLA_HEADS // g, nq),
        in_specs=[pl.BlockSpec((bq, g * MLA_QK_PAD), lambda b, h, i: (b * nq + i, h)),
                  pl.BlockSpec((seq, g * MLA_QK_PAD), lambda b, h, i: (b, h)),
                  pl.BlockSpec((seq, g * MLA_V_DIM), lambda b, h, i: (b, h))],
        out_specs=pl.BlockSpec((bq, g * MLA_V_DIM), lambda b, h, i: (b * nq + i, h)),
        out_shape=jax.ShapeDtypeStruct((batch * seq, MLA_WIDTH), BF16),
        compiler_params=_cparams(3, 48),
        name="mla_attention",
    )(q, k, v)


NA_BIAS_ROWS = 2 * NA_KH - 1
NA_BIAS_COLS = 2 * NA_KW - 1
NA_BIAS_PAIRS = NA_BIAS_ROWS - 1


def _na_bias_kernel(rpb_ref, o_ref):
    h = pl.program_id(0)
    row = lax.broadcasted_iota(jnp.int32, (GRID_W, LANES), 0)
    lane = lax.broadcasted_iota(jnp.int32, (GRID_W, LANES), 1)
    col_idx = (lane & (GRID_W - 1)) - row + (NA_KW - 1)
    left = lane < GRID_W

    def body(d, carry):
        base = (h * NA_BIAS_ROWS + d) * NA_BIAS_COLS
        acc = jnp.zeros((GRID_W, LANES), F32)
        for j in range(NA_BIAS_COLS):
            val = jnp.where(left, rpb_ref[base + j], rpb_ref[base + NA_BIAS_COLS + j])
            acc = acc + jnp.where(col_idx == j, val, 0.0)
        o_ref[0, d] = acc
        return carry

    lax.fori_loop(0, NA_BIAS_PAIRS, body, 0)


def na_bias_table(rpb):
    return pl.pallas_call(
        _na_bias_kernel,
        grid=(NA_HEADS,),
        in_specs=[pl.BlockSpec(memory_space=pltpu.SMEM)],
        out_specs=pl.BlockSpec((1, NA_BIAS_PAIRS, GRID_W, LANES), lambda h: (h, 0, 0, 0)),
        out_shape=jax.ShapeDtypeStruct((NA_HEADS, NA_BIAS_PAIRS, GRID_W, LANES), F32),
        compiler_params=_cparams(1, 16),
        name="na_bias_table",
    )(rpb.reshape(-1).astype(F32))


def _na_kernel(q_ref, k_ref, v_ref, bias_ref, o_ref, *, rows, heads, rows_per_step, scale):
    nk = NA_KH * GRID_W
    qc = lax.broadcasted_iota(jnp.int32, (GRID_W, nk), 0)
    kc = lax.broadcasted_iota(jnp.int32, (GRID_W, nk), 1) & (GRID_W - 1)
    cstart = jnp.clip(qc - NA_KW // 2, 0, GRID_W - NA_KW)
    in_window = (kc >= cstart) & (kc < cstart + NA_KW)

    def body(step, carry):
        chains = []
        for rr in range(rows_per_step):
            r = step * rows_per_step + rr
            rs = jnp.clip(r - NA_KH // 2, 0, rows - NA_KH)
            d0 = rs - r + (NA_KH - 1)
            q0 = pl.multiple_of(r * GRID_W, GRID_W)
            k0 = pl.multiple_of(rs * GRID_W, GRID_W)
            for h in range(heads):
                cols = slice(h * NA_HEAD_DIM, (h + 1) * NA_HEAD_DIM)
                q = q_ref[pl.ds(q0, GRID_W), cols]
                k = k_ref[pl.ds(k0, nk), cols]
                s = lax.dot_general(q, k, (((1,), (1,)), ((), ())), preferred_element_type=F32) * scale
                bias = jnp.concatenate([bias_ref[h, d0 + 2 * p] for p in range(NA_KH // 2)], axis=1)
                chains.append((q0, k0, cols, jnp.where(in_window, s + bias, NEG_INF)))
        for q0, k0, cols, s in chains:
            v = v_ref[pl.ds(k0, nk), cols]
            o_ref[pl.ds(q0, GRID_W), cols] = _softmax_pv(s, v).astype(o_ref.dtype)
        return carry

    lax.fori_loop(0, rows // rows_per_step, body, 0)


def na_attention(qkv, bias, batch, seq, heads_per_step=4):
    g = heads_per_step
    w = g * NA_HEAD_DIM
    groups = NA_HEADS // g
    rows = seq // GRID_W
    return pl.pallas_call(
        functools.partial(_na_kernel, rows=rows, heads=g, rows_per_step=2, scale=float(NA_HEAD_DIM) ** -0.5),
        grid=(batch, groups),
        in_specs=[pl.BlockSpec((seq, w), lambda b, j: (b, j)),
                  pl.BlockSpec((seq, w), lambda b, j: (b, groups + j)),
                  pl.BlockSpec((seq, w), lambda b, j: (b, 2 * groups + j)),
                  pl.BlockSpec((g, NA_BIAS_PAIRS, GRID_W, LANES), lambda b, j: (j, 0, 0, 0))],
        out_specs=pl.BlockSpec((seq, w), lambda b, j: (b, j)),
        out_shape=jax.ShapeDtypeStruct((batch * seq, NA_WIDTH), BF16),
        compiler_params=_cparams(2, 32),
        name="na_attention",
    )(qkv, qkv, qkv, bias)


def _mem_attn_kernel(q_ref, kv_ref, wo_ref, h_ref, gp_ref, gn_ref, ho_ref, uo_ref, *, scale):
    outs = []
    for h in range(MEM_HEADS):
        q = q_ref[:, h * MEM_HEAD_DIM:(h + 1) * MEM_HEAD_DIM]
        k = kv_ref[:, 2 * h * MEM_HEAD_DIM:(2 * h + 1) * MEM_HEAD_DIM]
        v = kv_ref[:, (2 * h + 1) * MEM_HEAD_DIM:(2 * h + 2) * MEM_HEAD_DIM]
        s = lax.dot_general(q, k, (((1,), (1,)), ((), ())), preferred_element_type=F32) * scale
        outs.append(_softmax_pv(s, v).astype(BF16))
    o = jnp.concatenate(outs, axis=1)
    a = jnp.dot(o, wo_ref[...].astype(BF16), preferred_element_type=F32)
    h = h_ref[...] + _rms(a, gp_ref[...])
    ho_ref[...] = h
    uo_ref[...] = _rms(h, gn_ref[...]).astype(uo_ref.dtype)


def mem_attention(q, kv, w_o, layer, h, g_post, g_next, batch, seq, bq=256):
    mem_len = kv.shape[0] // batch
    d = w_o.shape[2]
    bq = min(bq, seq)
    nq = seq // bq
    width = MEM_HEADS * MEM_HEAD_DIM
    row = pl.BlockSpec((bq, d), lambda b, i: (b * nq + i, 0))
    vec = pl.BlockSpec((1, d), lambda b, i: (0, 0))
    return pl.pallas_call(
        functools.partial(_mem_attn_kernel, scale=float(MEM_HEAD_DIM) ** -0.5),
        grid=(batch, nq),
        in_specs=[pl.BlockSpec((bq, width), lambda b, i: (b * nq + i, 0)),
                  pl.BlockSpec((mem_len, 2 * width), lambda b, i: (b, 0)),
                  pl.BlockSpec((None, width, d), lambda b, i: (layer, 0, 0)),
                  row, vec, vec],
        out_specs=[row, row],
        out_shape=[jax.ShapeDtypeStruct((batch * seq, d), F32), jax.ShapeDtypeStruct((batch * seq, d), BF16)],
        compiler_params=_cparams(2, 56),
        name="mem_attention",
    )(q, kv, w_o, h, _row(g_post), _row(g_next))


def kernel(x, mem, positions, ffn1_w_gate, ffn1_w_up, ffn1_w_down, g_ffn1, w_in, g_q_a, w_q_b, g_kv_a, w_kv_b, na_rpb, w_out, g_mix, g_mem_in, w_mem_q, w_mem_kv, w_mem_o, g_mem_attn, ffn2_w_gate, ffn2_w_up, ffn2_w_down, g_ffn2, g_final):
    batch, seq, d = x.shape
    m = batch * seq
    depth = ffn1_w_gate.shape[0]
    pos = positions.reshape(m, 1).astype(jnp.int32)
    invf = _inv_freq_lanes()
    mem2 = mem.reshape(-1, d)

    h = x.reshape(m, d)
    u = norm_cast(h, g_ffn1[0, 0])
    out = None
    for l in range(depth):
        f = ffn(u, ffn1_w_gate, ffn1_w_up, ffn1_w_down, l)
        h, u = resid_norm(h, f, g_ffn1[l, 1], FFN_RES_WEIGHT, g_mix[l, 0])

        qkv = mm(u, w_in, l, BF16, bm=2048, bn=512, n_cols=IN_PROJ_MAIN, single_buffer_x=True, name="in_proj")
        q_cat = q_b_proj(qkv, g_q_a[l], w_q_b[l], pos, invf)
        k_cat, v_mla = kv_b_proj(qkv, g_kv_a[l], w_kv_b[l], u, w_in[l, :, IN_PROJ_MAIN:], pos, invf)
        o_mla = mla_attention(q_cat, k_cat, v_mla, batch, seq)
        o_na = na_attention(qkv, na_bias_table(na_rpb[l]), batch, seq)
        o = mm2(o_na, o_mla, w_out, l, BF16, bm=2048, bn=512, single_buffer_x=True, name="out_proj")
        h, u = resid_norm(h, o, g_mix[l, 1], 1.0, g_mem_attn[l, 0])

        mem_n = norm_cast(mem2, g_mem_in[l])
        kv_mem = mm(mem_n, w_mem_kv, l, BF16, bm=1024, bn=512, name="mem_kv_proj")
        q_mem = mm(u, w_mem_q, l, BF16, bm=1024, bn=512, name="mem_q_proj")
        h, u = mem_attention(q_mem, kv_mem, w_mem_o, l, h, g_mem_attn[l, 1], g_ffn2[l, 0], batch, seq)

        f = ffn(u, ffn2_w_gate, ffn2_w_up, ffn2_w_down, l)
        if l + 1 < depth:
            h = resid_final(h, f, g_ffn2[l, 1], FFN_RES_WEIGHT, g_final[l])
            u = norm_cast(h, g_ffn1[l + 1, 0])
        else:
            out = resid_final(h, f, g_ffn2[l, 1], FFN_RES_WEIGHT, g_final[l])
    return out.reshape(batch, seq, d)
```

```python
import functools

import jax
import jax.numpy as jnp
from jax import lax
from jax.experimental import pallas as pl
from jax.experimental.pallas import tpu as pltpu

F32 = jnp.float32
BF16 = jnp.bfloat16

GRID_W = 64
NA_HEADS = 16
NA_HEAD_DIM = 128
NA_KH = 8
NA_KW = 16
MLA_HEADS = 16
MLA_Q_RANK = 1024
MLA_KV_RANK = 512
MLA_NOPE_DIM = 128
MLA_ROPE_DIM = 64
MLA_V_DIM = 128
MLA_QK_PAD = 256
MLA_VT_ROWS = MLA_V_DIM + 16
LOG2_E = 1.4426950408889634
ROPE_THETA = 10000.0
MEM_HEADS = 4
MEM_HEAD_DIM = 128
FFN_RES_WEIGHT = 0.5
NORM_EPS = 1e-6
NEG_INF = -1e30
NA_WIDTH = NA_HEADS * NA_HEAD_DIM
MLA_WIDTH = MLA_HEADS * MLA_V_DIM
IN_PROJ_MAIN = 3 * NA_WIDTH + MLA_Q_RANK + MLA_KV_RANK
LANES = 128
MIB = 1024 * 1024


def _cparams(n_axes, vmem_mib):
    return pltpu.CompilerParams(dimension_semantics=("parallel",) * n_axes,
                                vmem_limit_bytes=vmem_mib * MIB)


def _rms(x, g):
    return x * lax.rsqrt(jnp.mean(x * x, axis=-1, keepdims=True) + NORM_EPS) * g


def _row(v):
    return v.reshape(1, -1).astype(F32)


def _norm_cast_kernel(x_ref, g_ref, o_ref):
    o_ref[...] = _rms(x_ref[...], g_ref[...]).astype(o_ref.dtype)


def norm_cast(x, g, bm=256):
    m, d = x.shape
    bm = min(bm, m)
    return pl.pallas_call(
        _norm_cast_kernel,
        grid=(m // bm,),
        in_specs=[pl.BlockSpec((bm, d), lambda i: (i, 0)), pl.BlockSpec((1, d), lambda i: (0, 0))],
        out_specs=pl.BlockSpec((bm, d), lambda i: (i, 0)),
        out_shape=jax.ShapeDtypeStruct((m, d), BF16),
        compiler_params=_cparams(1, 32),
        name="norm_cast",
    )(x, _row(g))


def _resid_norm_kernel(h_ref, f_ref, gp_ref, gn_ref, ho_ref, uo_ref, *, weight):
    h = h_ref[...] + weight * _rms(f_ref[...].astype(F32), gp_ref[...])
    ho_ref[...] = h
    uo_ref[...] = _rms(h, gn_ref[...]).astype(uo_ref.dtype)


def resid_norm(h, f, g_post, weight, g_next, bm=256):
    m, d = h.shape
    bm = min(bm, m)
    row = pl.BlockSpec((bm, d), lambda i: (i, 0))
    vec = pl.BlockSpec((1, d), lambda i: (0, 0))
    return pl.pallas_call(
        functools.partial(_resid_norm_kernel, weight=weight),
        grid=(m // bm,),
        in_specs=[row, row, vec, vec],
        out_specs=[row, row],
        out_shape=[jax.ShapeDtypeStruct((m, d), F32), jax.ShapeDtypeStruct((m, d), BF16)],
        compiler_params=_cparams(1, 48),
        name="resid_norm",
    )(h, f, _row(g_post), _row(g_next))


def _resid_final_kernel(h_ref, f_ref, gp_ref, gn_ref, o_ref, *, weight):
    h = h_ref[...] + weight * _rms(f_ref[...].astype(F32), gp_ref[...])
    o_ref[...] = _rms(h, gn_ref[...])


def resid_final(h, f, g_post, weight, g_final, bm=256):
    m, d = h.shape
    bm = min(bm, m)
    row = pl.BlockSpec((bm, d), lambda i: (i, 0))
    vec = pl.BlockSpec((1, d), lambda i: (0, 0))
    return pl.pallas_call(
        functools.partial(_resid_final_kernel, weight=weight),
        grid=(m // bm,),
        in_specs=[row, row, vec, vec],
        out_specs=row,
        out_shape=jax.ShapeDtypeStruct((m, d), F32),
        compiler_params=_cparams(1, 48),
        name="resid_final",
    )(h, f, _row(g_post), _row(g_final))


def _x_spec(bm, k, single_buffer):
    if single_buffer:
        return pl.BlockSpec((bm, k), lambda i, j: (i, 0), pipeline_mode=pl.Buffered(1))
    return pl.BlockSpec((bm, k), lambda i, j: (i, 0))


def _w_spec(k, bn, layer):
    return pl.BlockSpec((None, k, bn), lambda i, j: (layer, 0, j))


def _mm_kernel(x_ref, w_ref, o_ref):
    o_ref[...] = jnp.dot(x_ref[...], w_ref[...].astype(BF16),
                         preferred_element_type=F32).astype(o_ref.dtype)


def mm(x, w, layer, out_dtype, bm, bn, n_cols=None, single_buffer_x=False, vmem_mib=56, name="mm"):
    m, k = x.shape
    n = w.shape[2] if n_cols is None else n_cols
    bm, bn = min(bm, m), min(bn, n)
    assert m % bm == 0 and n % bn == 0 and w.shape[1] == k
    return pl.pallas_call(
        _mm_kernel,
        grid=(m // bm, n // bn),
        in_specs=[_x_spec(bm, k, single_buffer_x), _w_spec(k, bn, layer)],
        out_specs=pl.BlockSpec((bm, bn), lambda i, j: (i, j)),
        out_shape=jax.ShapeDtypeStruct((m, n), out_dtype),
        compiler_params=_cparams(2, vmem_mib),
        name=name,
    )(x, w)


def _mm_nt_kernel(x_ref, wt_ref, o_ref):
    o_ref[...] = lax.dot_general(x_ref[...], wt_ref[...].astype(BF16), (((1,), (1,)), ((), ())),
                                 preferred_element_type=F32).astype(o_ref.dtype)


def mm_nt(x, wt, layer, out_dtype, bm, bn, n_cols=None, single_buffer_x=False, vmem_mib=56, name="mm_nt"):
    m, k = x.shape
    n = wt.shape[1] if n_cols is None else n_cols
    bm, bn = min(bm, m), min(bn, n)
    assert m % bm == 0 and n % bn == 0 and wt.shape[2] == k
    return pl.pallas_call(
        _mm_nt_kernel,
        grid=(m // bm, n // bn),
        in_specs=[_x_spec(bm, k, single_buffer_x), pl.BlockSpec((None, bn, k), lambda i, j: (layer, j, 0))],
        out_specs=pl.BlockSpec((bm, bn), lambda i, j: (i, j)),
        out_shape=jax.ShapeDtypeStruct((m, n), out_dtype),
        compiler_params=_cparams(2, vmem_mib),
        name=name,
    )(x, wt)


def _mm2_kernel(xa_ref, xb_ref, w_ref, o_ref):
    ka = xa_ref.shape[1]
    w = w_ref[...].astype(BF16)
    acc = jnp.dot(xa_ref[...], w[:ka], preferred_element_type=F32)
    acc = acc + jnp.dot(xb_ref[...], w[ka:], preferred_element_type=F32)
    o_ref[...] = acc.astype(o_ref.dtype)


def mm2(xa, xb, w, layer, out_dtype, bm, bn, single_buffer_x=False, vmem_mib=56, name="mm2"):
    m, ka = xa.shape
    kb = xb.shape[1]
    n = w.shape[2]
    bm, bn = min(bm, m), min(bn, n)
    assert m % bm == 0 and n % bn == 0 and w.shape[1] == ka + kb
    return pl.pallas_call(
        _mm2_kernel,
        grid=(m // bm, n // bn),
        in_specs=[_x_spec(bm, ka, single_buffer_x), _x_spec(bm, kb, single_buffer_x),
                  _w_spec(ka + kb, bn, layer)],
        out_specs=pl.BlockSpec((bm, bn), lambda i, j: (i, j)),
        out_shape=jax.ShapeDtypeStruct((m, n), out_dtype),
        compiler_params=_cparams(2, vmem_mib),
        name=name,
    )(xa, xb, w)


def _gateup_kernel(x_ref, wg_ref, wu_ref, o_ref):
    x = x_ref[...]
    g = jnp.dot(x, wg_ref[...].astype(BF16), preferred_element_type=F32)
    u = jnp.dot(x, wu_ref[...].astype(BF16), preferred_element_type=F32)
    o_ref[...] = (g * jax.nn.sigmoid(g) * u).astype(o_ref.dtype)


def gateup(x, w_gate, w_up, layer, bm=2048, bf=256):
    m, k = x.shape
    f = w_gate.shape[2]
    bm, bf = min(bm, m), min(bf, f)
    assert m % bm == 0 and f % bf == 0
    return pl.pallas_call(
        _gateup_kernel,
        grid=(m // bm, f // bf),
        in_specs=[_x_spec(bm, k, True), _w_spec(k, bf, layer), _w_spec(k, bf, layer)],
        out_specs=pl.BlockSpec((bm, bf), lambda i, j: (i, j)),
        out_shape=jax.ShapeDtypeStruct((m, f), BF16),
        compiler_params=_cparams(2, 56),
        name="ffn_gateup",
    )(x, w_gate, w_up)


def ffn(u, w_gate, w_up, w_down, layer):
    hidden = gateup(u, w_gate, w_up, layer)
    return mm(hidden, w_down, layer, BF16, bm=1024, bn=256, single_buffer_x=True, vmem_mib=60, name="ffn_down")


def _rope_tables(pos_ref, invf_ref):
    ang = pos_ref[...].astype(F32) * invf_ref[...]
    lane = lax.broadcasted_iota(jnp.int32, ang.shape, 1)
    half = MLA_ROPE_DIM // 2
    cos_t = jnp.where(lane < MLA_ROPE_DIM, jnp.cos(ang), 0.0)
    sin = jnp.sin(ang)
    sin_t = jnp.where(lane < half, -sin, jnp.where(lane < MLA_ROPE_DIM, sin, 0.0))
    return cos_t, sin_t, lane < half


def _rope_apply(t, cos_t, sin_t, first_half):
    half = MLA_ROPE_DIM // 2
    partner = jnp.where(first_half, pltpu.roll(t, LANES - half, 1), pltpu.roll(t, half, 1))
    return t * cos_t + partner * sin_t


def _inv_freq_lanes():
    half = MLA_ROPE_DIM // 2
    inv_freq = 1.0 / (ROPE_THETA ** (jnp.arange(half, dtype=F32) / half))
    return jnp.concatenate([inv_freq, inv_freq, jnp.zeros((LANES - MLA_ROPE_DIM,), F32)]).reshape(1, LANES)


def _qb_kernel(cq_ref, g_ref, w_ref, pos_ref, invf_ref, o_ref, *, scale):
    cn = _rms(cq_ref[...].astype(F32), g_ref[...]).astype(BF16)
    q = jnp.dot(cn, w_ref[...], preferred_element_type=F32)
    cos_t, sin_t, first_half = _rope_tables(pos_ref, invf_ref)
    for h in range(MLA_HEADS):
        c0 = h * MLA_QK_PAD
        o_ref[:, c0:c0 + MLA_NOPE_DIM] = (q[:, c0:c0 + MLA_NOPE_DIM] * scale).astype(o_ref.dtype)
        t = q[:, c0 + MLA_NOPE_DIM:c0 + MLA_QK_PAD]
        o_ref[:, c0 + MLA_NOPE_DIM:c0 + MLA_QK_PAD] = (
            _rope_apply(t, cos_t, sin_t, first_half) * scale).astype(o_ref.dtype)


def q_b_proj(qkv, g_q_a, w_q_b, pos, invf, bm=512):
    m = qkv.shape[0]
    bm = min(bm, m)
    h, dq = MLA_HEADS, MLA_NOPE_DIM + MLA_ROPE_DIM
    w = w_q_b.reshape(MLA_Q_RANK, h, dq)
    w = jnp.pad(w, ((0, 0), (0, 0), (0, MLA_QK_PAD - dq))).reshape(MLA_Q_RANK, h * MLA_QK_PAD).astype(BF16)
    cq_block = (3 * NA_WIDTH) // MLA_Q_RANK
    return pl.pallas_call(
        functools.partial(_qb_kernel, scale=LOG2_E * float(dq) ** -0.5),
        grid=(m // bm,),
        in_specs=[pl.BlockSpec((bm, MLA_Q_RANK), lambda i: (i, cq_block)),
                  pl.BlockSpec((1, MLA_Q_RANK), lambda i: (0, 0)),
                  pl.BlockSpec((MLA_Q_RANK, h * MLA_QK_PAD), lambda i: (0, 0)),
                  pl.BlockSpec((bm, 1), lambda i: (i, 0)),
                  pl.BlockSpec((1, LANES), lambda i: (0, 0))],
        out_specs=pl.BlockSpec((bm, h * MLA_QK_PAD), lambda i: (i, 0)),
        out_shape=jax.ShapeDtypeStruct((m, h * MLA_QK_PAD), BF16),
        compiler_params=_cparams(1, 56),
        name="mla_q_proj",
    )(qkv, _row(g_q_a), w, pos, invf)


def _kvb_kernel(ckv_ref, g_ref, wk_ref, wvt_ref, u_ref, wt_ref, pos_ref, invf_ref, k_ref, vt_ref):
    cn = _rms(ckv_ref[...].astype(F32), g_ref[...]).astype(BF16)
    kn = jnp.dot(cn, wk_ref[...], preferred_element_type=F32)
    kr = jnp.dot(u_ref[...], wt_ref[...], preferred_element_type=F32)
    cos_t, sin_t, first_half = _rope_tables(pos_ref, invf_ref)
    kpe = _rope_apply(kr, cos_t, sin_t, first_half).astype(k_ref.dtype)
    for h in range(MLA_HEADS):
        k_ref[:, h * MLA_QK_PAD:h * MLA_QK_PAD + MLA_NOPE_DIM] = (
            kn[:, h * MLA_NOPE_DIM:(h + 1) * MLA_NOPE_DIM].astype(k_ref.dtype))
        k_ref[:, h * MLA_QK_PAD + MLA_NOPE_DIM:(h + 1) * MLA_QK_PAD] = kpe
    vt = lax.dot_general(wvt_ref[...], cn, (((1,), (1,)), ((), ())), preferred_element_type=F32)
    ones = jnp.ones((MLA_VT_ROWS - MLA_V_DIM, vt.shape[1]), vt_ref.dtype)
    for h in range(MLA_HEADS):
        r0 = h * MLA_VT_ROWS
        vt_ref[r0:r0 + MLA_V_DIM, :] = vt[h * MLA_V_DIM:(h + 1) * MLA_V_DIM].astype(vt_ref.dtype)
        vt_ref[r0 + MLA_V_DIM:r0 + MLA_VT_ROWS, :] = ones


def kv_b_proj(qkv, g_kv_a, w_kv_b, u, w_rope_in, pos, invf, bm=512):
    m, d = u.shape
    bm = min(bm, m)
    h = MLA_HEADS
    ckv_block = (3 * NA_WIDTH + MLA_Q_RANK) // MLA_KV_RANK
    wt = jnp.pad(w_rope_in, ((0, 0), (0, LANES - MLA_ROPE_DIM))).astype(BF16)
    w3 = w_kv_b.reshape(MLA_KV_RANK, h, MLA_NOPE_DIM + MLA_V_DIM)
    w_k = w3[:, :, :MLA_NOPE_DIM].reshape(MLA_KV_RANK, h * MLA_NOPE_DIM).astype(BF16)
    w_vt = w3[:, :, MLA_NOPE_DIM:].reshape(MLA_KV_RANK, h * MLA_V_DIM).T.astype(BF16)
    return pl.pallas_call(
        _kvb_kernel,
        grid=(m // bm,),
        in_specs=[pl.BlockSpec((bm, MLA_KV_RANK), lambda i: (i, ckv_block)),
                  pl.BlockSpec((1, MLA_KV_RANK), lambda i: (0, 0)),
                  pl.BlockSpec((MLA_KV_RANK, h * MLA_NOPE_DIM), lambda i: (0, 0)),
                  pl.BlockSpec((h * MLA_V_DIM, MLA_KV_RANK), lambda i: (0, 0)),
                  pl.BlockSpec((bm, d), lambda i: (i, 0)),
                  pl.BlockSpec((d, LANES), lambda i: (0, 0)),
                  pl.BlockSpec((bm, 1), lambda i: (i, 0)),
                  pl.BlockSpec((1, LANES), lambda i: (0, 0))],
        out_specs=[pl.BlockSpec((bm, h * MLA_QK_PAD), lambda i: (i, 0)),
                   pl.BlockSpec((h * MLA_VT_ROWS, bm), lambda i: (0, i))],
        out_shape=[jax.ShapeDtypeStruct((m, h * MLA_QK_PAD), BF16),
                   jax.ShapeDtypeStruct((h * MLA_VT_ROWS, m), BF16)],
        compiler_params=_cparams(1, 56),
        name="mla_kv_proj",
    )(qkv, _row(g_kv_a), w_k, w_vt, u, wt, pos, invf)


def _softmax_pv(s, v):
    m = jnp.max(s, axis=-1, keepdims=True)
    p = jnp.exp(s - m)
    l = jnp.sum(p, axis=-1, keepdims=True)
    return jnp.dot(p.astype(BF16), v, preferred_element_type=F32) / l


def _mla_kernel(q_ref, k_ref, vt_ref, o_ref, *, heads):
    scores = []
    for h in range(heads):
        qk = slice(h * MLA_QK_PAD, (h + 1) * MLA_QK_PAD)
        scores.append(lax.dot_general(k_ref[:, qk], q_ref[:, qk], (((1,), (1,)), ((), ())),
                                      preferred_element_type=F32))
    for h in range(heads):
        s = scores[h]
        p = jnp.exp2(s - jnp.max(s, axis=0, keepdims=True)).astype(BF16)
        ot = jnp.dot(vt_ref[h * MLA_VT_ROWS:(h + 1) * MLA_VT_ROWS, :], p,
                     preferred_element_type=F32)
        o = ot[:MLA_V_DIM] / ot[MLA_V_DIM:MLA_V_DIM + 1]
        o_ref[:, h * MLA_V_DIM:(h + 1) * MLA_V_DIM] = o.T.astype(o_ref.dtype)


def mla_attention(q, k, vt, batch, seq, bq=512, heads_per_step=4):
    bq = min(bq, seq)
    nq = seq // bq
    g = heads_per_step
    return pl.pallas_call(
        functools.partial(_mla_kernel, heads=g),
        grid=(batch, MLA_HEADS // g, nq),
        in_specs=[pl.BlockSpec((bq, g * MLA_QK_PAD), lambda b, h, i: (b * nq + i, h)),
                  pl.BlockSpec((seq, g * MLA_QK_PAD), lambda b, h, i: (b, h)),
                  pl.BlockSpec((g * MLA_VT_ROWS, seq), lambda b, h, i: (h, b))],
        out_specs=pl.BlockSpec((bq, g * MLA_V_DIM), lambda b, h, i: (b * nq + i, h)),
        out_shape=jax.ShapeDtypeStruct((batch * seq, MLA_WIDTH), BF16),
        compiler_params=_cparams(3, 48),
        name="mla_attention",
    )(q, k, vt)


NA_BIAS_ROWS = 2 * NA_KH - 1
NA_BIAS_COLS = 2 * NA_KW - 1
NA_BIAS_PAIRS = NA_BIAS_ROWS - 1


def _na_bias_kernel(rpb_ref, o_ref):
    h = pl.program_id(0)
    row = lax.broadcasted_iota(jnp.int32, (GRID_W, LANES), 0)
    lane = lax.broadcasted_iota(jnp.int32, (GRID_W, LANES), 1)
    col_idx = (lane & (GRID_W - 1)) - row + (NA_KW - 1)
    left = lane < GRID_W

    def body(d, carry):
        base = (h * NA_BIAS_ROWS + d) * NA_BIAS_COLS
        acc = jnp.zeros((GRID_W, LANES), F32)
        for j in range(NA_BIAS_COLS):
            val = jnp.where(left, rpb_ref[base + j], rpb_ref[base + NA_BIAS_COLS + j])
            acc = acc + jnp.where(col_idx == j, val, 0.0)
        o_ref[0, d] = acc
        return carry

    lax.fori_loop(0, NA_BIAS_PAIRS, body, 0)


def na_bias_table(rpb):
    return pl.pallas_call(
        _na_bias_kernel,
        grid=(NA_HEADS,),
        in_specs=[pl.BlockSpec(memory_space=pltpu.SMEM)],
        out_specs=pl.BlockSpec((1, NA_BIAS_PAIRS, GRID_W, LANES), lambda h: (h, 0, 0, 0)),
        out_shape=jax.ShapeDtypeStruct((NA_HEADS, NA_BIAS_PAIRS, GRID_W, LANES), F32),
        compiler_params=_cparams(1, 16),
        name="na_bias_table",
    )(rpb.reshape(-1).astype(F32))


def _na_kernel(q_ref, k_ref, v_ref, bias_ref, o_ref, *, rows, heads, rows_per_step, scale):
    nk = NA_KH * GRID_W
    qc = lax.broadcasted_iota(jnp.int32, (GRID_W, nk), 0)
    kc = lax.broadcasted_iota(jnp.int32, (GRID_W, nk), 1) & (GRID_W - 1)
    cstart = jnp.clip(qc - NA_KW // 2, 0, GRID_W - NA_KW)
    in_window = (kc >= cstart) & (kc < cstart + NA_KW)

    def body(step, carry):
        chains = []
        for rr in range(rows_per_step):
            r = step * rows_per_step + rr
            rs = jnp.clip(r - NA_KH // 2, 0, rows - NA_KH)
            d0 = rs - r + (NA_KH - 1)
            q0 = pl.multiple_of(r * GRID_W, GRID_W)
            k0 = pl.multiple_of(rs * GRID_W, GRID_W)
            for h in range(heads):
                cols = slice(h * NA_HEAD_DIM, (h + 1) * NA_HEAD_DIM)
                q = q_ref[pl.ds(q0, GRID_W), cols]
                k = k_ref[pl.ds(k0, nk), cols]
                s = lax.dot_general(q, k, (((1,), (1,)), ((), ())), preferred_element_type=F32) * scale
                bias = jnp.concatenate([bias_ref[h, d0 + 2 * p] for p in range(NA_KH // 2)], axis=1)
                chains.append((q0, k0, cols, jnp.where(in_window, s + bias, NEG_INF)))
        for q0, k0, cols, s in chains:
            v = v_ref[pl.ds(k0, nk), cols]
            o_ref[pl.ds(q0, GRID_W), cols] = _softmax_pv(s, v).astype(o_ref.dtype)
        return carry

    lax.fori_loop(0, rows // rows_per_step, body, 0)


def na_attention(qkv, bias, batch, seq, heads_per_step=4):
    g = heads_per_step
    w = g * NA_HEAD_DIM
    groups = NA_HEADS // g
    rows = seq // GRID_W
    return pl.pallas_call(
        functools.partial(_na_kernel, rows=rows, heads=g, rows_per_step=2, scale=float(NA_HEAD_DIM) ** -0.5),
        grid=(batch, groups),
        in_specs=[pl.BlockSpec((seq, w), lambda b, j: (b, j)),
                  pl.BlockSpec((seq, w), lambda b, j: (b, groups + j)),
                  pl.BlockSpec((seq, w), lambda b, j: (b, 2 * groups + j)),
                  pl.BlockSpec((g, NA_BIAS_PAIRS, GRID_W, LANES), lambda b, j: (j, 0, 0, 0))],
        out_specs=pl.BlockSpec((seq, w), lambda b, j: (b, j)),
        out_shape=jax.ShapeDtypeStruct((batch * seq, NA_WIDTH), BF16),
        compiler_params=_cparams(2, 32),
        name="na_attention",
    )(qkv, qkv, qkv, bias)


def _mem_attn_kernel(q_ref, kv_ref, wo_ref, h_ref, gp_ref, gn_ref, ho_ref, uo_ref, *, scale):
    outs = []
    for h in range(MEM_HEADS):
        q = q_ref[:, h * MEM_HEAD_DIM:(h + 1) * MEM_HEAD_DIM]
        k = kv_ref[:, 2 * h * MEM_HEAD_DIM:(2 * h + 1) * MEM_HEAD_DIM]
        v = kv_ref[:, (2 * h + 1) * MEM_HEAD_DIM:(2 * h + 2) * MEM_HEAD_DIM]
        s = lax.dot_general(q, k, (((1,), (1,)), ((), ())), preferred_element_type=F32) * scale
        outs.append(_softmax_pv(s, v).astype(BF16))
    o = jnp.concatenate(outs, axis=1)
    a = jnp.dot(o, wo_ref[...].astype(BF16), preferred_element_type=F32)
    h = h_ref[...] + _rms(a, gp_ref[...])
    ho_ref[...] = h
    uo_ref[...] = _rms(h, gn_ref[...]).astype(uo_ref.dtype)


def mem_attention(q, kv, w_o, layer, h, g_post, g_next, batch, seq, bq=256):
    mem_len = kv.shape[0] // batch
    d = w_o.shape[2]
    bq = min(bq, seq)
    nq = seq // bq
    width = MEM_HEADS * MEM_HEAD_DIM
    row = pl.BlockSpec((bq, d), lambda b, i: (b * nq + i, 0))
    vec = pl.BlockSpec((1, d), lambda b, i: (0, 0))
    return pl.pallas_call(
        functools.partial(_mem_attn_kernel, scale=float(MEM_HEAD_DIM) ** -0.5),
        grid=(batch, nq),
        in_specs=[pl.BlockSpec((bq, width), lambda b, i: (b * nq + i, 0)),
                  pl.BlockSpec((mem_len, 2 * width), lambda b, i: (b, 0)),
                  pl.BlockSpec((None, width, d), lambda b, i: (layer, 0, 0)),
                  row, vec, vec],
        out_specs=[row, row],
        out_shape=[jax.ShapeDtypeStruct((batch * seq, d), F32), jax.ShapeDtypeStruct((batch * seq, d), BF16)],
        compiler_params=_cparams(2, 56),
        name="mem_attention",
    )(q, kv, w_o, h, _row(g_post), _row(g_next))


def kernel(x, mem, positions, ffn1_w_gate, ffn1_w_up, ffn1_w_down, g_ffn1, w_in, g_q_a, w_q_b, g_kv_a, w_kv_b, na_rpb, w_out, g_mix, g_mem_in, w_mem_q, w_mem_kv, w_mem_o, g_mem_attn, ffn2_w_gate, ffn2_w_up, ffn2_w_down, g_ffn2, g_final):
    batch, seq, d = x.shape
    m = batch * seq
    depth = ffn1_w_gate.shape[0]
    pos = positions.reshape(m, 1).astype(jnp.int32)
    invf = _inv_freq_lanes()
    mem2 = mem.reshape(-1, d)

    h = x.reshape(m, d)
    u = norm_cast(h, g_ffn1[0, 0])
    out = None
    for l in range(depth):
        f = ffn(u, ffn1_w_gate, ffn1_w_up, ffn1_w_down, l)
        h, u = resid_norm(h, f, g_ffn1[l, 1], FFN_RES_WEIGHT, g_mix[l, 0])

        qkv = mm_nt(u, jnp.swapaxes(w_in, 1, 2), l, BF16, bm=2048, bn=512, n_cols=IN_PROJ_MAIN,
                    single_buffer_x=True, name="in_proj")
        q_cat = q_b_proj(qkv, g_q_a[l], w_q_b[l], pos, invf)
        k_cat, v_mla = kv_b_proj(qkv, g_kv_a[l], w_kv_b[l], u, w_in[l, :, IN_PROJ_MAIN:], pos, invf)
        o_mla = mla_attention(q_cat, k_cat, v_mla, batch, seq)
        o_na = na_attention(qkv, na_bias_table(na_rpb[l]), batch, seq)
        o = mm2(o_na, o_mla, w_out, l, BF16, bm=2048, bn=512, single_buffer_x=True, name="out_proj")
        h, u = resid_norm(h, o, g_mix[l, 1], 1.0, g_mem_attn[l, 0])

        mem_n = norm_cast(mem2, g_mem_in[l])
        kv_mem = mm(mem_n, w_mem_kv, l, BF16, bm=1024, bn=512, name="mem_kv_proj")
        q_mem = mm(u, w_mem_q, l, BF16, bm=1024, bn=512, name="mem_q_proj")
        h, u = mem_attention(q_mem, kv_mem, w_mem_o, l, h, g_mem_attn[l, 1], g_ffn2[l, 0], batch, seq)

        f = ffn(u, ffn2_w_gate, ffn2_w_up, ffn2_w_down, l)
        if l + 1 < depth:
            h = resid_final(h, f, g_ffn2[l, 1], FFN_RES_WEIGHT, g_final[l])
            u = norm_cast(h, g_ffn1[l + 1, 0])
        else:
            out = resid_final(h, f, g_ffn2[l, 1], FFN_RES_WEIGHT, g_final[l])
    return out.reshape(batch, seq, d)
```

```python
import functools

import jax
import jax.numpy as jnp
from jax import lax
from jax.experimental import pallas as pl
from jax.experimental.pallas import tpu as pltpu

F32 = jnp.float32
BF16 = jnp.bfloat16

GRID_W = 64
NA_HEADS = 16
NA_HEAD_DIM = 128
NA_KH = 8
NA_KW = 16
MLA_HEADS = 16
MLA_Q_RANK = 1024
MLA_KV_RANK = 512
MLA_NOPE_DIM = 128
MLA_ROPE_DIM = 64
MLA_V_DIM = 128
MLA_QK_PAD = 256
MLA_VT_ROWS = MLA_V_DIM + 16
LOG2_E = 1.4426950408889634
ROPE_THETA = 10000.0
MEM_HEADS = 4
MEM_HEAD_DIM = 128
FFN_RES_WEIGHT = 0.5
NORM_EPS = 1e-6
NEG_INF = -1e30
NA_WIDTH = NA_HEADS * NA_HEAD_DIM
MLA_WIDTH = MLA_HEADS * MLA_V_DIM
IN_PROJ_MAIN = 3 * NA_WIDTH + MLA_Q_RANK + MLA_KV_RANK
LANES = 128
MIB = 1024 * 1024


def _cparams(n_axes, vmem_mib):
    return pltpu.CompilerParams(dimension_semantics=("parallel",) * n_axes,
                                vmem_limit_bytes=vmem_mib * MIB)


def _rms(x, g):
    return x * lax.rsqrt(jnp.mean(x * x, axis=-1, keepdims=True) + NORM_EPS) * g


def _row(v):
    return v.reshape(1, -1).astype(F32)


def _norm_cast_kernel(x_ref, g_ref, o_ref):
    o_ref[...] = _rms(x_ref[...], g_ref[...]).astype(o_ref.dtype)


def norm_cast(x, g, bm=256):
    m, d = x.shape
    bm = min(bm, m)
    return pl.pallas_call(
        _norm_cast_kernel,
        grid=(m // bm,),
        in_specs=[pl.BlockSpec((bm, d), lambda i: (i, 0)), pl.BlockSpec((1, d), lambda i: (0, 0))],
        out_specs=pl.BlockSpec((bm, d), lambda i: (i, 0)),
        out_shape=jax.ShapeDtypeStruct((m, d), BF16),
        compiler_params=_cparams(1, 32),
        name="norm_cast",
    )(x, _row(g))


def _resid_norm_kernel(h_ref, f_ref, gp_ref, gn_ref, ho_ref, uo_ref, *, weight):
    h = h_ref[...] + weight * _rms(f_ref[...].astype(F32), gp_ref[...])
    ho_ref[...] = h
    uo_ref[...] = _rms(h, gn_ref[...]).astype(uo_ref.dtype)


def resid_norm(h, f, g_post, weight, g_next, bm=256):
    m, d = h.shape
    bm = min(bm, m)
    row = pl.BlockSpec((bm, d), lambda i: (i, 0))
    vec = pl.BlockSpec((1, d), lambda i: (0, 0))
    return pl.pallas_call(
        functools.partial(_resid_norm_kernel, weight=weight),
        grid=(m // bm,),
        in_specs=[row, row, vec, vec],
        out_specs=[row, row],
        out_shape=[jax.ShapeDtypeStruct((m, d), F32), jax.ShapeDtypeStruct((m, d), BF16)],
        compiler_params=_cparams(1, 48),
        name="resid_norm",
    )(h, f, _row(g_post), _row(g_next))


def _resid_final_kernel(h_ref, f_ref, gp_ref, gn_ref, o_ref, *, weight):
    h = h_ref[...] + weight * _rms(f_ref[...].astype(F32), gp_ref[...])
    o_ref[...] = _rms(h, gn_ref[...])


def resid_final(h, f, g_post, weight, g_final, bm=256):
    m, d = h.shape
    bm = min(bm, m)
    row = pl.BlockSpec((bm, d), lambda i: (i, 0))
    vec = pl.BlockSpec((1, d), lambda i: (0, 0))
    return pl.pallas_call(
        functools.partial(_resid_final_kernel, weight=weight),
        grid=(m // bm,),
        in_specs=[row, row, vec, vec],
        out_specs=row,
        out_shape=jax.ShapeDtypeStruct((m, d), F32),
        compiler_params=_cparams(1, 48),
        name="resid_final",
    )(h, f, _row(g_post), _row(g_final))


def _x_spec(bm, k, single_buffer):
    if single_buffer:
        return pl.BlockSpec((bm, k), lambda i, j: (i, 0), pipeline_mode=pl.Buffered(1))
    return pl.BlockSpec((bm, k), lambda i, j: (i, 0))


def _w_spec(k, bn, layer):
    return pl.BlockSpec((None, k, bn), lambda i, j: (layer, 0, j))


def _mm_kernel(x_ref, w_ref, o_ref):
    o_ref[...] = jnp.dot(x_ref[...], w_ref[...].astype(BF16),
                         preferred_element_type=F32).astype(o_ref.dtype)


def mm(x, w, layer, out_dtype, bm, bn, n_cols=None, single_buffer_x=False, vmem_mib=56, name="mm"):
    m, k = x.shape
    n = w.shape[2] if n_cols is None else n_cols
    bm, bn = min(bm, m), min(bn, n)
    assert m % bm == 0 and n % bn == 0 and w.shape[1] == k
    return pl.pallas_call(
        _mm_kernel,
        grid=(m // bm, n // bn),
        in_specs=[_x_spec(bm, k, single_buffer_x), _w_spec(k, bn, layer)],
        out_specs=pl.BlockSpec((bm, bn), lambda i, j: (i, j)),
        out_shape=jax.ShapeDtypeStruct((m, n), out_dtype),
        compiler_params=_cparams(2, vmem_mib),
        name=name,
    )(x, w)


def _mm_nt_kernel(x_ref, wt_ref, o_ref, *, scaled_blocks, scale):
    acc = lax.dot_general(x_ref[...], wt_ref[...].astype(BF16), (((1,), (1,)), ((), ())),
                          preferred_element_type=F32)
    if scaled_blocks:
        acc = acc * jnp.where(pl.program_id(1) < scaled_blocks, scale, 1.0)
    o_ref[...] = acc.astype(o_ref.dtype)


def mm_nt(x, wt, layer, out_dtype, bm, bn, n_cols=None, scaled_cols=0, scale=1.0, single_buffer_x=False,
          vmem_mib=56, name="mm_nt"):
    m, k = x.shape
    n = wt.shape[1] if n_cols is None else n_cols
    bm, bn = min(bm, m), min(bn, n)
    assert m % bm == 0 and n % bn == 0 and wt.shape[2] == k and scaled_cols % bn == 0
    return pl.pallas_call(
        functools.partial(_mm_nt_kernel, scaled_blocks=scaled_cols // bn, scale=scale),
        grid=(m // bm, n // bn),
        in_specs=[_x_spec(bm, k, single_buffer_x), pl.BlockSpec((None, bn, k), lambda i, j: (layer, j, 0))],
        out_specs=pl.BlockSpec((bm, bn), lambda i, j: (i, j)),
        out_shape=jax.ShapeDtypeStruct((m, n), out_dtype),
        compiler_params=_cparams(2, vmem_mib),
        name=name,
    )(x, wt)


def _mm2_kernel(xa_ref, xb_ref, w_ref, o_ref):
    ka = xa_ref.shape[1]
    w = w_ref[...].astype(BF16)
    acc = jnp.dot(xa_ref[...], w[:ka], preferred_element_type=F32)
    acc = acc + jnp.dot(xb_ref[...], w[ka:], preferred_element_type=F32)
    o_ref[...] = acc.astype(o_ref.dtype)


def mm2(xa, xb, w, layer, out_dtype, bm, bn, single_buffer_x=False, vmem_mib=56, name="mm2"):
    m, ka = xa.shape
    kb = xb.shape[1]
    n = w.shape[2]
    bm, bn = min(bm, m), min(bn, n)
    assert m % bm == 0 and n % bn == 0 and w.shape[1] == ka + kb
    return pl.pallas_call(
        _mm2_kernel,
        grid=(m // bm, n // bn),
        in_specs=[_x_spec(bm, ka, single_buffer_x), _x_spec(bm, kb, single_buffer_x),
                  _w_spec(ka + kb, bn, layer)],
        out_specs=pl.BlockSpec((bm, bn), lambda i, j: (i, j)),
        out_shape=jax.ShapeDtypeStruct((m, n), out_dtype),
        compiler_params=_cparams(2, vmem_mib),
        name=name,
    )(xa, xb, w)


def _gateup_kernel(x_ref, wg_ref, wu_ref, o_ref):
    x = x_ref[...]
    g = jnp.dot(x, wg_ref[...].astype(BF16), preferred_element_type=F32)
    u = jnp.dot(x, wu_ref[...].astype(BF16), preferred_element_type=F32)
    o_ref[...] = (g * jax.nn.sigmoid(g) * u).astype(o_ref.dtype)


def gateup(x, w_gate, w_up, layer, bm=2048, bf=256):
    m, k = x.shape
    f = w_gate.shape[2]
    bm, bf = min(bm, m), min(bf, f)
    assert m % bm == 0 and f % bf == 0
    return pl.pallas_call(
        _gateup_kernel,
        grid=(m // bm, f // bf),
        in_specs=[_x_spec(bm, k, True), _w_spec(k, bf, layer), _w_spec(k, bf, layer)],
        out_specs=pl.BlockSpec((bm, bf), lambda i, j: (i, j)),
        out_shape=jax.ShapeDtypeStruct((m, f), BF16),
        compiler_params=_cparams(2, 56),
        name="ffn_gateup",
    )(x, w_gate, w_up)


def ffn(u, w_gate, w_up, w_down, layer):
    hidden = gateup(u, w_gate, w_up, layer)
    return mm(hidden, w_down, layer, BF16, bm=1024, bn=256, single_buffer_x=True, vmem_mib=60, name="ffn_down")


def _rope_tables(pos_ref, invf_ref):
    ang = pos_ref[...].astype(F32) * invf_ref[...]
    lane = lax.broadcasted_iota(jnp.int32, ang.shape, 1)
    half = MLA_ROPE_DIM // 2
    cos_t = jnp.where(lane < MLA_ROPE_DIM, jnp.cos(ang), 0.0)
    sin = jnp.sin(ang)
    sin_t = jnp.where(lane < half, -sin, jnp.where(lane < MLA_ROPE_DIM, sin, 0.0))
    return cos_t, sin_t, lane < half


def _rope_apply(t, cos_t, sin_t, first_half):
    half = MLA_ROPE_DIM // 2
    partner = jnp.where(first_half, pltpu.roll(t, LANES - half, 1), pltpu.roll(t, half, 1))
    return t * cos_t + partner * sin_t


def _inv_freq_lanes():
    half = MLA_ROPE_DIM // 2
    inv_freq = 1.0 / (ROPE_THETA ** (jnp.arange(half, dtype=F32) / half))
    return jnp.concatenate([inv_freq, inv_freq, jnp.zeros((LANES - MLA_ROPE_DIM,), F32)]).reshape(1, LANES)


def _qb_kernel(cq_ref, g_ref, w_ref, pos_ref, invf_ref, o_ref, *, scale):
    cn = _rms(cq_ref[...].astype(F32), g_ref[...]).astype(BF16)
    q = jnp.dot(cn, w_ref[...], preferred_element_type=F32)
    cos_t, sin_t, first_half = _rope_tables(pos_ref, invf_ref)
    for h in range(MLA_HEADS):
        c0 = h * MLA_QK_PAD
        o_ref[:, c0:c0 + MLA_NOPE_DIM] = (q[:, c0:c0 + MLA_NOPE_DIM] * scale).astype(o_ref.dtype)
        t = q[:, c0 + MLA_NOPE_DIM:c0 + MLA_QK_PAD]
        o_ref[:, c0 + MLA_NOPE_DIM:c0 + MLA_QK_PAD] = (
            _rope_apply(t, cos_t, sin_t, first_half) * scale).astype(o_ref.dtype)


def q_b_proj(qkv, g_q_a, w_q_b, pos, invf, bm=512):
    m = qkv.shape[0]
    bm = min(bm, m)
    h, dq = MLA_HEADS, MLA_NOPE_DIM + MLA_ROPE_DIM
    w = w_q_b.reshape(MLA_Q_RANK, h, dq)
    w = jnp.pad(w, ((0, 0), (0, 0), (0, MLA_QK_PAD - dq))).reshape(MLA_Q_RANK, h * MLA_QK_PAD).astype(BF16)
    cq_block = (3 * NA_WIDTH) // MLA_Q_RANK
    return pl.pallas_call(
        functools.partial(_qb_kernel, scale=LOG2_E * float(dq) ** -0.5),
        grid=(m // bm,),
        in_specs=[pl.BlockSpec((bm, MLA_Q_RANK), lambda i: (i, cq_block)),
                  pl.BlockSpec((1, MLA_Q_RANK), lambda i: (0, 0)),
                  pl.BlockSpec((MLA_Q_RANK, h * MLA_QK_PAD), lambda i: (0, 0)),
                  pl.BlockSpec((bm, 1), lambda i: (i, 0)),
                  pl.BlockSpec((1, LANES), lambda i: (0, 0))],
        out_specs=pl.BlockSpec((bm, h * MLA_QK_PAD), lambda i: (i, 0)),
        out_shape=jax.ShapeDtypeStruct((m, h * MLA_QK_PAD), BF16),
        compiler_params=_cparams(1, 56),
        name="mla_q_proj",
    )(qkv, _row(g_q_a), w, pos, invf)


def _kvb_kernel(ckv_ref, g_ref, wk_ref, wvt_ref, u_ref, wt_ref, pos_ref, invf_ref, k_ref, vt_ref):
    cn = _rms(ckv_ref[...].astype(F32), g_ref[...]).astype(BF16)
    kn = jnp.dot(cn, wk_ref[...], preferred_element_type=F32)
    kr = jnp.dot(u_ref[...], wt_ref[...], preferred_element_type=F32)
    cos_t, sin_t, first_half = _rope_tables(pos_ref, invf_ref)
    kpe = _rope_apply(kr, cos_t, sin_t, first_half).astype(k_ref.dtype)
    for h in range(MLA_HEADS):
        k_ref[:, h * MLA_QK_PAD:h * MLA_QK_PAD + MLA_NOPE_DIM] = (
            kn[:, h * MLA_NOPE_DIM:(h + 1) * MLA_NOPE_DIM].astype(k_ref.dtype))
        k_ref[:, h * MLA_QK_PAD + MLA_NOPE_DIM:(h + 1) * MLA_QK_PAD] = kpe
    vt = lax.dot_general(wvt_ref[...], cn, (((1,), (1,)), ((), ())), preferred_element_type=F32)
    ones = jnp.ones((MLA_VT_ROWS - MLA_V_DIM, vt.shape[1]), vt_ref.dtype)
    for h in range(MLA_HEADS):
        r0 = h * MLA_VT_ROWS
        vt_ref[r0:r0 + MLA_V_DIM, :] = vt[h * MLA_V_DIM:(h + 1) * MLA_V_DIM].astype(vt_ref.dtype)
        vt_ref[r0 + MLA_V_DIM:r0 + MLA_VT_ROWS, :] = ones


def kv_b_proj(qkv, g_kv_a, w_kv_b, u, w_rope_in, pos, invf, bm=512):
    m, d = u.shape
    bm = min(bm, m)
    h = MLA_HEADS
    ckv_block = (3 * NA_WIDTH + MLA_Q_RANK) // MLA_KV_RANK
    wt = jnp.pad(w_rope_in, ((0, 0), (0, LANES - MLA_ROPE_DIM))).astype(BF16)
    w3 = w_kv_b.reshape(MLA_KV_RANK, h, MLA_NOPE_DIM + MLA_V_DIM)
    w_k = w3[:, :, :MLA_NOPE_DIM].reshape(MLA_KV_RANK, h * MLA_NOPE_DIM).astype(BF16)
    w_vt = w3[:, :, MLA_NOPE_DIM:].reshape(MLA_KV_RANK, h * MLA_V_DIM).T.astype(BF16)
    return pl.pallas_call(
        _kvb_kernel,
        grid=(m // bm,),
        in_specs=[pl.BlockSpec((bm, MLA_KV_RANK), lambda i: (i, ckv_block)),
                  pl.BlockSpec((1, MLA_KV_RANK), lambda i: (0, 0)),
                  pl.BlockSpec((MLA_KV_RANK, h * MLA_NOPE_DIM), lambda i: (0, 0)),
                  pl.BlockSpec((h * MLA_V_DIM, MLA_KV_RANK), lambda i: (0, 0)),
                  pl.BlockSpec((bm, d), lambda i: (i, 0)),
                  pl.BlockSpec((d, LANES), lambda i: (0, 0)),
                  pl.BlockSpec((bm, 1), lambda i: (i, 0)),
                  pl.BlockSpec((1, LANES), lambda i: (0, 0))],
        out_specs=[pl.BlockSpec((bm, h * MLA_QK_PAD), lambda i: (i, 0)),
                   pl.BlockSpec((h * MLA_VT_ROWS, bm), lambda i: (0, i))],
        out_shape=[jax.ShapeDtypeStruct((m, h * MLA_QK_PAD), BF16),
                   jax.ShapeDtypeStruct((h * MLA_VT_ROWS, m), BF16)],
        compiler_params=_cparams(1, 56),
        name="mla_kv_proj",
    )(qkv, _row(g_kv_a), w_k, w_vt, u, wt, pos, invf)


def _softmax_pv(s, v):
    m = jnp.max(s, axis=-1, keepdims=True)
    p = jnp.exp(s - m)
    l = jnp.sum(p, axis=-1, keepdims=True)
    return jnp.dot(p.astype(BF16), v, preferred_element_type=F32) / l


def _mla_kernel(q_ref, k_ref, vt_ref, o_ref, *, heads):
    scores = []
    for h in range(heads):
        qk = slice(h * MLA_QK_PAD, (h + 1) * MLA_QK_PAD)
        scores.append(lax.dot_general(k_ref[:, qk], q_ref[:, qk], (((1,), (1,)), ((), ())),
                                      preferred_element_type=F32))
    for h in range(heads):
        s = scores[h]
        p = jnp.exp2(s - jnp.max(s, axis=0, keepdims=True)).astype(BF16)
        ot = jnp.dot(vt_ref[h * MLA_VT_ROWS:(h + 1) * MLA_VT_ROWS, :], p,
                     preferred_element_type=F32)
        o = ot[:MLA_V_DIM] / ot[MLA_V_DIM:MLA_V_DIM + 1]
        o_ref[:, h * MLA_V_DIM:(h + 1) * MLA_V_DIM] = o.T.astype(o_ref.dtype)


def mla_attention(q, k, vt, batch, seq, bq=512, heads_per_step=4):
    bq = min(bq, seq)
    nq = seq // bq
    g = heads_per_step
    return pl.pallas_call(
        functools.partial(_mla_kernel, heads=g),
        grid=(batch, MLA_HEADS // g, nq),
        in_specs=[pl.BlockSpec((bq, g * MLA_QK_PAD), lambda b, h, i: (b * nq + i, h)),
                  pl.BlockSpec((seq, g * MLA_QK_PAD), lambda b, h, i: (b, h)),
                  pl.BlockSpec((g * MLA_VT_ROWS, seq), lambda b, h, i: (h, b))],
        out_specs=pl.BlockSpec((bq, g * MLA_V_DIM), lambda b, h, i: (b * nq + i, h)),
        out_shape=jax.ShapeDtypeStruct((batch * seq, MLA_WIDTH), BF16),
        compiler_params=_cparams(3, 48),
        name="mla_attention",
    )(q, k, vt)


NA_BIAS_ROWS = 2 * NA_KH - 1
NA_BIAS_COLS = 2 * NA_KW - 1
NA_BIAS_PAIRS = NA_BIAS_ROWS - 1


def _na_bias_kernel(rpb_ref, o_ref):
    h = pl.program_id(0)
    row = lax.broadcasted_iota(jnp.int32, (GRID_W, LANES), 0)
    lane = lax.broadcasted_iota(jnp.int32, (GRID_W, LANES), 1)
    kc = lane & (GRID_W - 1)
    col_idx = kc - row + (NA_KW - 1)
    left = lane < GRID_W
    cstart = jnp.clip(row - NA_KW // 2, 0, GRID_W - NA_KW)
    in_window = (kc >= cstart) & (kc < cstart + NA_KW)

    def body(d, carry):
        base = (h * NA_BIAS_ROWS + d) * NA_BIAS_COLS
        acc = jnp.zeros((GRID_W, LANES), F32)
        for j in range(NA_BIAS_COLS):
            val = jnp.where(left, rpb_ref[base + j], rpb_ref[base + NA_BIAS_COLS + j])
            acc = acc + jnp.where(col_idx == j, val, 0.0)
        o_ref[0, d] = jnp.where(in_window, acc * LOG2_E, NEG_INF)
        return carry

    lax.fori_loop(0, NA_BIAS_PAIRS, body, 0)


def na_bias_table(rpb):
    return pl.pallas_call(
        _na_bias_kernel,
        grid=(NA_HEADS,),
        in_specs=[pl.BlockSpec(memory_space=pltpu.SMEM)],
        out_specs=pl.BlockSpec((1, NA_BIAS_PAIRS, GRID_W, LANES), lambda h: (h, 0, 0, 0)),
        out_shape=jax.ShapeDtypeStruct((NA_HEADS, NA_BIAS_PAIRS, GRID_W, LANES), F32),
        compiler_params=_cparams(1, 16),
        name="na_bias_table",
    )(rpb.reshape(-1).astype(F32))


def _na_kernel(q_ref, k_ref, v_ref, bias_ref, o_ref, s_scr, *, rows, heads, rows_per_step):
    nk = NA_KH * GRID_W
    n_steps = rows // rows_per_step

    def row_geometry(step, rr):
        r = step * rows_per_step + rr
        rs = jnp.clip(r - NA_KH // 2, 0, rows - NA_KH)
        d0 = rs - r + (NA_KH - 1)
        return pl.multiple_of(r * GRID_W, GRID_W), pl.multiple_of(rs * GRID_W, GRID_W), d0

    def score_stage(step, slot):
        for rr in range(rows_per_step):
            q0, k0, d0 = row_geometry(step, rr)
            for h in range(heads):
                cols = slice(h * NA_HEAD_DIM, (h + 1) * NA_HEAD_DIM)
                q = q_ref[pl.ds(q0, GRID_W), cols]
                k = k_ref[pl.ds(k0, nk), cols]
                s = lax.dot_general(q, k, (((1,), (1,)), ((), ())), preferred_element_type=F32)
                bias = jnp.concatenate([bias_ref[h, d0 + 2 * p] for p in range(NA_KH // 2)], axis=1)
                s_scr[slot, rr * heads + h] = s + bias

    def value_stage(step, slot):
        for rr in range(rows_per_step):
            q0, k0, _ = row_geometry(step, rr)
            for h in range(heads):
                cols = slice(h * NA_HEAD_DIM, (h + 1) * NA_HEAD_DIM)
                s = s_scr[slot, rr * heads + h]
                p = jnp.exp2(s - jnp.max(s, axis=-1, keepdims=True))
                l = jnp.sum(p, axis=-1, keepdims=True)
                o = jnp.dot(p.astype(BF16), v_ref[pl.ds(k0, nk), cols], preferred_element_type=F32)
                o_ref[pl.ds(q0, GRID_W), cols] = (o / l).astype(o_ref.dtype)

    score_stage(0, 0)

    def body(u, carry):
        score_stage(2 * u + 1, 1)
        value_stage(2 * u, 0)
        score_stage(2 * u + 2, 0)
        value_stage(2 * u + 1, 1)
        return carry

    lax.fori_loop(0, n_steps // 2 - 1, body, 0)
    score_stage(n_steps - 1, 1)
    value_stage(n_steps - 2, 0)
    value_stage(n_steps - 1, 1)


def na_attention(qkv, bias, batch, seq, heads_per_step=4, rows_per_step=2):
    g = heads_per_step
    w = g * NA_HEAD_DIM
    groups = NA_HEADS // g
    rows = seq // GRID_W
    assert rows % rows_per_step == 0 and rows >= NA_KH
    return pl.pallas_call(
        functools.partial(_na_kernel, rows=rows, heads=g, rows_per_step=rows_per_step),
        grid=(batch, groups),
        in_specs=[pl.BlockSpec((seq, w), lambda b, j: (b, j)),
                  pl.BlockSpec((seq, w), lambda b, j: (b, groups + j)),
                  pl.BlockSpec((seq, w), lambda b, j: (b, 2 * groups + j)),
                  pl.BlockSpec((g, NA_BIAS_PAIRS, GRID_W, LANES), lambda b, j: (j, 0, 0, 0))],
        out_specs=pl.BlockSpec((seq, w), lambda b, j: (b, j)),
        out_shape=jax.ShapeDtypeStruct((batch * seq, NA_WIDTH), BF16),
        scratch_shapes=[pltpu.VMEM((2, rows_per_step * g, GRID_W, NA_KH * GRID_W), F32)],
        compiler_params=_cparams(2, 32),
        name="na_attention",
    )(qkv, qkv, qkv, bias)


def _mem_attn_kernel(q_ref, kv_ref, wo_ref, h_ref, gp_ref, gn_ref, ho_ref, uo_ref, *, scale):
    outs = []
    for h in range(MEM_HEADS):
        q = q_ref[:, h * MEM_HEAD_DIM:(h + 1) * MEM_HEAD_DIM]
        k = kv_ref[:, 2 * h * MEM_HEAD_DIM:(2 * h + 1) * MEM_HEAD_DIM]
        v = kv_ref[:, (2 * h + 1) * MEM_HEAD_DIM:(2 * h + 2) * MEM_HEAD_DIM]
        s = lax.dot_general(q, k, (((1,), (1,)), ((), ())), preferred_element_type=F32) * scale
        outs.append(_softmax_pv(s, v).astype(BF16))
    o = jnp.concatenate(outs, axis=1)
    a = jnp.dot(o, wo_ref[...].astype(BF16), preferred_element_type=F32)
    h = h_ref[...] + _rms(a, gp_ref[...])
    ho_ref[...] = h
    uo_ref[...] = _rms(h, gn_ref[...]).astype(uo_ref.dtype)


def mem_attention(q, kv, w_o, layer, h, g_post, g_next, batch, seq, bq=256):
    mem_len = kv.shape[0] // batch
    d = w_o.shape[2]
    bq = min(bq, seq)
    nq = seq // bq
    width = MEM_HEADS * MEM_HEAD_DIM
    row = pl.BlockSpec((bq, d), lambda b, i: (b * nq + i, 0))
    vec = pl.BlockSpec((1, d), lambda b, i: (0, 0))
    return pl.pallas_call(
        functools.partial(_mem_attn_kernel, scale=float(MEM_HEAD_DIM) ** -0.5),
        grid=(batch, nq),
        in_specs=[pl.BlockSpec((bq, width), lambda b, i: (b * nq + i, 0)),
                  pl.BlockSpec((mem_len, 2 * width), lambda b, i: (b, 0)),
                  pl.BlockSpec((None, width, d), lambda b, i: (layer, 0, 0)),
                  row, vec, vec],
        out_specs=[row, row],
        out_shape=[jax.ShapeDtypeStruct((batch * seq, d), F32), jax.ShapeDtypeStruct((batch * seq, d), BF16)],
        compiler_params=_cparams(2, 56),
        name="mem_attention",
    )(q, kv, w_o, h, _row(g_post), _row(g_next))


def kernel(x, mem, positions, ffn1_w_gate, ffn1_w_up, ffn1_w_down, g_ffn1, w_in, g_q_a, w_q_b, g_kv_a, w_kv_b, na_rpb, w_out, g_mix, g_mem_in, w_mem_q, w_mem_kv, w_mem_o, g_mem_attn, ffn2_w_gate, ffn2_w_up, ffn2_w_down, g_ffn2, g_final):
    batch, seq, d = x.shape
    m = batch * seq
    depth = ffn1_w_gate.shape[0]
    pos = positions.reshape(m, 1).astype(jnp.int32)
    invf = _inv_freq_lanes()
    mem2 = mem.reshape(-1, d)

    h = x.reshape(m, d)
    u = norm_cast(h, g_ffn1[0, 0])
    out = None
    for l in range(depth):
        f = ffn(u, ffn1_w_gate, ffn1_w_up, ffn1_w_down, l)
        h, u = resid_norm(h, f, g_ffn1[l, 1], FFN_RES_WEIGHT, g_mix[l, 0])

        qkv = mm_nt(u, jnp.swapaxes(w_in, 1, 2), l, BF16, bm=2048, bn=512, n_cols=IN_PROJ_MAIN,
                    scaled_cols=NA_WIDTH, scale=LOG2_E * float(NA_HEAD_DIM) ** -0.5,
                    single_buffer_x=True, name="in_proj")
        q_cat = q_b_proj(qkv, g_q_a[l], w_q_b[l], pos, invf)
        k_cat, v_mla = kv_b_proj(qkv, g_kv_a[l], w_kv_b[l], u, w_in[l, :, IN_PROJ_MAIN:], pos, invf)
        o_mla = mla_attention(q_cat, k_cat, v_mla, batch, seq)
        o_na = na_attention(qkv, na_bias_table(na_rpb[l]), batch, seq)
        o = mm2(o_na, o_mla, w_out, l, BF16, bm=2048, bn=512, single_buffer_x=True, name="out_proj")
        h, u = resid_norm(h, o, g_mix[l, 1], 1.0, g_mem_attn[l, 0])

        mem_n = norm_cast(mem2, g_mem_in[l])
        kv_mem = mm(mem_n, w_mem_kv, l, BF16, bm=1024, bn=512, name="mem_kv_proj")
        q_mem = mm(u, w_mem_q, l, BF16, bm=1024, bn=512, name="mem_q_proj")
        h, u = mem_attention(q_mem, kv_mem, w_mem_o, l, h, g_mem_attn[l, 1], g_ffn2[l, 0], batch, seq)

        f = ffn(u, ffn2_w_gate, ffn2_w_up, ffn2_w_down, l)
        if l + 1 < depth:
            h = resid_final(h, f, g_ffn2[l, 1], FFN_RES_WEIGHT, g_final[l])
            u = norm_cast(h, g_ffn1[l + 1, 0])
        else:
            out = resid_final(h, f, g_ffn2[l, 1], FFN_RES_WEIGHT, g_final[l])
    return out.reshape(batch, seq, d)
```

```python
import functools

import jax
import jax.numpy as jnp
from jax import lax
from jax.experimental import pallas as pl
from jax.experimental.pallas import tpu as pltpu

F32 = jnp.float32
BF16 = jnp.bfloat16

GRID_W = 64
NA_HEADS = 16
NA_HEAD_DIM = 128
NA_KH = 8
NA_KW = 16
MLA_HEADS = 16
MLA_Q_RANK = 1024
MLA_KV_RANK = 512
MLA_NOPE_DIM = 128
MLA_ROPE_DIM = 64
MLA_V_DIM = 128
MLA_QK_PAD = 256
MLA_VT_ROWS = MLA_V_DIM + 16
LOG2_E = 1.4426950408889634
ROPE_THETA = 10000.0
MEM_HEADS = 4
MEM_HEAD_DIM = 128
FFN_RES_WEIGHT = 0.5
NORM_EPS = 1e-6
NEG_INF = -1e30
NA_WIDTH = NA_HEADS * NA_HEAD_DIM
MLA_WIDTH = MLA_HEADS * MLA_V_DIM
IN_PROJ_MAIN = 3 * NA_WIDTH + MLA_Q_RANK + MLA_KV_RANK
LANES = 128
MIB = 1024 * 1024


def _cparams(n_axes, vmem_mib):
    return pltpu.CompilerParams(dimension_semantics=("parallel",) * n_axes,
                                vmem_limit_bytes=vmem_mib * MIB)


def _rms(x, g):
    return x * lax.rsqrt(jnp.mean(x * x, axis=-1, keepdims=True) + NORM_EPS) * g


def _row(v):
    return v.reshape(1, -1).astype(F32)


def _norm_cast_kernel(x_ref, g_ref, o_ref):
    o_ref[...] = _rms(x_ref[...], g_ref[...]).astype(o_ref.dtype)


def norm_cast(x, g, bm=256):
    m, d = x.shape
    bm = min(bm, m)
    return pl.pallas_call(
        _norm_cast_kernel,
        grid=(m // bm,),
        in_specs=[pl.BlockSpec((bm, d), lambda i: (i, 0)), pl.BlockSpec((1, d), lambda i: (0, 0))],
        out_specs=pl.BlockSpec((bm, d), lambda i: (i, 0)),
        out_shape=jax.ShapeDtypeStruct((m, d), BF16),
        compiler_params=_cparams(1, 32),
        name="norm_cast",
    )(x, _row(g))


def _resid_norm_kernel(h_ref, f_ref, gp_ref, gn_ref, ho_ref, uo_ref, *, weight):
    h = h_ref[...] + weight * _rms(f_ref[...].astype(F32), gp_ref[...])
    ho_ref[...] = h
    uo_ref[...] = _rms(h, gn_ref[...]).astype(uo_ref.dtype)


def resid_norm(h, f, g_post, weight, g_next, bm=256):
    m, d = h.shape
    bm = min(bm, m)
    row = pl.BlockSpec((bm, d), lambda i: (i, 0))
    vec = pl.BlockSpec((1, d), lambda i: (0, 0))
    return pl.pallas_call(
        functools.partial(_resid_norm_kernel, weight=weight),
        grid=(m // bm,),
        in_specs=[row, row, vec, vec],
        out_specs=[row, row],
        out_shape=[jax.ShapeDtypeStruct((m, d), F32), jax.ShapeDtypeStruct((m, d), BF16)],
        compiler_params=_cparams(1, 48),
        name="resid_norm",
    )(h, f, _row(g_post), _row(g_next))


def _resid_final_kernel(h_ref, f_ref, gp_ref, gn_ref, o_ref, *, weight):
    h = h_ref[...] + weight * _rms(f_ref[...].astype(F32), gp_ref[...])
    o_ref[...] = _rms(h, gn_ref[...])


def resid_final(h, f, g_post, weight, g_final, bm=256):
    m, d = h.shape
    bm = min(bm, m)
    row = pl.BlockSpec((bm, d), lambda i: (i, 0))
    vec = pl.BlockSpec((1, d), lambda i: (0, 0))
    return pl.pallas_call(
        functools.partial(_resid_final_kernel, weight=weight),
        grid=(m // bm,),
        in_specs=[row, row, vec, vec],
        out_specs=row,
        out_shape=jax.ShapeDtypeStruct((m, d), F32),
        compiler_params=_cparams(1, 48),
        name="resid_final",
    )(h, f, _row(g_post), _row(g_final))


def _x_spec(bm, k, single_buffer):
    if single_buffer:
        return pl.BlockSpec((bm, k), lambda i, j: (i, 0), pipeline_mode=pl.Buffered(1))
    return pl.BlockSpec((bm, k), lambda i, j: (i, 0))


def _w_spec(k, bn, layer):
    return pl.BlockSpec((None, k, bn), lambda i, j: (layer, 0, j))


def _mm_kernel(x_ref, w_ref, o_ref):
    o_ref[...] = jnp.dot(x_ref[...], w_ref[...].astype(BF16),
                         preferred_element_type=F32).astype(o_ref.dtype)


def mm(x, w, layer, out_dtype, bm, bn, n_cols=None, single_buffer_x=False, vmem_mib=56, name="mm"):
    m, k = x.shape
    n = w.shape[2] if n_cols is None else n_cols
    bm, bn = min(bm, m), min(bn, n)
    assert m % bm == 0 and n % bn == 0 and w.shape[1] == k
    return pl.pallas_call(
        _mm_kernel,
        grid=(m // bm, n // bn),
        in_specs=[_x_spec(bm, k, single_buffer_x), _w_spec(k, bn, layer)],
        out_specs=pl.BlockSpec((bm, bn), lambda i, j: (i, j)),
        out_shape=jax.ShapeDtypeStruct((m, n), out_dtype),
        compiler_params=_cparams(2, vmem_mib),
        name=name,
    )(x, w)


def _mm_nt_kernel(x_ref, wt_ref, o_ref, *, scaled_blocks, scale):
    acc = lax.dot_general(x_ref[...], wt_ref[...].astype(BF16), (((1,), (1,)), ((), ())),
                          preferred_element_type=F32)
    if scaled_blocks:
        acc = acc * jnp.where(pl.program_id(1) < scaled_blocks, scale, 1.0)
    o_ref[...] = acc.astype(o_ref.dtype)


def mm_nt(x, wt, layer, out_dtype, bm, bn, n_cols=None, scaled_cols=0, scale=1.0, single_buffer_x=False,
          vmem_mib=56, name="mm_nt"):
    m, k = x.shape
    n = wt.shape[1] if n_cols is None else n_cols
    bm, bn = min(bm, m), min(bn, n)
    assert m % bm == 0 and n % bn == 0 and wt.shape[2] == k and scaled_cols % bn == 0
    return pl.pallas_call(
        functools.partial(_mm_nt_kernel, scaled_blocks=scaled_cols // bn, scale=scale),
        grid=(m // bm, n // bn),
        in_specs=[_x_spec(bm, k, single_buffer_x), pl.BlockSpec((None, bn, k), lambda i, j: (layer, j, 0))],
        out_specs=pl.BlockSpec((bm, bn), lambda i, j: (i, j)),
        out_shape=jax.ShapeDtypeStruct((m, n), out_dtype),
        compiler_params=_cparams(2, vmem_mib),
        name=name,
    )(x, wt)


def _mm2_kernel(xa_ref, xb_ref, w_ref, o_ref):
    ka = xa_ref.shape[1]
    w = w_ref[...].astype(BF16)
    acc = jnp.dot(xa_ref[...], w[:ka], preferred_element_type=F32)
    acc = acc + jnp.dot(xb_ref[...], w[ka:], preferred_element_type=F32)
    o_ref[...] = acc.astype(o_ref.dtype)


def mm2(xa, xb, w, layer, out_dtype, bm, bn, single_buffer_x=False, vmem_mib=56, name="mm2"):
    m, ka = xa.shape
    kb = xb.shape[1]
    n = w.shape[2]
    bm, bn = min(bm, m), min(bn, n)
    assert m % bm == 0 and n % bn == 0 and w.shape[1] == ka + kb
    return pl.pallas_call(
        _mm2_kernel,
        grid=(m // bm, n // bn),
        in_specs=[_x_spec(bm, ka, single_buffer_x), _x_spec(bm, kb, single_buffer_x),
                  _w_spec(ka + kb, bn, layer)],
        out_specs=pl.BlockSpec((bm, bn), lambda i, j: (i, j)),
        out_shape=jax.ShapeDtypeStruct((m, n), out_dtype),
        compiler_params=_cparams(2, vmem_mib),
        name=name,
    )(xa, xb, w)


def _gateup_kernel(x_ref, wg_ref, wu_ref, o_ref):
    x = x_ref[...]
    g = jnp.dot(x, wg_ref[...].astype(BF16), preferred_element_type=F32)
    u = jnp.dot(x, wu_ref[...].astype(BF16), preferred_element_type=F32)
    o_ref[...] = (g * jax.nn.sigmoid(g) * u).astype(o_ref.dtype)


def gateup(x, w_gate, w_up, layer, bm=2048, bf=256):
    m, k = x.shape
    f = w_gate.shape[2]
    bm, bf = min(bm, m), min(bf, f)
    assert m % bm == 0 and f % bf == 0
    return pl.pallas_call(
        _gateup_kernel,
        grid=(m // bm, f // bf),
        in_specs=[_x_spec(bm, k, True), _w_spec(k, bf, layer), _w_spec(k, bf, layer)],
        out_specs=pl.BlockSpec((bm, bf), lambda i, j: (i, j)),
        out_shape=jax.ShapeDtypeStruct((m, f), BF16),
        compiler_params=_cparams(2, 56),
        name="ffn_gateup",
    )(x, w_gate, w_up)


def ffn(u, w_gate, w_up, w_down, layer):
    hidden = gateup(u, w_gate, w_up, layer)
    return mm(hidden, w_down, layer, BF16, bm=1024, bn=256, single_buffer_x=True, vmem_mib=60, name="ffn_down")


def _rope_tables(pos_ref, invf_ref, width):
    ang = pos_ref[...].astype(F32) * invf_ref[...]
    lane = lax.broadcasted_iota(jnp.int32, ang.shape, 1)
    first_half = (lane & (MLA_ROPE_DIM - 1)) < MLA_ROPE_DIM // 2
    valid = lane < width
    sin = jnp.sin(ang)
    cos_t = jnp.where(valid, jnp.cos(ang), 0.0)
    sin_t = jnp.where(valid, jnp.where(first_half, -sin, sin), 0.0)
    return cos_t, sin_t, first_half


def _rope_apply(t, cos_t, sin_t, first_half):
    half = MLA_ROPE_DIM // 2
    partner = jnp.where(first_half, pltpu.roll(t, LANES - half, 1), pltpu.roll(t, half, 1))
    return t * cos_t + partner * sin_t


def _inv_freq_lanes(width):
    half = MLA_ROPE_DIM // 2
    inv_freq = 1.0 / (ROPE_THETA ** (jnp.arange(half, dtype=F32) / half))
    return jnp.concatenate([jnp.tile(inv_freq, width // half), jnp.zeros((LANES - width,), F32)]).reshape(1, LANES)


def _qb_kernel(cq_ref, g_ref, wn_ref, wp_ref, pos_ref, invf_ref, o_ref):
    cn = _rms(cq_ref[...].astype(F32), g_ref[...]).astype(BF16)
    qn = jnp.dot(cn, wn_ref[...], preferred_element_type=F32)
    qp = jnp.dot(cn, wp_ref[...], preferred_element_type=F32)
    cos_t, sin_t, first_half = _rope_tables(pos_ref, invf_ref, LANES)
    low = lax.broadcasted_iota(jnp.int32, cos_t.shape, 1) < MLA_ROPE_DIM
    for pair in range(MLA_HEADS // 2):
        r = _rope_apply(qp[:, pair * LANES:(pair + 1) * LANES], cos_t, sin_t, first_half)
        pe = (jnp.where(low, r, 0.0), jnp.where(low, pltpu.roll(r, LANES - MLA_ROPE_DIM, 1), 0.0))
        for k in range(2):
            h = 2 * pair + k
            c0 = h * MLA_QK_PAD
            o_ref[:, c0:c0 + MLA_NOPE_DIM] = qn[:, h * MLA_NOPE_DIM:(h + 1) * MLA_NOPE_DIM].astype(o_ref.dtype)
            o_ref[:, c0 + MLA_NOPE_DIM:c0 + MLA_QK_PAD] = pe[k].astype(o_ref.dtype)


def q_b_proj(qkv, g_q_a, w_q_b, pos, invf, bm=512):
    m = qkv.shape[0]
    bm = min(bm, m)
    h, dq = MLA_HEADS, MLA_NOPE_DIM + MLA_ROPE_DIM
    w = w_q_b.reshape(MLA_Q_RANK, h, dq) * (LOG2_E * float(dq) ** -0.5)
    w_nope = w[:, :, :MLA_NOPE_DIM].reshape(MLA_Q_RANK, h * MLA_NOPE_DIM).astype(BF16)
    w_pe = w[:, :, MLA_NOPE_DIM:].reshape(MLA_Q_RANK, h * MLA_ROPE_DIM).astype(BF16)
    cq_block = (3 * NA_WIDTH) // MLA_Q_RANK
    return pl.pallas_call(
        _qb_kernel,
        grid=(m // bm,),
        in_specs=[pl.BlockSpec((bm, MLA_Q_RANK), lambda i: (i, cq_block)),
                  pl.BlockSpec((1, MLA_Q_RANK), lambda i: (0, 0)),
                  pl.BlockSpec((MLA_Q_RANK, h * MLA_NOPE_DIM), lambda i: (0, 0)),
                  pl.BlockSpec((MLA_Q_RANK, h * MLA_ROPE_DIM), lambda i: (0, 0)),
                  pl.BlockSpec((bm, 1), lambda i: (i, 0)),
                  pl.BlockSpec((1, LANES), lambda i: (0, 0))],
        out_specs=pl.BlockSpec((bm, h * MLA_QK_PAD), lambda i: (i, 0)),
        out_shape=jax.ShapeDtypeStruct((m, h * MLA_QK_PAD), BF16),
        compiler_params=_cparams(1, 56),
        name="mla_q_proj",
    )(qkv, _row(g_q_a), w_nope, w_pe, pos, invf)


def _kvb_kernel(ckv_ref, g_ref, wk_ref, wvt_ref, u_ref, wt_ref, pos_ref, invf_ref, k_ref, vt_ref):
    cn = _rms(ckv_ref[...].astype(F32), g_ref[...]).astype(BF16)
    kn = jnp.dot(cn, wk_ref[...], preferred_element_type=F32)
    kr = jnp.dot(u_ref[...], wt_ref[...], preferred_element_type=F32)
    cos_t, sin_t, first_half = _rope_tables(pos_ref, invf_ref, MLA_ROPE_DIM)
    kpe = _rope_apply(kr, cos_t, sin_t, first_half).astype(k_ref.dtype)
    for h in range(MLA_HEADS):
        k_ref[:, h * MLA_QK_PAD:h * MLA_QK_PAD + MLA_NOPE_DIM] = (
            kn[:, h * MLA_NOPE_DIM:(h + 1) * MLA_NOPE_DIM].astype(k_ref.dtype))
        k_ref[:, h * MLA_QK_PAD + MLA_NOPE_DIM:(h + 1) * MLA_QK_PAD] = kpe
    vt = lax.dot_general(wvt_ref[...], cn, (((1,), (1,)), ((), ())), preferred_element_type=F32)
    ones = jnp.ones((MLA_VT_ROWS - MLA_V_DIM, vt.shape[1]), vt_ref.dtype)
    for h in range(MLA_HEADS):
        r0 = h * MLA_VT_ROWS
        vt_ref[r0:r0 + MLA_V_DIM, :] = vt[h * MLA_V_DIM:(h + 1) * MLA_V_DIM].astype(vt_ref.dtype)
        vt_ref[r0 + MLA_V_DIM:r0 + MLA_VT_ROWS, :] = ones


def kv_b_proj(qkv, g_kv_a, w_kv_b, u, w_rope_in, pos, invf, bm=512):
    m, d = u.shape
    bm = min(bm, m)
    h = MLA_HEADS
    ckv_block = (3 * NA_WIDTH + MLA_Q_RANK) // MLA_KV_RANK
    wt = jnp.pad(w_rope_in, ((0, 0), (0, LANES - MLA_ROPE_DIM))).astype(BF16)
    w3 = w_kv_b.reshape(MLA_KV_RANK, h, MLA_NOPE_DIM + MLA_V_DIM)
    w_k = w3[:, :, :MLA_NOPE_DIM].reshape(MLA_KV_RANK, h * MLA_NOPE_DIM).astype(BF16)
    w_vt = w3[:, :, MLA_NOPE_DIM:].reshape(MLA_KV_RANK, h * MLA_V_DIM).T.astype(BF16)
    return pl.pallas_call(
        _kvb_kernel,
        grid=(m // bm,),
        in_specs=[pl.BlockSpec((bm, MLA_KV_RANK), lambda i: (i, ckv_block)),
                  pl.BlockSpec((1, MLA_KV_RANK), lambda i: (0, 0)),
                  pl.BlockSpec((MLA_KV_RANK, h * MLA_NOPE_DIM), lambda i: (0, 0)),
                  pl.BlockSpec((h * MLA_V_DIM, MLA_KV_RANK), lambda i: (0, 0)),
                  pl.BlockSpec((bm, d), lambda i: (i, 0)),
                  pl.BlockSpec((d, LANES), lambda i: (0, 0)),
                  pl.BlockSpec((bm, 1), lambda i: (i, 0)),
                  pl.BlockSpec((1, LANES), lambda i: (0, 0))],
        out_specs=[pl.BlockSpec((bm, h * MLA_QK_PAD), lambda i: (i, 0)),
                   pl.BlockSpec((h * MLA_VT_ROWS, bm), lambda i: (0, i))],
        out_shape=[jax.ShapeDtypeStruct((m, h * MLA_QK_PAD), BF16),
                   jax.ShapeDtypeStruct((h * MLA_VT_ROWS, m), BF16)],
        compiler_params=_cparams(1, 56),
        name="mla_kv_proj",
    )(qkv, _row(g_kv_a), w_k, w_vt, u, wt, pos, invf)


def _softmax_pv(s, v):
    m = jnp.max(s, axis=-1, keepdims=True)
    p = jnp.exp(s - m)
    l = jnp.sum(p, axis=-1, keepdims=True)
    return jnp.dot(p.astype(BF16), v, preferred_element_type=F32) / l


def _mla_kernel(q_ref, qn_ref, k_ref, kn_ref, vt_ref, o_ref, s_scr, *, heads, bq):
    def score_stage(q_tile_ref, row0, keys_ref, slot):
        for h in range(heads):
            qk = slice(h * MLA_QK_PAD, (h + 1) * MLA_QK_PAD)
            s_scr[slot, h] = lax.dot_general(keys_ref[:, qk], q_tile_ref[row0:row0 + bq, qk],
                                             (((1,), (1,)), ((), ())), preferred_element_type=F32)

    def value_stage(slot, row0):
        for h in range(heads):
            s = s_scr[slot, h]
            p = jnp.exp2(s - jnp.max(s, axis=0, keepdims=True)).astype(BF16)
            ot = jnp.dot(vt_ref[h * MLA_VT_ROWS:(h + 1) * MLA_VT_ROWS, :], p,
                         preferred_element_type=F32)
            o = ot[:MLA_V_DIM] / ot[MLA_V_DIM:MLA_V_DIM + 1]
            o_ref[row0:row0 + bq, h * MLA_V_DIM:(h + 1) * MLA_V_DIM] = o.T.astype(o_ref.dtype)

    first_step = (pl.program_id(0) == 0) & (pl.program_id(1) == 0) & (pl.program_id(2) == 0)

    @pl.when(first_step)
    def _():
        score_stage(q_ref, 0, k_ref, 0)

    score_stage(q_ref, bq, k_ref, 1)
    value_stage(0, 0)
    score_stage(qn_ref, 0, kn_ref, 0)
    value_stage(1, bq)


def mla_attention(q, k, vt, batch, seq, bq=512, heads_per_step=2):
    bq = min(bq, seq // 2)
    pairs = seq // (2 * bq)
    g = heads_per_step
    groups = MLA_HEADS // g
    n_steps = batch * groups * pairs

    def next_step(b, h, i):
        flat = jnp.minimum((b * groups + h) * pairs + i + 1, n_steps - 1)
        return flat // (groups * pairs), (flat // pairs) % groups, flat % pairs

    def q_next_map(b, h, i):
        nb, nh, ni = next_step(b, h, i)
        return (nb * pairs + ni) * 2, nh

    def k_next_map(b, h, i):
        nb, nh, _ = next_step(b, h, i)
        return nb, nh

    return pl.pallas_call(
        functools.partial(_mla_kernel, heads=g, bq=bq),
        grid=(batch, groups, pairs),
        in_specs=[pl.BlockSpec((2 * bq, g * MLA_QK_PAD), lambda b, h, i: (b * pairs + i, h)),
                  pl.BlockSpec((bq, g * MLA_QK_PAD), q_next_map),
                  pl.BlockSpec((seq, g * MLA_QK_PAD), lambda b, h, i: (b, h)),
                  pl.BlockSpec((seq, g * MLA_QK_PAD), k_next_map),
                  pl.BlockSpec((g * MLA_VT_ROWS, seq), lambda b, h, i: (h, b))],
        out_specs=pl.BlockSpec((2 * bq, g * MLA_V_DIM), lambda b, h, i: (b * pairs + i, h)),
        out_shape=jax.ShapeDtypeStruct((batch * seq, MLA_WIDTH), BF16),
        scratch_shapes=[pltpu.VMEM((2, g, seq, bq), F32)],
        compiler_params=pltpu.CompilerParams(dimension_semantics=("arbitrary",) * 3,
                                             vmem_limit_bytes=48 * MIB),
        name="mla_attention",
    )(q, q, k, k, vt)


NA_BIAS_ROWS = 2 * NA_KH - 1
NA_BIAS_COLS = 2 * NA_KW - 1
NA_BIAS_PAIRS = NA_BIAS_ROWS - 1


def _na_bias_kernel(rpb_ref, o_ref):
    h = pl.program_id(0)
    row = lax.broadcasted_iota(jnp.int32, (GRID_W, LANES), 0)
    lane = lax.broadcasted_iota(jnp.int32, (GRID_W, LANES), 1)
    kc = lane & (GRID_W - 1)
    col_idx = kc - row + (NA_KW - 1)
    left = lane < GRID_W
    cstart = jnp.clip(row - NA_KW // 2, 0, GRID_W - NA_KW)
    in_window = (kc >= cstart) & (kc < cstart + NA_KW)

    def body(d, carry):
        base = (h * NA_BIAS_ROWS + d) * NA_BIAS_COLS
        acc = jnp.zeros((GRID_W, LANES), F32)
        for j in range(NA_BIAS_COLS):
            val = jnp.where(left, rpb_ref[base + j], rpb_ref[base + NA_BIAS_COLS + j])
            acc = acc + jnp.where(col_idx == j, val, 0.0)
        o_ref[0, d] = jnp.where(in_window, acc * LOG2_E, NEG_INF)
        return carry

    lax.fori_loop(0, NA_BIAS_PAIRS, body, 0)


def na_bias_table(rpb):
    return pl.pallas_call(
        _na_bias_kernel,
        grid=(NA_HEADS,),
        in_specs=[pl.BlockSpec(memory_space=pltpu.SMEM)],
        out_specs=pl.BlockSpec((1, NA_BIAS_PAIRS, GRID_W, LANES), lambda h: (h, 0, 0, 0)),
        out_shape=jax.ShapeDtypeStruct((NA_HEADS, NA_BIAS_PAIRS, GRID_W, LANES), F32),
        compiler_params=_cparams(1, 16),
        name="na_bias_table",
    )(rpb.reshape(-1).astype(F32))


def _na_kernel(q_ref, k_ref, v_ref, bias_ref, o_ref, s_scr, *, rows, heads, rows_per_step):
    nk = NA_KH * GRID_W
    n_steps = rows // rows_per_step

    def row_geometry(step, rr):
        r = step * rows_per_step + rr
        rs = jnp.clip(r - NA_KH // 2, 0, rows - NA_KH)
        d0 = rs - r + (NA_KH - 1)
        return pl.multiple_of(r * GRID_W, GRID_W), pl.multiple_of(rs * GRID_W, GRID_W), d0

    def score_stage(step, slot):
        for rr in range(rows_per_step):
            q0, k0, d0 = row_geometry(step, rr)
            for h in range(heads):
                cols = slice(h * NA_HEAD_DIM, (h + 1) * NA_HEAD_DIM)
                q = q_ref[pl.ds(q0, GRID_W), cols]
                k = k_ref[pl.ds(k0, nk), cols]
                s = lax.dot_general(q, k, (((1,), (1,)), ((), ())), preferred_element_type=F32)
                bias = jnp.concatenate([bias_ref[h, d0 + 2 * p] for p in range(NA_KH // 2)], axis=1)
                s_scr[slot, rr * heads + h] = s + bias

    def value_stage(step, slot):
        for rr in range(rows_per_step):
            q0, k0, _ = row_geometry(step, rr)
            for h in range(heads):
                cols = slice(h * NA_HEAD_DIM, (h + 1) * NA_HEAD_DIM)
                s = s_scr[slot, rr * heads + h]
                p = jnp.exp2(s - jnp.max(s, axis=-1, keepdims=True))
                l = jnp.sum(p, axis=-1, keepdims=True)
                o = jnp.dot(p.astype(BF16), v_ref[pl.ds(k0, nk), cols], preferred_element_type=F32)
                o_ref[pl.ds(q0, GRID_W), cols] = (o / l).astype(o_ref.dtype)

    score_stage(0, 0)

    def body(u, carry):
        score_stage(2 * u + 1, 1)
        value_stage(2 * u, 0)
        score_stage(2 * u + 2, 0)
        value_stage(2 * u + 1, 1)
        return carry

    lax.fori_loop(0, n_steps // 2 - 1, body, 0)
    score_stage(n_steps - 1, 1)
    value_stage(n_steps - 2, 0)
    value_stage(n_steps - 1, 1)


def na_attention(qkv, bias, batch, seq, heads_per_step=4, rows_per_step=2):
    g = heads_per_step
    w = g * NA_HEAD_DIM
    groups = NA_HEADS // g
    rows = seq // GRID_W
    assert rows % rows_per_step == 0 and rows >= NA_KH
    return pl.pallas_call(
        functools.partial(_na_kernel, rows=rows, heads=g, rows_per_step=rows_per_step),
        grid=(batch, groups),
        in_specs=[pl.BlockSpec((seq, w), lambda b, j: (b, j)),
                  pl.BlockSpec((seq, w), lambda b, j: (b, groups + j)),
                  pl.BlockSpec((seq, w), lambda b, j: (b, 2 * groups + j)),
                  pl.BlockSpec((g, NA_BIAS_PAIRS, GRID_W, LANES), lambda b, j: (j, 0, 0, 0))],
        out_specs=pl.BlockSpec((seq, w), lambda b, j: (b, j)),
        out_shape=jax.ShapeDtypeStruct((batch * seq, NA_WIDTH), BF16),
        scratch_shapes=[pltpu.VMEM((2, rows_per_step * g, GRID_W, NA_KH * GRID_W), F32)],
        compiler_params=_cparams(2, 32),
        name="na_attention",
    )(qkv, qkv, qkv, bias)


def _mem_attn_kernel(q_ref, kv_ref, wo_ref, h_ref, gp_ref, gn_ref, ho_ref, uo_ref, *, scale):
    outs = []
    for h in range(MEM_HEADS):
        q = q_ref[:, h * MEM_HEAD_DIM:(h + 1) * MEM_HEAD_DIM]
        k = kv_ref[:, 2 * h * MEM_HEAD_DIM:(2 * h + 1) * MEM_HEAD_DIM]
        v = kv_ref[:, (2 * h + 1) * MEM_HEAD_DIM:(2 * h + 2) * MEM_HEAD_DIM]
        s = lax.dot_general(q, k, (((1,), (1,)), ((), ())), preferred_element_type=F32) * scale
        outs.append(_softmax_pv(s, v).astype(BF16))
    o = jnp.concatenate(outs, axis=1)
    a = jnp.dot(o, wo_ref[...].astype(BF16), preferred_element_type=F32)
    h = h_ref[...] + _rms(a, gp_ref[...])
    ho_ref[...] = h
    uo_ref[...] = _rms(h, gn_ref[...]).astype(uo_ref.dtype)


def mem_attention(q, kv, w_o, layer, h, g_post, g_next, batch, seq, bq=256):
    mem_len = kv.shape[0] // batch
    d = w_o.shape[2]
    bq = min(bq, seq)
    nq = seq // bq
    width = MEM_HEADS * MEM_HEAD_DIM
    row = pl.BlockSpec((bq, d), lambda b, i: (b * nq + i, 0))
    vec = pl.BlockSpec((1, d), lambda b, i: (0, 0))
    return pl.pallas_call(
        functools.partial(_mem_attn_kernel, scale=float(MEM_HEAD_DIM) ** -0.5),
        grid=(batch, nq),
        in_specs=[pl.BlockSpec((bq, width), lambda b, i: (b * nq + i, 0)),
                  pl.BlockSpec((mem_len, 2 * width), lambda b, i: (b, 0)),
                  pl.BlockSpec((None, width, d), lambda b, i: (layer, 0, 0)),
                  row, vec, vec],
        out_specs=[row, row],
        out_shape=[jax.ShapeDtypeStruct((batch * seq, d), F32), jax.ShapeDtypeStruct((batch * seq, d), BF16)],
        compiler_params=_cparams(2, 56),
        name="mem_attention",
    )(q, kv, w_o, h, _row(g_post), _row(g_next))


def kernel(x, mem, positions, ffn1_w_gate, ffn1_w_up, ffn1_w_down, g_ffn1, w_in, g_q_a, w_q_b, g_kv_a, w_kv_b, na_rpb, w_out, g_mix, g_mem_in, w_mem_q, w_mem_kv, w_mem_o, g_mem_attn, ffn2_w_gate, ffn2_w_up, ffn2_w_down, g_ffn2, g_final):
    batch, seq, d = x.shape
    m = batch * seq
    depth = ffn1_w_gate.shape[0]
    pos = positions.reshape(m, 1).astype(jnp.int32)
    invf_q, invf_k = _inv_freq_lanes(LANES), _inv_freq_lanes(MLA_ROPE_DIM)
    mem2 = mem.reshape(-1, d)

    h = x.reshape(m, d)
    u = norm_cast(h, g_ffn1[0, 0])
    out = None
    for l in range(depth):
        f = ffn(u, ffn1_w_gate, ffn1_w_up, ffn1_w_down, l)
        h, u = resid_norm(h, f, g_ffn1[l, 1], FFN_RES_WEIGHT, g_mix[l, 0])

        qkv = mm_nt(u, jnp.swapaxes(w_in, 1, 2), l, BF16, bm=2048, bn=512, n_cols=IN_PROJ_MAIN,
                    scaled_cols=NA_WIDTH, scale=LOG2_E * float(NA_HEAD_DIM) ** -0.5,
                    single_buffer_x=True, name="in_proj")
        q_cat = q_b_proj(qkv, g_q_a[l], w_q_b[l], pos, invf_q)
        k_cat, v_mla = kv_b_proj(qkv, g_kv_a[l], w_kv_b[l], u, w_in[l, :, IN_PROJ_MAIN:], pos, invf_k)
        o_mla = mla_attention(q_cat, k_cat, v_mla, batch, seq)
        o_na = na_attention(qkv, na_bias_table(na_rpb[l]), batch, seq)
        o = mm2(o_na, o_mla, w_out, l, BF16, bm=2048, bn=512, single_buffer_x=True, name="out_proj")
        h, u = resid_norm(h, o, g_mix[l, 1], 1.0, g_mem_attn[l, 0])

        mem_n = norm_cast(mem2, g_mem_in[l])
        kv_mem = mm(mem_n, w_mem_kv, l, BF16, bm=1024, bn=512, name="mem_kv_proj")
        q_mem = mm(u, w_mem_q, l, BF16, bm=1024, bn=512, name="mem_q_proj")
        h, u = mem_attention(q_mem, kv_mem, w_mem_o, l, h, g_mem_attn[l, 1], g_ffn2[l, 0], batch, seq)

        f = ffn(u, ffn2_w_gate, ffn2_w_up, ffn2_w_down, l)
        if l + 1 < depth:
            h = resid_final(h, f, g_ffn2[l, 1], FFN_RES_WEIGHT, g_final[l])
            u = norm_cast(h, g_ffn1[l + 1, 0])
        else:
            out = resid_final(h, f, g_ffn2[l, 1], FFN_RES_WEIGHT, g_final[l])
    return out.reshape(batch, seq, d)
```

```python
import functools

import jax
import jax.numpy as jnp
from jax import lax
from jax.experimental import pallas as pl
from jax.experimental.pallas import tpu as pltpu

F32 = jnp.float32
BF16 = jnp.bfloat16

GRID_W = 64
NA_HEADS = 16
NA_HEAD_DIM = 128
NA_KH = 8
NA_KW = 16
MLA_HEADS = 16
MLA_Q_RANK = 1024
MLA_KV_RANK = 512
MLA_NOPE_DIM = 128
MLA_ROPE_DIM = 64
MLA_V_DIM = 128
MLA_QK_PAD = 256
MLA_VT_ROWS = MLA_V_DIM + 16
LOG2_E = 1.4426950408889634
ROPE_THETA = 10000.0
MEM_HEADS = 4
MEM_HEAD_DIM = 128
FFN_RES_WEIGHT = 0.5
NORM_EPS = 1e-6
NEG_INF = -1e30
NA_WIDTH = NA_HEADS * NA_HEAD_DIM
MLA_WIDTH = MLA_HEADS * MLA_V_DIM
IN_PROJ_MAIN = 3 * NA_WIDTH + MLA_Q_RANK + MLA_KV_RANK
LANES = 128
MIB = 1024 * 1024


def _cparams(n_axes, vmem_mib):
    return pltpu.CompilerParams(dimension_semantics=("parallel",) * n_axes,
                                vmem_limit_bytes=vmem_mib * MIB)


def _rms(x, g):
    return x * lax.rsqrt(jnp.mean(x * x, axis=-1, keepdims=True) + NORM_EPS) * g


def _row(v):
    return v.reshape(1, -1).astype(F32)


def _norm_cast_kernel(x_ref, g_ref, o_ref):
    o_ref[...] = _rms(x_ref[...], g_ref[...]).astype(o_ref.dtype)


def norm_cast(x, g, bm=256):
    m, d = x.shape
    bm = min(bm, m)
    return pl.pallas_call(
        _norm_cast_kernel,
        grid=(m // bm,),
        in_specs=[pl.BlockSpec((bm, d), lambda i: (i, 0)), pl.BlockSpec((1, d), lambda i: (0, 0))],
        out_specs=pl.BlockSpec((bm, d), lambda i: (i, 0)),
        out_shape=jax.ShapeDtypeStruct((m, d), BF16),
        compiler_params=_cparams(1, 32),
        name="norm_cast",
    )(x, _row(g))


def _resid_norm_kernel(h_ref, f_ref, gp_ref, gn_ref, ho_ref, uo_ref, *, weight):
    h = h_ref[...] + weight * _rms(f_ref[...].astype(F32), gp_ref[...])
    ho_ref[...] = h
    uo_ref[...] = _rms(h, gn_ref[...]).astype(uo_ref.dtype)


def resid_norm(h, f, g_post, weight, g_next, bm=256):
    m, d = h.shape
    bm = min(bm, m)
    row = pl.BlockSpec((bm, d), lambda i: (i, 0))
    vec = pl.BlockSpec((1, d), lambda i: (0, 0))
    return pl.pallas_call(
        functools.partial(_resid_norm_kernel, weight=weight),
        grid=(m // bm,),
        in_specs=[row, row, vec, vec],
        out_specs=[row, row],
        out_shape=[jax.ShapeDtypeStruct((m, d), F32), jax.ShapeDtypeStruct((m, d), BF16)],
        compiler_params=_cparams(1, 48),
        name="resid_norm",
    )(h, f, _row(g_post), _row(g_next))


def _resid_final_kernel(h_ref, f_ref, gp_ref, gn_ref, o_ref, *, weight):
    h = h_ref[...] + weight * _rms(f_ref[...].astype(F32), gp_ref[...])
    o_ref[...] = _rms(h, gn_ref[...])


def resid_final(h, f, g_post, weight, g_final, bm=256):
    m, d = h.shape
    bm = min(bm, m)
    row = pl.BlockSpec((bm, d), lambda i: (i, 0))
    vec = pl.BlockSpec((1, d), lambda i: (0, 0))
    return pl.pallas_call(
        functools.partial(_resid_final_kernel, weight=weight),
        grid=(m // bm,),
        in_specs=[row, row, vec, vec],
        out_specs=row,
        out_shape=jax.ShapeDtypeStruct((m, d), F32),
        compiler_params=_cparams(1, 48),
        name="resid_final",
    )(h, f, _row(g_post), _row(g_final))


def _x_spec(bm, k, single_buffer):
    if single_buffer:
        return pl.BlockSpec((bm, k), lambda i, j: (i, 0), pipeline_mode=pl.Buffered(1))
    return pl.BlockSpec((bm, k), lambda i, j: (i, 0))


def _w_spec(k, bn, layer):
    return pl.BlockSpec((None, k, bn), lambda i, j: (layer, 0, j))


def _mm_kernel(x_ref, w_ref, o_ref):
    o_ref[...] = jnp.dot(x_ref[...], w_ref[...].astype(BF16),
                         preferred_element_type=F32).astype(o_ref.dtype)


def mm(x, w, layer, out_dtype, bm, bn, n_cols=None, single_buffer_x=False, vmem_mib=56, name="mm"):
    m, k = x.shape
    n = w.shape[2] if n_cols is None else n_cols
    bm, bn = min(bm, m), min(bn, n)
    assert m % bm == 0 and n % bn == 0 and w.shape[1] == k
    return pl.pallas_call(
        _mm_kernel,
        grid=(m // bm, n // bn),
        in_specs=[_x_spec(bm, k, single_buffer_x), _w_spec(k, bn, layer)],
        out_specs=pl.BlockSpec((bm, bn), lambda i, j: (i, j)),
        out_shape=jax.ShapeDtypeStruct((m, n), out_dtype),
        compiler_params=_cparams(2, vmem_mib),
        name=name,
    )(x, w)


def _mm_nt_kernel(x_ref, wt_ref, o_ref, *, scaled_blocks, scale):
    acc = lax.dot_general(x_ref[...], wt_ref[...].astype(BF16), (((1,), (1,)), ((), ())),
                          preferred_element_type=F32)
    if scaled_blocks:
        acc = acc * jnp.where(pl.program_id(1) < scaled_blocks, scale, 1.0)
    o_ref[...] = acc.astype(o_ref.dtype)


def mm_nt(x, wt, layer, out_dtype, bm, bn, n_cols=None, scaled_cols=0, scale=1.0, single_buffer_x=False,
          vmem_mib=56, name="mm_nt"):
    m, k = x.shape
    n = wt.shape[1] if n_cols is None else n_cols
    bm, bn = min(bm, m), min(bn, n)
    assert m % bm == 0 and n % bn == 0 and wt.shape[2] == k and scaled_cols % bn == 0
    return pl.pallas_call(
        functools.partial(_mm_nt_kernel, scaled_blocks=scaled_cols // bn, scale=scale),
        grid=(m // bm, n // bn),
        in_specs=[_x_spec(bm, k, single_buffer_x), pl.BlockSpec((None, bn, k), lambda i, j: (layer, j, 0))],
        out_specs=pl.BlockSpec((bm, bn), lambda i, j: (i, j)),
        out_shape=jax.ShapeDtypeStruct((m, n), out_dtype),
        compiler_params=_cparams(2, vmem_mib),
        name=name,
    )(x, wt)


def _mm2_kernel(xa_ref, xb_ref, w_ref, o_ref):
    ka = xa_ref.shape[1]
    w = w_ref[...].astype(BF16)
    acc = jnp.dot(xa_ref[...], w[:ka], preferred_element_type=F32)
    acc = acc + jnp.dot(xb_ref[...], w[ka:], preferred_element_type=F32)
    o_ref[...] = acc.astype(o_ref.dtype)


def mm2(xa, xb, w, layer, out_dtype, bm, bn, single_buffer_x=False, vmem_mib=56, name="mm2"):
    m, ka = xa.shape
    kb = xb.shape[1]
    n = w.shape[2]
    bm, bn = min(bm, m), min(bn, n)
    assert m % bm == 0 and n % bn == 0 and w.shape[1] == ka + kb
    return pl.pallas_call(
        _mm2_kernel,
        grid=(m // bm, n // bn),
        in_specs=[_x_spec(bm, ka, single_buffer_x), _x_spec(bm, kb, single_buffer_x),
                  _w_spec(ka + kb, bn, layer)],
        out_specs=pl.BlockSpec((bm, bn), lambda i, j: (i, j)),
        out_shape=jax.ShapeDtypeStruct((m, n), out_dtype),
        compiler_params=_cparams(2, vmem_mib),
        name=name,
    )(xa, xb, w)


def _gateup_kernel(x_ref, wg_ref, wu_ref, o_ref):
    x = x_ref[...]
    g = jnp.dot(x, wg_ref[...].astype(BF16), preferred_element_type=F32)
    u = jnp.dot(x, wu_ref[...].astype(BF16), preferred_element_type=F32)
    o_ref[...] = (g * jax.nn.sigmoid(g) * u).astype(o_ref.dtype)


def gateup(x, w_gate, w_up, layer, bm=2048, bf=256):
    m, k = x.shape
    f = w_gate.shape[2]
    bm, bf = min(bm, m), min(bf, f)
    assert m % bm == 0 and f % bf == 0
    return pl.pallas_call(
        _gateup_kernel,
        grid=(m // bm, f // bf),
        in_specs=[_x_spec(bm, k, True), _w_spec(k, bf, layer), _w_spec(k, bf, layer)],
        out_specs=pl.BlockSpec((bm, bf), lambda i, j: (i, j)),
        out_shape=jax.ShapeDtypeStruct((m, f), BF16),
        compiler_params=_cparams(2, 56),
        name="ffn_gateup",
    )(x, w_gate, w_up)


def ffn(u, w_gate, w_up, w_down, layer):
    hidden = gateup(u, w_gate, w_up, layer)
    return mm(hidden, w_down, layer, BF16, bm=1024, bn=256, single_buffer_x=True, vmem_mib=60, name="ffn_down")


def _rope_tables(pos_ref, invf_ref, width):
    ang = pos_ref[...].astype(F32) * invf_ref[...]
    lane = lax.broadcasted_iota(jnp.int32, ang.shape, 1)
    first_half = (lane & (MLA_ROPE_DIM - 1)) < MLA_ROPE_DIM // 2
    valid = lane < width
    sin = jnp.sin(ang)
    cos_t = jnp.where(valid, jnp.cos(ang), 0.0)
    sin_t = jnp.where(valid, jnp.where(first_half, -sin, sin), 0.0)
    return cos_t, sin_t, first_half


def _rope_apply(t, cos_t, sin_t, first_half):
    half = MLA_ROPE_DIM // 2
    partner = jnp.where(first_half, pltpu.roll(t, LANES - half, 1), pltpu.roll(t, half, 1))
    return t * cos_t + partner * sin_t


def _inv_freq_lanes(width):
    half = MLA_ROPE_DIM // 2
    inv_freq = 1.0 / (ROPE_THETA ** (jnp.arange(half, dtype=F32) / half))
    return jnp.concatenate([jnp.tile(inv_freq, width // half), jnp.zeros((LANES - width,), F32)]).reshape(1, LANES)


def _qb_kernel(cq_ref, g_ref, wn_ref, wp_ref, pos_ref, invf_ref, o_ref):
    cn = _rms(cq_ref[...].astype(F32), g_ref[...]).astype(BF16)
    qn = jnp.dot(cn, wn_ref[...], preferred_element_type=F32)
    qp = jnp.dot(cn, wp_ref[...], preferred_element_type=F32)
    cos_t, sin_t, first_half = _rope_tables(pos_ref, invf_ref, LANES)
    low = lax.broadcasted_iota(jnp.int32, cos_t.shape, 1) < MLA_ROPE_DIM
    for pair in range(MLA_HEADS // 2):
        r = _rope_apply(qp[:, pair * LANES:(pair + 1) * LANES], cos_t, sin_t, first_half)
        pe = (jnp.where(low, r, 0.0), jnp.where(low, pltpu.roll(r, LANES - MLA_ROPE_DIM, 1), 0.0))
        for k in range(2):
            h = 2 * pair + k
            c0 = h * MLA_QK_PAD
            o_ref[:, c0:c0 + MLA_NOPE_DIM] = qn[:, h * MLA_NOPE_DIM:(h + 1) * MLA_NOPE_DIM].astype(o_ref.dtype)
            o_ref[:, c0 + MLA_NOPE_DIM:c0 + MLA_QK_PAD] = pe[k].astype(o_ref.dtype)


def q_b_proj(qkv, g_q_a, w_q_b, pos, invf, bm=512):
    m = qkv.shape[0]
    bm = min(bm, m)
    h, dq = MLA_HEADS, MLA_NOPE_DIM + MLA_ROPE_DIM
    w = w_q_b.reshape(MLA_Q_RANK, h, dq) * (LOG2_E * float(dq) ** -0.5)
    w_nope = w[:, :, :MLA_NOPE_DIM].reshape(MLA_Q_RANK, h * MLA_NOPE_DIM).astype(BF16)
    w_pe = w[:, :, MLA_NOPE_DIM:].reshape(MLA_Q_RANK, h * MLA_ROPE_DIM).astype(BF16)
    cq_block = (3 * NA_WIDTH) // MLA_Q_RANK
    return pl.pallas_call(
        _qb_kernel,
        grid=(m // bm,),
        in_specs=[pl.BlockSpec((bm, MLA_Q_RANK), lambda i: (i, cq_block)),
                  pl.BlockSpec((1, MLA_Q_RANK), lambda i: (0, 0)),
                  pl.BlockSpec((MLA_Q_RANK, h * MLA_NOPE_DIM), lambda i: (0, 0)),
                  pl.BlockSpec((MLA_Q_RANK, h * MLA_ROPE_DIM), lambda i: (0, 0)),
                  pl.BlockSpec((bm, 1), lambda i: (i, 0)),
                  pl.BlockSpec((1, LANES), lambda i: (0, 0))],
        out_specs=pl.BlockSpec((bm, h * MLA_QK_PAD), lambda i: (i, 0)),
        out_shape=jax.ShapeDtypeStruct((m, h * MLA_QK_PAD), BF16),
        compiler_params=_cparams(1, 56),
        name="mla_q_proj",
    )(qkv, _row(g_q_a), w_nope, w_pe, pos, invf)


def _kvb_kernel(ckv_ref, g_ref, wk_ref, wvt_ref, u_ref, wt_ref, pos_ref, invf_ref, k_ref, vt_ref):
    cn = _rms(ckv_ref[...].astype(F32), g_ref[...]).astype(BF16)
    kn = jnp.dot(cn, wk_ref[...], preferred_element_type=F32)
    kr = jnp.dot(u_ref[...], wt_ref[...], preferred_element_type=F32)
    cos_t, sin_t, first_half = _rope_tables(pos_ref, invf_ref, MLA_ROPE_DIM)
    kpe = _rope_apply(kr, cos_t, sin_t, first_half).astype(k_ref.dtype)
    for h in range(MLA_HEADS):
        k_ref[:, h * MLA_QK_PAD:h * MLA_QK_PAD + MLA_NOPE_DIM] = (
            kn[:, h * MLA_NOPE_DIM:(h + 1) * MLA_NOPE_DIM].astype(k_ref.dtype))
        k_ref[:, h * MLA_QK_PAD + MLA_NOPE_DIM:(h + 1) * MLA_QK_PAD] = kpe
    vt = lax.dot_general(wvt_ref[...], cn, (((1,), (1,)), ((), ())), preferred_element_type=F32)
    ones = jnp.ones((MLA_VT_ROWS - MLA_V_DIM, vt.shape[1]), vt_ref.dtype)
    for h in range(MLA_HEADS):
        r0 = h * MLA_VT_ROWS
        vt_ref[r0:r0 + MLA_V_DIM, :] = vt[h * MLA_V_DIM:(h + 1) * MLA_V_DIM].astype(vt_ref.dtype)
        vt_ref[r0 + MLA_V_DIM:r0 + MLA_VT_ROWS, :] = ones


def kv_b_proj(qkv, g_kv_a, w_kv_b, u, w_rope_in, pos, invf, bm=512):
    m, d = u.shape
    bm = min(bm, m)
    h = MLA_HEADS
    ckv_block = (3 * NA_WIDTH + MLA_Q_RANK) // MLA_KV_RANK
    wt = jnp.pad(w_rope_in, ((0, 0), (0, LANES - MLA_ROPE_DIM))).astype(BF16)
    w3 = w_kv_b.reshape(MLA_KV_RANK, h, MLA_NOPE_DIM + MLA_V_DIM)
    w_k = w3[:, :, :MLA_NOPE_DIM].reshape(MLA_KV_RANK, h * MLA_NOPE_DIM).astype(BF16)
    w_vt = w3[:, :, MLA_NOPE_DIM:].reshape(MLA_KV_RANK, h * MLA_V_DIM).T.astype(BF16)
    return pl.pallas_call(
        _kvb_kernel,
        grid=(m // bm,),
        in_specs=[pl.BlockSpec((bm, MLA_KV_RANK), lambda i: (i, ckv_block)),
                  pl.BlockSpec((1, MLA_KV_RANK), lambda i: (0, 0)),
                  pl.BlockSpec((MLA_KV_RANK, h * MLA_NOPE_DIM), lambda i: (0, 0)),
                  pl.BlockSpec((h * MLA_V_DIM, MLA_KV_RANK), lambda i: (0, 0)),
                  pl.BlockSpec((bm, d), lambda i: (i, 0)),
                  pl.BlockSpec((d, LANES), lambda i: (0, 0)),
                  pl.BlockSpec((bm, 1), lambda i: (i, 0)),
                  pl.BlockSpec((1, LANES), lambda i: (0, 0))],
        out_specs=[pl.BlockSpec((bm, h * MLA_QK_PAD), lambda i: (i, 0)),
                   pl.BlockSpec((h * MLA_VT_ROWS, bm), lambda i: (0, i))],
        out_shape=[jax.ShapeDtypeStruct((m, h * MLA_QK_PAD), BF16),
                   jax.ShapeDtypeStruct((h * MLA_VT_ROWS, m), BF16)],
        compiler_params=_cparams(1, 56),
        name="mla_kv_proj",
    )(qkv, _row(g_kv_a), w_k, w_vt, u, wt, pos, invf)


def _softmax_pv(s, v):
    m = jnp.max(s, axis=-1, keepdims=True)
    p = jnp.exp(s - m)
    l = jnp.sum(p, axis=-1, keepdims=True)
    return jnp.dot(p.astype(BF16), v, preferred_element_type=F32) / l


def _mla_kernel(q_ref, qn_ref, k_ref, kn_ref, vt_ref, o_ref, s_scr, *, heads, bq):
    def score_stage(q_tile_ref, row0, keys_ref, slot):
        for h in range(heads):
            qk = slice(h * MLA_QK_PAD, (h + 1) * MLA_QK_PAD)
            s_scr[slot, h] = lax.dot_general(keys_ref[:, qk], q_tile_ref[row0:row0 + bq, qk],
                                             (((1,), (1,)), ((), ())), preferred_element_type=F32)

    def value_stage(slot, row0):
        for h in range(heads):
            s = s_scr[slot, h]
            p = jnp.exp2(s - jnp.max(s, axis=0, keepdims=True)).astype(BF16)
            ot = jnp.dot(vt_ref[h * MLA_VT_ROWS:(h + 1) * MLA_VT_ROWS, :], p,
                         preferred_element_type=F32)
            o = ot[:MLA_V_DIM] / ot[MLA_V_DIM:MLA_V_DIM + 1]
            o_ref[row0:row0 + bq, h * MLA_V_DIM:(h + 1) * MLA_V_DIM] = o.T.astype(o_ref.dtype)

    first_step = (pl.program_id(0) == 0) & (pl.program_id(1) == 0) & (pl.program_id(2) == 0)

    @pl.when(first_step)
    def _():
        score_stage(q_ref, 0, k_ref, 0)

    score_stage(q_ref, bq, k_ref, 1)
    value_stage(0, 0)
    score_stage(qn_ref, 0, kn_ref, 0)
    value_stage(1, bq)


def mla_attention(q, k, vt, batch, seq, bq=512, heads_per_step=2):
    bq = min(bq, seq // 2)
    pairs = seq // (2 * bq)
    g = heads_per_step
    groups = MLA_HEADS // g
    n_steps = batch * groups * pairs

    def next_step(b, h, i):
        flat = jnp.minimum((b * groups + h) * pairs + i + 1, n_steps - 1)
        return flat // (groups * pairs), (flat // pairs) % groups, flat % pairs

    def q_next_map(b, h, i):
        nb, nh, ni = next_step(b, h, i)
        return (nb * pairs + ni) * 2, nh

    def k_next_map(b, h, i):
        nb, nh, _ = next_step(b, h, i)
        return nb, nh

    return pl.pallas_call(
        functools.partial(_mla_kernel, heads=g, bq=bq),
        grid=(batch, groups, pairs),
        in_specs=[pl.BlockSpec((2 * bq, g * MLA_QK_PAD), lambda b, h, i: (b * pairs + i, h)),
                  pl.BlockSpec((bq, g * MLA_QK_PAD), q_next_map),
                  pl.BlockSpec((seq, g * MLA_QK_PAD), lambda b, h, i: (b, h)),
                  pl.BlockSpec((seq, g * MLA_QK_PAD), k_next_map),
                  pl.BlockSpec((g * MLA_VT_ROWS, seq), lambda b, h, i: (h, b))],
        out_specs=pl.BlockSpec((2 * bq, g * MLA_V_DIM), lambda b, h, i: (b * pairs + i, h)),
        out_shape=jax.ShapeDtypeStruct((batch * seq, MLA_WIDTH), BF16),
        scratch_shapes=[pltpu.VMEM((2, g, seq, bq), F32)],
        compiler_params=pltpu.CompilerParams(dimension_semantics=("arbitrary",) * 3,
                                             vmem_limit_bytes=48 * MIB),
        name="mla_attention",
    )(q, q, k, k, vt)


NA_BIAS_ROWS = 2 * NA_KH - 1
NA_BIAS_COLS = 2 * NA_KW - 1
NA_BIAS_PAIRS = NA_BIAS_ROWS - 1


def _na_bias_kernel(rpb_ref, o_ref):
    h = pl.program_id(0)
    row = lax.broadcasted_iota(jnp.int32, (GRID_W, LANES), 0)
    lane = lax.broadcasted_iota(jnp.int32, (GRID_W, LANES), 1)
    kc = lane & (GRID_W - 1)
    col_idx = kc - row + (NA_KW - 1)
    left = lane < GRID_W
    cstart = jnp.clip(row - NA_KW // 2, 0, GRID_W - NA_KW)
    in_window = (kc >= cstart) & (kc < cstart + NA_KW)

    def body(d, carry):
        base = (h * NA_BIAS_ROWS + d) * NA_BIAS_COLS
        acc = jnp.zeros((GRID_W, LANES), F32)
        for j in range(NA_BIAS_COLS):
            val = jnp.where(left, rpb_ref[base + j], rpb_ref[base + NA_BIAS_COLS + j])
            acc = acc + jnp.where(col_idx == j, val, 0.0)
        o_ref[0, d] = jnp.where(in_window, acc * LOG2_E, NEG_INF)
        return carry

    lax.fori_loop(0, NA_BIAS_PAIRS, body, 0)


def na_bias_table(rpb):
    return pl.pallas_call(
        _na_bias_kernel,
        grid=(NA_HEADS,),
        in_specs=[pl.BlockSpec(memory_space=pltpu.SMEM)],
        out_specs=pl.BlockSpec((1, NA_BIAS_PAIRS, GRID_W, LANES), lambda h: (h, 0, 0, 0)),
        out_shape=jax.ShapeDtypeStruct((NA_HEADS, NA_BIAS_PAIRS, GRID_W, LANES), F32),
        compiler_params=_cparams(1, 16),
        name="na_bias_table",
    )(rpb.reshape(-1).astype(F32))


def _na_kernel(q_ref, k_ref, v_ref, bias_ref, o_ref, s_scr, *, rows, heads, rows_per_step):
    nk = NA_KH * GRID_W
    n_steps = rows // rows_per_step

    def row_geometry(step, rr):
        r = step * rows_per_step + rr
        rs = jnp.clip(r - NA_KH // 2, 0, rows - NA_KH)
        d0 = rs - r + (NA_KH - 1)
        return pl.multiple_of(r * GRID_W, GRID_W), pl.multiple_of(rs * GRID_W, GRID_W), d0

    def score_stage(step, slot):
        for rr in range(rows_per_step):
            q0, k0, d0 = row_geometry(step, rr)
            for h in range(heads):
                cols = slice(h * NA_HEAD_DIM, (h + 1) * NA_HEAD_DIM)
                q = q_ref[pl.ds(q0, GRID_W), cols]
                k = k_ref[pl.ds(k0, nk), cols]
                s = lax.dot_general(q, k, (((1,), (1,)), ((), ())), preferred_element_type=F32)
                bias = jnp.concatenate([bias_ref[h, d0 + 2 * p] for p in range(NA_KH // 2)], axis=1)
                s_scr[slot, rr * heads + h] = s + bias

    def value_stage(step, slot):
        for rr in range(rows_per_step):
            q0, k0, _ = row_geometry(step, rr)
            for h in range(heads):
                cols = slice(h * NA_HEAD_DIM, (h + 1) * NA_HEAD_DIM)
                s = s_scr[slot, rr * heads + h]
                p = jnp.exp2(s - jnp.max(s, axis=-1, keepdims=True))
                l = jnp.sum(p, axis=-1, keepdims=True)
                o = jnp.dot(p.astype(BF16), v_ref[pl.ds(k0, nk), cols], preferred_element_type=F32)
                o_ref[pl.ds(q0, GRID_W), cols] = (o / l).astype(o_ref.dtype)

    score_stage(0, 0)

    def body(u, carry):
        score_stage(2 * u + 1, 1)
        value_stage(2 * u, 0)
        score_stage(2 * u + 2, 0)
        value_stage(2 * u + 1, 1)
        return carry

    lax.fori_loop(0, n_steps // 2 - 1, body, 0)
    score_stage(n_steps - 1, 1)
    value_stage(n_steps - 2, 0)
    value_stage(n_steps - 1, 1)


def na_attention(qkv, bias, batch, seq, heads_per_step=4, rows_per_step=2):
    g = heads_per_step
    w = g * NA_HEAD_DIM
    groups = NA_HEADS // g
    rows = seq // GRID_W
    assert rows % rows_per_step == 0 and rows >= NA_KH
    return pl.pallas_call(
        functools.partial(_na_kernel, rows=rows, heads=g, rows_per_step=rows_per_step),
        grid=(batch, groups),
        in_specs=[pl.BlockSpec((seq, w), lambda b, j: (b, j)),
                  pl.BlockSpec((seq, w), lambda b, j: (b, groups + j)),
                  pl.BlockSpec((seq, w), lambda b, j: (b, 2 * groups + j)),
                  pl.BlockSpec((g, NA_BIAS_PAIRS, GRID_W, LANES), lambda b, j: (j, 0, 0, 0))],
        out_specs=pl.BlockSpec((seq, w), lambda b, j: (b, j)),
        out_shape=jax.ShapeDtypeStruct((batch * seq, NA_WIDTH), BF16),
        scratch_shapes=[pltpu.VMEM((2, rows_per_step * g, GRID_W, NA_KH * GRID_W), F32)],
        compiler_params=_cparams(2, 32),
        name="na_attention",
    )(qkv, qkv, qkv, bias)


def _mem_block_kernel(h_ref, o_ref, g_mix_ref, g_pre_ref, wq_ref, kv_ref, wo_ref, g_post_ref, g_next_ref,
                      ho_ref, uo_ref, *, scale):
    h1 = h_ref[...] + _rms(o_ref[...].astype(F32), g_mix_ref[...])
    u = _rms(h1, g_pre_ref[...]).astype(BF16)
    q = (jnp.dot(u, wq_ref[...], preferred_element_type=F32) * scale).astype(BF16)
    outs = []
    for hd in range(MEM_HEADS):
        qh = q[:, hd * MEM_HEAD_DIM:(hd + 1) * MEM_HEAD_DIM]
        k = kv_ref[:, 2 * hd * MEM_HEAD_DIM:(2 * hd + 1) * MEM_HEAD_DIM]
        v = kv_ref[:, (2 * hd + 1) * MEM_HEAD_DIM:(2 * hd + 2) * MEM_HEAD_DIM]
        s = lax.dot_general(qh, k, (((1,), (1,)), ((), ())), preferred_element_type=F32)
        p = jnp.exp2(s - jnp.max(s, axis=-1, keepdims=True))
        l = jnp.sum(p, axis=-1, keepdims=True)
        outs.append((jnp.dot(p.astype(BF16), v, preferred_element_type=F32) / l).astype(BF16))
    a = jnp.dot(jnp.concatenate(outs, axis=1), wo_ref[...], preferred_element_type=F32)
    h2 = h1 + _rms(a, g_post_ref[...])
    ho_ref[...] = h2
    uo_ref[...] = _rms(h2, g_next_ref[...]).astype(uo_ref.dtype)


def mem_block(h, o, g_mix_post, g_pre, w_q, kv, w_o, g_post, g_next, batch, seq, bq=256):
    mem_len = kv.shape[0] // batch
    d = h.shape[1]
    bq = min(bq, seq)
    nq = seq // bq
    width = MEM_HEADS * MEM_HEAD_DIM
    row = pl.BlockSpec((bq, d), lambda b, i: (b * nq + i, 0))
    vec = pl.BlockSpec((1, d), lambda b, i: (0, 0))
    once = pl.Buffered(1)
    return pl.pallas_call(
        functools.partial(_mem_block_kernel, scale=LOG2_E * float(MEM_HEAD_DIM) ** -0.5),
        grid=(batch, nq),
        in_specs=[row, row, vec, vec,
                  pl.BlockSpec((d, width), lambda b, i: (0, 0), pipeline_mode=once),
                  pl.BlockSpec((mem_len, 2 * width), lambda b, i: (b, 0)),
                  pl.BlockSpec((width, d), lambda b, i: (0, 0), pipeline_mode=once),
                  vec, vec],
        out_specs=[row, row],
        out_shape=[jax.ShapeDtypeStruct((batch * seq, d), F32), jax.ShapeDtypeStruct((batch * seq, d), BF16)],
        compiler_params=_cparams(2, 56),
        name="mem_block",
    )(h, o, _row(g_mix_post), _row(g_pre), w_q.astype(BF16), kv, w_o.astype(BF16), _row(g_post), _row(g_next))


def kernel(x, mem, positions, ffn1_w_gate, ffn1_w_up, ffn1_w_down, g_ffn1, w_in, g_q_a, w_q_b, g_kv_a, w_kv_b, na_rpb, w_out, g_mix, g_mem_in, w_mem_q, w_mem_kv, w_mem_o, g_mem_attn, ffn2_w_gate, ffn2_w_up, ffn2_w_down, g_ffn2, g_final):
    batch, seq, d = x.shape
    m = batch * seq
    depth = ffn1_w_gate.shape[0]
    pos = positions.reshape(m, 1).astype(jnp.int32)
    invf_q, invf_k = _inv_freq_lanes(LANES), _inv_freq_lanes(MLA_ROPE_DIM)
    mem2 = mem.reshape(-1, d)

    h = x.reshape(m, d)
    u = norm_cast(h, g_ffn1[0, 0])
    out = None
    for l in range(depth):
        f = ffn(u, ffn1_w_gate, ffn1_w_up, ffn1_w_down, l)
        h, u = resid_norm(h, f, g_ffn1[l, 1], FFN_RES_WEIGHT, g_mix[l, 0])

        qkv = mm_nt(u, jnp.swapaxes(w_in, 1, 2), l, BF16, bm=2048, bn=512, n_cols=IN_PROJ_MAIN,
                    scaled_cols=NA_WIDTH, scale=LOG2_E * float(NA_HEAD_DIM) ** -0.5,
                    single_buffer_x=True, name="in_proj")
        q_cat = q_b_proj(qkv, g_q_a[l], w_q_b[l], pos, invf_q)
        k_cat, v_mla = kv_b_proj(qkv, g_kv_a[l], w_kv_b[l], u, w_in[l, :, IN_PROJ_MAIN:], pos, invf_k)
        o_mla = mla_attention(q_cat, k_cat, v_mla, batch, seq)
        o_na = na_attention(qkv, na_bias_table(na_rpb[l]), batch, seq)
        o = mm2(o_na, o_mla, w_out, l, BF16, bm=2048, bn=512, single_buffer_x=True, name="out_proj")

        mem_n = norm_cast(mem2, g_mem_in[l])
        kv_mem = mm(mem_n, w_mem_kv, l, BF16, bm=1024, bn=512, name="mem_kv_proj")
        h, u = mem_block(h, o, g_mix[l, 1], g_mem_attn[l, 0], w_mem_q[l], kv_mem, w_mem_o[l],
                         g_mem_attn[l, 1], g_ffn2[l, 0], batch, seq)

        f = ffn(u, ffn2_w_gate, ffn2_w_up, ffn2_w_down, l)
        if l + 1 < depth:
            h = resid_final(h, f, g_ffn2[l, 1], FFN_RES_WEIGHT, g_final[l])
            u = norm_cast(h, g_ffn1[l + 1, 0])
        else:
            out = resid_final(h, f, g_ffn2[l, 1], FFN_RES_WEIGHT, g_final[l])
    return out.reshape(batch, seq, d)
```

```python
import functools

import jax
import jax.numpy as jnp
from jax import lax
from jax.experimental import pallas as pl
from jax.experimental.pallas import tpu as pltpu

F32 = jnp.float32
BF16 = jnp.bfloat16

GRID_W = 64
NA_HEADS = 16
NA_HEAD_DIM = 128
NA_KH = 8
NA_KW = 16
MLA_HEADS = 16
MLA_Q_RANK = 1024
MLA_KV_RANK = 512
MLA_NOPE_DIM = 128
MLA_ROPE_DIM = 64
MLA_V_DIM = 128
MLA_QK_PAD = 256
MLA_VT_ROWS = MLA_V_DIM + 16
LOG2_E = 1.4426950408889634
ROPE_THETA = 10000.0
MEM_HEADS = 4
MEM_HEAD_DIM = 128
FFN_RES_WEIGHT = 0.5
NORM_EPS = 1e-6
NEG_INF = -1e30
NA_WIDTH = NA_HEADS * NA_HEAD_DIM
MLA_WIDTH = MLA_HEADS * MLA_V_DIM
IN_PROJ_MAIN = 3 * NA_WIDTH + MLA_Q_RANK + MLA_KV_RANK
LANES = 128
MIB = 1024 * 1024
VMEM_SMALL_MIB = 32
VMEM_MIB = 56
VMEM_MAX_MIB = 60


def _cparams(n_axes, vmem_mib):
    return pltpu.CompilerParams(dimension_semantics=("parallel",) * n_axes,
                                vmem_limit_bytes=vmem_mib * MIB)


def _rms(x, g):
    return x * lax.rsqrt(jnp.mean(x * x, axis=-1, keepdims=True) + NORM_EPS) * g


def _row(v):
    return v.reshape(1, -1).astype(F32)


def _norm_cast_kernel(x_ref, g_ref, o_ref):
    o_ref[...] = _rms(x_ref[...], g_ref[...]).astype(o_ref.dtype)


def norm_cast(x, g, bm=512):
    m, d = x.shape
    bm = min(bm, m)
    return pl.pallas_call(
        _norm_cast_kernel,
        grid=(m // bm,),
        in_specs=[pl.BlockSpec((bm, d), lambda i: (i, 0)), pl.BlockSpec((1, d), lambda i: (0, 0))],
        out_specs=pl.BlockSpec((bm, d), lambda i: (i, 0)),
        out_shape=jax.ShapeDtypeStruct((m, d), BF16),
        compiler_params=_cparams(1, VMEM_SMALL_MIB),
        name="norm_cast",
    )(x, _row(g))


def _resid_norm_kernel(h_ref, f_ref, gp_ref, gn_ref, ho_ref, uo_ref, *, weight):
    h = h_ref[...] + weight * _rms(f_ref[...].astype(F32), gp_ref[...])
    ho_ref[...] = h
    uo_ref[...] = _rms(h, gn_ref[...]).astype(uo_ref.dtype)


def resid_norm(h, f, g_post, weight, g_next, bm=256):
    m, d = h.shape
    bm = min(bm, m)
    row = pl.BlockSpec((bm, d), lambda i: (i, 0))
    vec = pl.BlockSpec((1, d), lambda i: (0, 0))
    return pl.pallas_call(
        functools.partial(_resid_norm_kernel, weight=weight),
        grid=(m // bm,),
        in_specs=[row, row, vec, vec],
        out_specs=[row, row],
        out_shape=[jax.ShapeDtypeStruct((m, d), F32), jax.ShapeDtypeStruct((m, d), BF16)],
        compiler_params=_cparams(1, VMEM_MIB),
        name="resid_norm",
    )(h, f, _row(g_post), _row(g_next))


def _resid_final_kernel(h_ref, f_ref, gp_ref, gn_ref, o_ref, *, weight):
    h = h_ref[...] + weight * _rms(f_ref[...].astype(F32), gp_ref[...])
    o_ref[...] = _rms(h, gn_ref[...])


def resid_final(h, f, g_post, weight, g_final, bm=512):
    m, d = h.shape
    bm = min(bm, m)
    row = pl.BlockSpec((bm, d), lambda i: (i, 0))
    vec = pl.BlockSpec((1, d), lambda i: (0, 0))
    return pl.pallas_call(
        functools.partial(_resid_final_kernel, weight=weight),
        grid=(m // bm,),
        in_specs=[row, row, vec, vec],
        out_specs=row,
        out_shape=jax.ShapeDtypeStruct((m, d), F32),
        compiler_params=_cparams(1, VMEM_MIB),
        name="resid_final",
    )(h, f, _row(g_post), _row(g_final))


def _x_spec(bm, k, single_buffer):
    if single_buffer:
        return pl.BlockSpec((bm, k), lambda i, j: (i, 0), pipeline_mode=pl.Buffered(1))
    return pl.BlockSpec((bm, k), lambda i, j: (i, 0))


def _w_spec(k, bn, layer):
    return pl.BlockSpec((None, k, bn), lambda i, j: (layer, 0, j))


def _mm_kernel(x_ref, w_ref, o_ref):
    o_ref[...] = jnp.dot(x_ref[...], w_ref[...].astype(BF16),
                         preferred_element_type=F32).astype(o_ref.dtype)


def mm(x, w, layer, out_dtype, bm, bn, n_cols=None, single_buffer_x=False, vmem_mib=VMEM_MIB, name="mm"):
    m, k = x.shape
    n = w.shape[2] if n_cols is None else n_cols
    bm, bn = min(bm, m), min(bn, n)
    assert m % bm == 0 and n % bn == 0 and w.shape[1] == k
    return pl.pallas_call(
        _mm_kernel,
        grid=(m // bm, n // bn),
        in_specs=[_x_spec(bm, k, single_buffer_x), _w_spec(k, bn, layer)],
        out_specs=pl.BlockSpec((bm, bn), lambda i, j: (i, j)),
        out_shape=jax.ShapeDtypeStruct((m, n), out_dtype),
        compiler_params=_cparams(2, vmem_mib),
        name=name,
    )(x, w)


def _mm_nt_kernel(x_ref, wt_ref, o_ref, *, scaled_blocks, scale):
    acc = lax.dot_general(x_ref[...], wt_ref[...].astype(BF16), (((1,), (1,)), ((), ())),
                          preferred_element_type=F32)
    if scaled_blocks:
        acc = acc * jnp.where(pl.program_id(1) < scaled_blocks, scale, 1.0)
    o_ref[...] = acc.astype(o_ref.dtype)


def mm_nt(x, wt, layer, out_dtype, bm, bn, n_cols=None, scaled_cols=0, scale=1.0, single_buffer_x=False,
          vmem_mib=VMEM_MIB, name="mm_nt"):
    m, k = x.shape
    n = wt.shape[1] if n_cols is None else n_cols
    bm, bn = min(bm, m), min(bn, n)
    assert m % bm == 0 and n % bn == 0 and wt.shape[2] == k and scaled_cols % bn == 0
    return pl.pallas_call(
        functools.partial(_mm_nt_kernel, scaled_blocks=scaled_cols // bn, scale=scale),
        grid=(m // bm, n // bn),
        in_specs=[_x_spec(bm, k, single_buffer_x), pl.BlockSpec((None, bn, k), lambda i, j: (layer, j, 0))],
        out_specs=pl.BlockSpec((bm, bn), lambda i, j: (i, j)),
        out_shape=jax.ShapeDtypeStruct((m, n), out_dtype),
        compiler_params=_cparams(2, vmem_mib),
        name=name,
    )(x, wt)


def _mm2_kernel(xa_ref, xb_ref, w_ref, o_ref):
    ka = xa_ref.shape[1]
    w = w_ref[...].astype(BF16)
    acc = jnp.dot(xa_ref[...], w[:ka], preferred_element_type=F32)
    acc = acc + jnp.dot(xb_ref[...], w[ka:], preferred_element_type=F32)
    o_ref[...] = acc.astype(o_ref.dtype)


def mm2(xa, xb, w, layer, out_dtype, bm, bn, single_buffer_x=False, vmem_mib=VMEM_MIB, name="mm2"):
    m, ka = xa.shape
    kb = xb.shape[1]
    n = w.shape[2]
    bm, bn = min(bm, m), min(bn, n)
    assert m % bm == 0 and n % bn == 0 and w.shape[1] == ka + kb
    return pl.pallas_call(
        _mm2_kernel,
        grid=(m // bm, n // bn),
        in_specs=[_x_spec(bm, ka, single_buffer_x), _x_spec(bm, kb, single_buffer_x),
                  _w_spec(ka + kb, bn, layer)],
        out_specs=pl.BlockSpec((bm, bn), lambda i, j: (i, j)),
        out_shape=jax.ShapeDtypeStruct((m, n), out_dtype),
        compiler_params=_cparams(2, vmem_mib),
        name=name,
    )(xa, xb, w)


def _gateup_kernel(x_ref, wg_ref, wu_ref, wd_ref, o_ref, wdo_ref):
    x = x_ref[...]
    g = jnp.dot(x, wg_ref[...].astype(BF16), preferred_element_type=F32)
    u = jnp.dot(x, wu_ref[...].astype(BF16), preferred_element_type=F32)
    o_ref[...] = (g * jax.nn.sigmoid(g) * u).astype(o_ref.dtype)
    wdo_ref[...] = wd_ref[...].astype(wdo_ref.dtype)


def gateup(x, w_gate, w_up, w_down, layer, bm=2048, bf=256):
    m, k = x.shape
    f = w_gate.shape[2]
    d_out = w_down.shape[2]
    bm, bf = min(bm, m), min(bf, f)
    assert m % bm == 0 and f % bf == 0 and w_down.shape[1] == f
    n_j = f // bf
    slab = f // ((m // bm) * n_j)
    assert slab * (m // bm) * n_j == f and slab % 16 == 0
    return pl.pallas_call(
        _gateup_kernel,
        grid=(m // bm, n_j),
        in_specs=[_x_spec(bm, k, True), _w_spec(k, bf, layer), _w_spec(k, bf, layer),
                  pl.BlockSpec((None, slab, d_out), lambda i, j: (layer, i * n_j + j, 0))],
        out_specs=[pl.BlockSpec((bm, bf), lambda i, j: (i, j)),
                   pl.BlockSpec((None, slab, d_out), lambda i, j: (0, i * n_j + j, 0))],
        out_shape=[jax.ShapeDtypeStruct((m, f), BF16), jax.ShapeDtypeStruct((1, f, d_out), BF16)],
        compiler_params=_cparams(2, VMEM_MIB),
        name="ffn_gateup",
    )(x, w_gate, w_up, w_down)


def ffn(u, w_gate, w_up, w_down, layer):
    hidden, w_down_bf16 = gateup(u, w_gate, w_up, w_down, layer)
    return mm(hidden, w_down_bf16, 0, BF16, bm=512, bn=512, name="ffn_down")


def _rope_tables(pos_ref, invf_ref, width):
    ang = pos_ref[...].astype(F32) * invf_ref[...]
    lane = lax.broadcasted_iota(jnp.int32, ang.shape, 1)
    first_half = (lane & (MLA_ROPE_DIM - 1)) < MLA_ROPE_DIM // 2
    valid = lane < width
    sin = jnp.sin(ang)
    cos_t = jnp.where(valid, jnp.cos(ang), 0.0)
    sin_t = jnp.where(valid, jnp.where(first_half, -sin, sin), 0.0)
    return cos_t, sin_t, first_half


def _rope_apply(t, cos_t, sin_t, first_half):
    half = MLA_ROPE_DIM // 2
    partner = jnp.where(first_half, pltpu.roll(t, LANES - half, 1), pltpu.roll(t, half, 1))
    return t * cos_t + partner * sin_t


def _inv_freq_lanes(width):
    half = MLA_ROPE_DIM // 2
    inv_freq = 1.0 / (ROPE_THETA ** (jnp.arange(half, dtype=F32) / half))
    return jnp.concatenate([jnp.tile(inv_freq, width // half), jnp.zeros((LANES - width,), F32)]).reshape(1, LANES)


def _qb_kernel(cq_ref, g_ref, wn_ref, wp_ref, pos_ref, invf_ref, o_ref):
    cn = _rms(cq_ref[...].astype(F32), g_ref[...]).astype(BF16)
    qn = jnp.dot(cn, wn_ref[...], preferred_element_type=F32)
    qp = jnp.dot(cn, wp_ref[...], preferred_element_type=F32)
    cos_t, sin_t, first_half = _rope_tables(pos_ref, invf_ref, LANES)
    low = lax.broadcasted_iota(jnp.int32, cos_t.shape, 1) < MLA_ROPE_DIM
    for pair in range(MLA_HEADS // 2):
        r = _rope_apply(qp[:, pair * LANES:(pair + 1) * LANES], cos_t, sin_t, first_half)
        pe = (jnp.where(low, r, 0.0), jnp.where(low, pltpu.roll(r, LANES - MLA_ROPE_DIM, 1), 0.0))
        for k in range(2):
            h = 2 * pair + k
            c0 = h * MLA_QK_PAD
            o_ref[:, c0:c0 + MLA_NOPE_DIM] = qn[:, h * MLA_NOPE_DIM:(h + 1) * MLA_NOPE_DIM].astype(o_ref.dtype)
            o_ref[:, c0 + MLA_NOPE_DIM:c0 + MLA_QK_PAD] = pe[k].astype(o_ref.dtype)


def q_b_proj(qkv, g_q_a, w_q_b, pos, invf, bm=512):
    m = qkv.shape[0]
    bm = min(bm, m)
    h, dq = MLA_HEADS, MLA_NOPE_DIM + MLA_ROPE_DIM
    w = w_q_b.reshape(MLA_Q_RANK, h, dq) * (LOG2_E * float(dq) ** -0.5)
    w_nope = w[:, :, :MLA_NOPE_DIM].reshape(MLA_Q_RANK, h * MLA_NOPE_DIM).astype(BF16)
    w_pe = w[:, :, MLA_NOPE_DIM:].reshape(MLA_Q_RANK, h * MLA_ROPE_DIM).astype(BF16)
    cq_block = (3 * NA_WIDTH) // MLA_Q_RANK
    return pl.pallas_call(
        _qb_kernel,
        grid=(m // bm,),
        in_specs=[pl.BlockSpec((bm, MLA_Q_RANK), lambda i: (i, cq_block)),
                  pl.BlockSpec((1, MLA_Q_RANK), lambda i: (0, 0)),
                  pl.BlockSpec((MLA_Q_RANK, h * MLA_NOPE_DIM), lambda i: (0, 0)),
                  pl.BlockSpec((MLA_Q_RANK, h * MLA_ROPE_DIM), lambda i: (0, 0)),
                  pl.BlockSpec((bm, 1), lambda i: (i, 0)),
                  pl.BlockSpec((1, LANES), lambda i: (0, 0))],
        out_specs=pl.BlockSpec((bm, h * MLA_QK_PAD), lambda i: (i, 0)),
        out_shape=jax.ShapeDtypeStruct((m, h * MLA_QK_PAD), BF16),
        compiler_params=_cparams(1, VMEM_MIB),
        name="mla_q_proj",
    )(qkv, _row(g_q_a), w_nope, w_pe, pos, invf)


def _kvb_kernel(ckv_ref, g_ref, wk_ref, wvt_ref, u_ref, wt_ref, pos_ref, invf_ref, k_ref, vt_ref):
    cn = _rms(ckv_ref[...].astype(F32), g_ref[...]).astype(BF16)
    kn = jnp.dot(cn, wk_ref[...], preferred_element_type=F32)
    kr = jnp.dot(u_ref[...], wt_ref[...], preferred_element_type=F32)
    cos_t, sin_t, first_half = _rope_tables(pos_ref, invf_ref, MLA_ROPE_DIM)
    kpe = _rope_apply(kr, cos_t, sin_t, first_half).astype(k_ref.dtype)
    for h in range(MLA_HEADS):
        k_ref[:, h * MLA_QK_PAD:h * MLA_QK_PAD + MLA_NOPE_DIM] = (
            kn[:, h * MLA_NOPE_DIM:(h + 1) * MLA_NOPE_DIM].astype(k_ref.dtype))
        k_ref[:, h * MLA_QK_PAD + MLA_NOPE_DIM:(h + 1) * MLA_QK_PAD] = kpe
    vt = lax.dot_general(wvt_ref[...], cn, (((1,), (1,)), ((), ())), preferred_element_type=F32)
    ones = jnp.ones((MLA_VT_ROWS - MLA_V_DIM, vt.shape[1]), vt_ref.dtype)
    for h in range(MLA_HEADS):
        r0 = h * MLA_VT_ROWS
        vt_ref[r0:r0 + MLA_V_DIM, :] = vt[h * MLA_V_DIM:(h + 1) * MLA_V_DIM].astype(vt_ref.dtype)
        vt_ref[r0 + MLA_V_DIM:r0 + MLA_VT_ROWS, :] = ones


def kv_b_proj(qkv, g_kv_a, w_kv_b, u, w_rope_in, pos, invf, bm=512):
    m, d = u.shape
    bm = min(bm, m)
    h = MLA_HEADS
    ckv_block = (3 * NA_WIDTH + MLA_Q_RANK) // MLA_KV_RANK
    wt = jnp.pad(w_rope_in, ((0, 0), (0, LANES - MLA_ROPE_DIM))).astype(BF16)
    w3 = w_kv_b.reshape(MLA_KV_RANK, h, MLA_NOPE_DIM + MLA_V_DIM)
    w_k = w3[:, :, :MLA_NOPE_DIM].reshape(MLA_KV_RANK, h * MLA_NOPE_DIM).astype(BF16)
    w_vt = w3[:, :, MLA_NOPE_DIM:].reshape(MLA_KV_RANK, h * MLA_V_DIM).T.astype(BF16)
    return pl.pallas_call(
        _kvb_kernel,
        grid=(m // bm,),
        in_specs=[pl.BlockSpec((bm, MLA_KV_RANK), lambda i: (i, ckv_block)),
                  pl.BlockSpec((1, MLA_KV_RANK), lambda i: (0, 0)),
                  pl.BlockSpec((MLA_KV_RANK, h * MLA_NOPE_DIM), lambda i: (0, 0)),
                  pl.BlockSpec((h * MLA_V_DIM, MLA_KV_RANK), lambda i: (0, 0)),
                  pl.BlockSpec((bm, d), lambda i: (i, 0)),
                  pl.BlockSpec((d, LANES), lambda i: (0, 0)),
                  pl.BlockSpec((bm, 1), lambda i: (i, 0)),
                  pl.BlockSpec((1, LANES), lambda i: (0, 0))],
        out_specs=[pl.BlockSpec((bm, h * MLA_QK_PAD), lambda i: (i, 0)),
                   pl.BlockSpec((h * MLA_VT_ROWS, bm), lambda i: (0, i))],
        out_shape=[jax.ShapeDtypeStruct((m, h * MLA_QK_PAD), BF16),
                   jax.ShapeDtypeStruct((h * MLA_VT_ROWS, m), BF16)],
        compiler_params=_cparams(1, VMEM_MIB),
        name="mla_kv_proj",
    )(qkv, _row(g_kv_a), w_k, w_vt, u, wt, pos, invf)


def _mla_kernel(q_ref, qn_ref, k_ref, kn_ref, vt_ref, o_ref, s_scr, *, heads, bq):
    def score_stage(q_tile_ref, row0, keys_ref, slot):
        for h in range(heads):
            qk = slice(h * MLA_QK_PAD, (h + 1) * MLA_QK_PAD)
            s_scr[slot, h] = lax.dot_general(keys_ref[:, qk], q_tile_ref[row0:row0 + bq, qk],
                                             (((1,), (1,)), ((), ())), preferred_element_type=F32)

    def value_stage(slot, row0):
        for h in range(heads):
            s = s_scr[slot, h]
            p = jnp.exp2(s - jnp.max(s, axis=0, keepdims=True)).astype(BF16)
            ot = jnp.dot(vt_ref[h * MLA_VT_ROWS:(h + 1) * MLA_VT_ROWS, :], p,
                         preferred_element_type=F32)
            o = ot[:MLA_V_DIM] / ot[MLA_V_DIM:MLA_V_DIM + 1]
            o_ref[row0:row0 + bq, h * MLA_V_DIM:(h + 1) * MLA_V_DIM] = o.T.astype(o_ref.dtype)

    first_step = (pl.program_id(0) == 0) & (pl.program_id(1) == 0) & (pl.program_id(2) == 0)

    @pl.when(first_step)
    def _():
        score_stage(q_ref, 0, k_ref, 0)

    score_stage(q_ref, bq, k_ref, 1)
    value_stage(0, 0)
    score_stage(qn_ref, 0, kn_ref, 0)
    value_stage(1, bq)


def mla_attention(q, k, vt, batch, seq, bq=512, heads_per_step=2):
    bq = min(bq, seq // 2)
    pairs = seq // (2 * bq)
    g = heads_per_step
    groups = MLA_HEADS // g
    n_steps = batch * groups * pairs

    def next_step(b, h, i):
        flat = jnp.minimum((b * groups + h) * pairs + i + 1, n_steps - 1)
        return flat // (groups * pairs), (flat // pairs) % groups, flat % pairs

    def q_next_map(b, h, i):
        nb, nh, ni = next_step(b, h, i)
        return (nb * pairs + ni) * 2, nh

    def k_next_map(b, h, i):
        nb, nh, _ = next_step(b, h, i)
        return nb, nh

    return pl.pallas_call(
        functools.partial(_mla_kernel, heads=g, bq=bq),
        grid=(batch, groups, pairs),
        in_specs=[pl.BlockSpec((2 * bq, g * MLA_QK_PAD), lambda b, h, i: (b * pairs + i, h)),
                  pl.BlockSpec((bq, g * MLA_QK_PAD), q_next_map),
                  pl.BlockSpec((seq, g * MLA_QK_PAD), lambda b, h, i: (b, h)),
                  pl.BlockSpec((seq, g * MLA_QK_PAD), k_next_map),
                  pl.BlockSpec((g * MLA_VT_ROWS, seq), lambda b, h, i: (h, b))],
        out_specs=pl.BlockSpec((2 * bq, g * MLA_V_DIM), lambda b, h, i: (b * pairs + i, h)),
        out_shape=jax.ShapeDtypeStruct((batch * seq, MLA_WIDTH), BF16),
        scratch_shapes=[pltpu.VMEM((2, g, seq, bq), F32)],
        compiler_params=pltpu.CompilerParams(dimension_semantics=("arbitrary",) * 3,
                                             vmem_limit_bytes=VMEM_MIB * MIB),
        name="mla_attention",
    )(q, q, k, k, vt)


NA_BIAS_ROWS = 2 * NA_KH - 1
NA_BIAS_COLS = 2 * NA_KW - 1
NA_BIAS_PAIRS = NA_BIAS_ROWS - 1


def _na_bias_kernel(rpb_ref, o_ref):
    row = lax.broadcasted_iota(jnp.int32, (GRID_W, LANES), 0)
    lane = lax.broadcasted_iota(jnp.int32, (GRID_W, LANES), 1)
    kc = lane & (GRID_W - 1)
    left = lane < GRID_W
    cstart = jnp.clip(row - NA_KW // 2, 0, GRID_W - NA_KW)
    in_window = (kc >= cstart) & (kc < cstart + NA_KW)
    toeplitz = []
    for d in range(NA_BIAS_ROWS):
        vec = jnp.broadcast_to(rpb_ref[d:d + 1, :], (GRID_W, LANES))
        toeplitz.append(pltpu.roll(vec, LANES - (NA_KW - 1), 1, stride=1, stride_axis=0))
    for d in range(NA_BIAS_PAIRS):
        tile = jnp.where(left, toeplitz[d], pltpu.roll(toeplitz[d + 1], GRID_W, 1))
        o_ref[0, d] = jnp.where(in_window, tile * LOG2_E, NEG_INF)


def na_bias_table(rpb):
    rpb_lanes = jnp.pad(rpb.astype(F32), ((0, 0), (0, 0), (0, LANES - NA_BIAS_COLS)))
    return pl.pallas_call(
        _na_bias_kernel,
        grid=(NA_HEADS,),
        in_specs=[pl.BlockSpec((None, NA_BIAS_ROWS, LANES), lambda h: (h, 0, 0))],
        out_specs=pl.BlockSpec((1, NA_BIAS_PAIRS, GRID_W, LANES), lambda h: (h, 0, 0, 0)),
        out_shape=jax.ShapeDtypeStruct((NA_HEADS, NA_BIAS_PAIRS, GRID_W, LANES), F32),
        compiler_params=_cparams(1, VMEM_SMALL_MIB),
        name="na_bias_table",
    )(rpb_lanes)


def _na_kernel(q_ref, k_ref, v_ref, bias_ref, o_ref, s_scr, *, rows, heads, rows_per_step):
    nk = NA_KH * GRID_W
    n_steps = rows // rows_per_step

    def row_geometry(step, rr):
        r = step * rows_per_step + rr
        rs = jnp.clip(r - NA_KH // 2, 0, rows - NA_KH)
        d0 = rs - r + (NA_KH - 1)
        return pl.multiple_of(r * GRID_W, GRID_W), pl.multiple_of(rs * GRID_W, GRID_W), d0

    def score_stage(step, slot):
        for rr in range(rows_per_step):
            q0, k0, d0 = row_geometry(step, rr)
            for h in range(heads):
                cols = slice(h * NA_HEAD_DIM, (h + 1) * NA_HEAD_DIM)
                q = q_ref[pl.ds(q0, GRID_W), cols]
                k = k_ref[pl.ds(k0, nk), cols]
                s = lax.dot_general(q, k, (((1,), (1,)), ((), ())), preferred_element_type=F32)
                bias = jnp.concatenate([bias_ref[h, d0 + 2 * p] for p in range(NA_KH // 2)], axis=1)
                s_scr[slot, rr * heads + h] = s + bias

    def value_stage(step, slot):
        for rr in range(rows_per_step):
            q0, k0, _ = row_geometry(step, rr)
            for h in range(heads):
                cols = slice(h * NA_HEAD_DIM, (h + 1) * NA_HEAD_DIM)
                s = s_scr[slot, rr * heads + h]
                p = jnp.exp2(s - jnp.max(s, axis=-1, keepdims=True))
                l = jnp.sum(p, axis=-1, keepdims=True)
                o = jnp.dot(p.astype(BF16), v_ref[pl.ds(k0, nk), cols], preferred_element_type=F32)
                o_ref[pl.ds(q0, GRID_W), cols] = (o / l).astype(o_ref.dtype)

    score_stage(0, 0)

    def body(u, carry):
        score_stage(2 * u + 1, 1)
        value_stage(2 * u, 0)
        score_stage(2 * u + 2, 0)
        value_stage(2 * u + 1, 1)
        return carry

    lax.fori_loop(0, n_steps // 2 - 1, body, 0)
    score_stage(n_steps - 1, 1)
    value_stage(n_steps - 2, 0)
    value_stage(n_steps - 1, 1)


def na_attention(qkv, bias, batch, seq, heads_per_step=4, rows_per_step=2):
    g = heads_per_step
    w = g * NA_HEAD_DIM
    groups = NA_HEADS // g
    rows = seq // GRID_W
    assert rows % rows_per_step == 0 and rows >= NA_KH
    return pl.pallas_call(
        functools.partial(_na_kernel, rows=rows, heads=g, rows_per_step=rows_per_step),
        grid=(batch, groups),
        in_specs=[pl.BlockSpec((seq, w), lambda b, j: (b, j)),
                  pl.BlockSpec((seq, w), lambda b, j: (b, groups + j)),
                  pl.BlockSpec((seq, w), lambda b, j: (b, 2 * groups + j)),
                  pl.BlockSpec((g, NA_BIAS_PAIRS, GRID_W, LANES), lambda b, j: (j, 0, 0, 0))],
        out_specs=pl.BlockSpec((seq, w), lambda b, j: (b, j)),
        out_shape=jax.ShapeDtypeStruct((batch * seq, NA_WIDTH), BF16),
        scratch_shapes=[pltpu.VMEM((2, rows_per_step * g, GRID_W, NA_KH * GRID_W), F32)],
        compiler_params=_cparams(2, VMEM_MIB),
        name="na_attention",
    )(qkv, qkv, qkv, bias)


def _mem_block_kernel(h_ref, o_ref, g_mix_ref, g_pre_ref, wq_ref, kv_ref, wo_ref, g_post_ref, g_next_ref,
                      ho_ref, uo_ref, *, scale):
    h1 = h_ref[...] + _rms(o_ref[...].astype(F32), g_mix_ref[...])
    u = _rms(h1, g_pre_ref[...]).astype(BF16)
    q = (jnp.dot(u, wq_ref[...], preferred_element_type=F32) * scale).astype(BF16)
    outs = []
    for hd in range(MEM_HEADS):
        qh = q[:, hd * MEM_HEAD_DIM:(hd + 1) * MEM_HEAD_DIM]
        k = kv_ref[:, 2 * hd * MEM_HEAD_DIM:(2 * hd + 1) * MEM_HEAD_DIM]
        v = kv_ref[:, (2 * hd + 1) * MEM_HEAD_DIM:(2 * hd + 2) * MEM_HEAD_DIM]
        s = lax.dot_general(qh, k, (((1,), (1,)), ((), ())), preferred_element_type=F32)
        p = jnp.exp2(s - jnp.max(s, axis=-1, keepdims=True))
        l = jnp.sum(p, axis=-1, keepdims=True)
        outs.append((jnp.dot(p.astype(BF16), v, preferred_element_type=F32) / l).astype(BF16))
    a = jnp.dot(jnp.concatenate(outs, axis=1), wo_ref[...], preferred_element_type=F32)
    h2 = h1 + _rms(a, g_post_ref[...])
    ho_ref[...] = h2
    uo_ref[...] = _rms(h2, g_next_ref[...]).astype(uo_ref.dtype)


def mem_block(h, o, g_mix_post, g_pre, w_q, kv, w_o, g_post, g_next, batch, seq, bq=256):
    mem_len = kv.shape[0] // batch
    d = h.shape[1]
    bq = min(bq, seq)
    nq = seq // bq
    width = MEM_HEADS * MEM_HEAD_DIM
    row = pl.BlockSpec((bq, d), lambda b, i: (b * nq + i, 0))
    vec = pl.BlockSpec((1, d), lambda b, i: (0, 0))
    once = pl.Buffered(1)
    return pl.pallas_call(
        functools.partial(_mem_block_kernel, scale=LOG2_E * float(MEM_HEAD_DIM) ** -0.5),
        grid=(batch, nq),
        in_specs=[row, row, vec, vec,
                  pl.BlockSpec((d, width), lambda b, i: (0, 0), pipeline_mode=once),
                  pl.BlockSpec((mem_len, 2 * width), lambda b, i: (b, 0)),
                  pl.BlockSpec((width, d), lambda b, i: (0, 0), pipeline_mode=once),
                  vec, vec],
        out_specs=[row, row],
        out_shape=[jax.ShapeDtypeStruct((batch * seq, d), F32), jax.ShapeDtypeStruct((batch * seq, d), BF16)],
        compiler_params=_cparams(2, VMEM_MIB),
        name="mem_block",
    )(h, o, _row(g_mix_post), _row(g_pre), w_q.astype(BF16), kv, w_o.astype(BF16), _row(g_post), _row(g_next))


def kernel(x, mem, positions, ffn1_w_gate, ffn1_w_up, ffn1_w_down, g_ffn1, w_in, g_q_a, w_q_b, g_kv_a, w_kv_b, na_rpb, w_out, g_mix, g_mem_in, w_mem_q, w_mem_kv, w_mem_o, g_mem_attn, ffn2_w_gate, ffn2_w_up, ffn2_w_down, g_ffn2, g_final):
    batch, seq, d = x.shape
    m = batch * seq
    depth = ffn1_w_gate.shape[0]
    pos = positions.reshape(m, 1).astype(jnp.int32)
    invf_q, invf_k = _inv_freq_lanes(LANES), _inv_freq_lanes(MLA_ROPE_DIM)
    mem2 = mem.reshape(-1, d)

    h = x.reshape(m, d)
    u = norm_cast(h, g_ffn1[0, 0])
    out = None
    for l in range(depth):
        f = ffn(u, ffn1_w_gate, ffn1_w_up, ffn1_w_down, l)
        h, u = resid_norm(h, f, g_ffn1[l, 1], FFN_RES_WEIGHT, g_mix[l, 0])

        qkv = mm_nt(u, jnp.swapaxes(w_in, 1, 2), l, BF16, bm=2048, bn=512, n_cols=IN_PROJ_MAIN,
                    scaled_cols=NA_WIDTH, scale=LOG2_E * float(NA_HEAD_DIM) ** -0.5,
                    single_buffer_x=True, name="in_proj")
        q_cat = q_b_proj(qkv, g_q_a[l], w_q_b[l], pos, invf_q)
        k_cat, v_mla = kv_b_proj(qkv, g_kv_a[l], w_kv_b[l], u, w_in[l, :, IN_PROJ_MAIN:], pos, invf_k)
        o_mla = mla_attention(q_cat, k_cat, v_mla, batch, seq)
        o_na = na_attention(qkv, na_bias_table(na_rpb[l]), batch, seq)
        o = mm2(o_na, o_mla, w_out, l, BF16, bm=2048, bn=512, single_buffer_x=True, name="out_proj")

        mem_n = norm_cast(mem2, g_mem_in[l])
        kv_mem = mm(mem_n, w_mem_kv, l, BF16, bm=1024, bn=512, name="mem_kv_proj")
        h, u = mem_block(h, o, g_mix[l, 1], g_mem_attn[l, 0], w_mem_q[l], kv_mem, w_mem_o[l],
                         g_mem_attn[l, 1], g_ffn2[l, 0], batch, seq)

        f = ffn(u, ffn2_w_gate, ffn2_w_up, ffn2_w_down, l)
        if l + 1 < depth:
            h = resid_final(h, f, g_ffn2[l, 1], FFN_RES_WEIGHT, g_final[l])
            u = norm_cast(h, g_ffn1[l + 1, 0])
        else:
            out = resid_final(h, f, g_ffn2[l, 1], FFN_RES_WEIGHT, g_final[l])
    return out.reshape(batch, seq, d)
```

```python
import functools

import jax
import jax.numpy as jnp
from jax import lax
from jax.experimental import pallas as pl
from jax.experimental.pallas import tpu as pltpu

F32 = jnp.float32
BF16 = jnp.bfloat16

GRID_W = 64
NA_HEADS = 16
NA_HEAD_DIM = 128
NA_KH = 8
NA_KW = 16
MLA_HEADS = 16
MLA_Q_RANK = 1024
MLA_KV_RANK = 512
MLA_NOPE_DIM = 128
MLA_ROPE_DIM = 64
MLA_V_DIM = 128
MLA_QK_PAD = 256
MLA_VT_ROWS = MLA_V_DIM + 16
LOG2_E = 1.4426950408889634
ROPE_THETA = 10000.0
MEM_HEADS = 4
MEM_HEAD_DIM = 128
FFN_RES_WEIGHT = 0.5
NORM_EPS = 1e-6
NEG_INF = -1e30
NA_WIDTH = NA_HEADS * NA_HEAD_DIM
MLA_WIDTH = MLA_HEADS * MLA_V_DIM
IN_PROJ_MAIN = 3 * NA_WIDTH + MLA_Q_RANK + MLA_KV_RANK
LANES = 128
MIB = 1024 * 1024
VMEM_SMALL_MIB = 32
VMEM_MIB = 56
VMEM_MAX_MIB = 60


def _cparams(n_axes, vmem_mib):
    return pltpu.CompilerParams(dimension_semantics=("parallel",) * n_axes,
                                vmem_limit_bytes=vmem_mib * MIB)


def _rms(x, g):
    return x * lax.rsqrt(jnp.mean(x * x, axis=-1, keepdims=True) + NORM_EPS) * g


def _row(v):
    return v.reshape(1, -1).astype(F32)


def _norm_cast_kernel(x_ref, g_ref, o_ref):
    o_ref[...] = _rms(x_ref[...], g_ref[...]).astype(o_ref.dtype)


def norm_cast(x, g, bm=512):
    m, d = x.shape
    bm = min(bm, m)
    return pl.pallas_call(
        _norm_cast_kernel,
        grid=(m // bm,),
        in_specs=[pl.BlockSpec((bm, d), lambda i: (i, 0)), pl.BlockSpec((1, d), lambda i: (0, 0))],
        out_specs=pl.BlockSpec((bm, d), lambda i: (i, 0)),
        out_shape=jax.ShapeDtypeStruct((m, d), BF16),
        compiler_params=_cparams(1, VMEM_SMALL_MIB),
        name="norm_cast",
    )(x, _row(g))


def _resid_norm_kernel(h_ref, f_ref, gp_ref, gn_ref, ho_ref, uo_ref, *, weight):
    h = h_ref[...] + weight * _rms(f_ref[...].astype(F32), gp_ref[...])
    ho_ref[...] = h
    uo_ref[...] = _rms(h, gn_ref[...]).astype(uo_ref.dtype)


def resid_norm(h, f, g_post, weight, g_next, bm=256):
    m, d = h.shape
    bm = min(bm, m)
    row = pl.BlockSpec((bm, d), lambda i: (i, 0))
    vec = pl.BlockSpec((1, d), lambda i: (0, 0))
    return pl.pallas_call(
        functools.partial(_resid_norm_kernel, weight=weight),
        grid=(m // bm,),
        in_specs=[row, row, vec, vec],
        out_specs=[row, row],
        out_shape=[jax.ShapeDtypeStruct((m, d), F32), jax.ShapeDtypeStruct((m, d), BF16)],
        compiler_params=_cparams(1, VMEM_MIB),
        name="resid_norm",
    )(h, f, _row(g_post), _row(g_next))


def _resid_final_kernel(h_ref, f_ref, gp_ref, gn_ref, o_ref, *, weight):
    h = h_ref[...] + weight * _rms(f_ref[...].astype(F32), gp_ref[...])
    o_ref[...] = _rms(h, gn_ref[...])


def resid_final(h, f, g_post, weight, g_final, bm=512):
    m, d = h.shape
    bm = min(bm, m)
    row = pl.BlockSpec((bm, d), lambda i: (i, 0))
    vec = pl.BlockSpec((1, d), lambda i: (0, 0))
    return pl.pallas_call(
        functools.partial(_resid_final_kernel, weight=weight),
        grid=(m // bm,),
        in_specs=[row, row, vec, vec],
        out_specs=row,
        out_shape=jax.ShapeDtypeStruct((m, d), F32),
        compiler_params=_cparams(1, VMEM_MIB),
        name="resid_final",
    )(h, f, _row(g_post), _row(g_final))


def _x_spec(bm, k, single_buffer):
    if single_buffer:
        return pl.BlockSpec((bm, k), lambda i, j: (i, 0), pipeline_mode=pl.Buffered(1))
    return pl.BlockSpec((bm, k), lambda i, j: (i, 0))


def _w_spec(k, bn, layer):
    return pl.BlockSpec((None, k, bn), lambda i, j: (layer, 0, j))


def _mm_kernel(x_ref, w_ref, o_ref):
    o_ref[...] = jnp.dot(x_ref[...], w_ref[...].astype(BF16),
                         preferred_element_type=F32).astype(o_ref.dtype)


def mm(x, w, layer, out_dtype, bm, bn, n_cols=None, single_buffer_x=False, vmem_mib=VMEM_MIB, name="mm"):
    m, k = x.shape
    n = w.shape[2] if n_cols is None else n_cols
    bm, bn = min(bm, m), min(bn, n)
    assert m % bm == 0 and n % bn == 0 and w.shape[1] == k
    return pl.pallas_call(
        _mm_kernel,
        grid=(m // bm, n // bn),
        in_specs=[_x_spec(bm, k, single_buffer_x), _w_spec(k, bn, layer)],
        out_specs=pl.BlockSpec((bm, bn), lambda i, j: (i, j)),
        out_shape=jax.ShapeDtypeStruct((m, n), out_dtype),
        compiler_params=_cparams(2, vmem_mib),
        name=name,
    )(x, w)


def _mm_nt_kernel(x_ref, wt_ref, o_ref, *, scaled_blocks, scale):
    acc = lax.dot_general(x_ref[...], wt_ref[...].astype(BF16), (((1,), (1,)), ((), ())),
                          preferred_element_type=F32)
    if scaled_blocks:
        acc = acc * jnp.where(pl.program_id(1) < scaled_blocks, scale, 1.0)
    o_ref[...] = acc.astype(o_ref.dtype)


def mm_nt(x, wt, layer, out_dtype, bm, bn, n_cols=None, scaled_cols=0, scale=1.0, single_buffer_x=False,
          vmem_mib=VMEM_MIB, name="mm_nt"):
    m, k = x.shape
    n = wt.shape[1] if n_cols is None else n_cols
    bm, bn = min(bm, m), min(bn, n)
    assert m % bm == 0 and n % bn == 0 and wt.shape[2] == k and scaled_cols % bn == 0
    return pl.pallas_call(
        functools.partial(_mm_nt_kernel, scaled_blocks=scaled_cols // bn, scale=scale),
        grid=(m // bm, n // bn),
        in_specs=[_x_spec(bm, k, single_buffer_x), pl.BlockSpec((None, bn, k), lambda i, j: (layer, j, 0))],
        out_specs=pl.BlockSpec((bm, bn), lambda i, j: (i, j)),
        out_shape=jax.ShapeDtypeStruct((m, n), out_dtype),
        compiler_params=_cparams(2, vmem_mib),
        name=name,
    )(x, wt)


def _mm2_kernel(xa_ref, xb_ref, w_ref, o_ref):
    ka = xa_ref.shape[1]
    w = w_ref[...].astype(BF16)
    acc = jnp.dot(xa_ref[...], w[:ka], preferred_element_type=F32)
    acc = acc + jnp.dot(xb_ref[...], w[ka:], preferred_element_type=F32)
    o_ref[...] = acc.astype(o_ref.dtype)


def mm2(xa, xb, w, layer, out_dtype, bm, bn, single_buffer_x=False, vmem_mib=VMEM_MIB, name="mm2"):
    m, ka = xa.shape
    kb = xb.shape[1]
    n = w.shape[2]
    bm, bn = min(bm, m), min(bn, n)
    assert m % bm == 0 and n % bn == 0 and w.shape[1] == ka + kb
    return pl.pallas_call(
        _mm2_kernel,
        grid=(m // bm, n // bn),
        in_specs=[_x_spec(bm, ka, single_buffer_x), _x_spec(bm, kb, single_buffer_x),
                  _w_spec(ka + kb, bn, layer)],
        out_specs=pl.BlockSpec((bm, bn), lambda i, j: (i, j)),
        out_shape=jax.ShapeDtypeStruct((m, n), out_dtype),
        compiler_params=_cparams(2, vmem_mib),
        name=name,
    )(xa, xb, w)


def _gateup_kernel(x_ref, wg_ref, wu_ref, wd_ref, o_ref, wdo_ref):
    x = x_ref[...]
    g = jnp.dot(x, wg_ref[...].astype(BF16), preferred_element_type=F32)
    u = jnp.dot(x, wu_ref[...].astype(BF16), preferred_element_type=F32)
    o_ref[...] = (g * jax.nn.sigmoid(g) * u).astype(o_ref.dtype)
    wdo_ref[...] = wd_ref[...].astype(wdo_ref.dtype)


def gateup(x, w_gate, w_up, w_down, layer, bm=2048, bf=256):
    m, k = x.shape
    f = w_gate.shape[2]
    d_out = w_down.shape[2]
    bm, bf = min(bm, m), min(bf, f)
    assert m % bm == 0 and f % bf == 0 and w_down.shape[1] == f
    n_j = f // bf
    slab = f // ((m // bm) * n_j)
    assert slab * (m // bm) * n_j == f and slab % 16 == 0
    return pl.pallas_call(
        _gateup_kernel,
        grid=(m // bm, n_j),
        in_specs=[_x_spec(bm, k, True), _w_spec(k, bf, layer), _w_spec(k, bf, layer),
                  pl.BlockSpec((None, slab, d_out), lambda i, j: (layer, i * n_j + j, 0))],
        out_specs=[pl.BlockSpec((bm, bf), lambda i, j: (i, j)),
                   pl.BlockSpec((None, slab, d_out), lambda i, j: (0, i * n_j + j, 0))],
        out_shape=[jax.ShapeDtypeStruct((m, f), BF16), jax.ShapeDtypeStruct((1, f, d_out), BF16)],
        compiler_params=_cparams(2, VMEM_MIB),
        name="ffn_gateup",
    )(x, w_gate, w_up, w_down)


def ffn(u, w_gate, w_up, w_down, layer):
    hidden, w_down_bf16 = gateup(u, w_gate, w_up, w_down, layer)
    return mm(hidden, w_down_bf16, 0, BF16, bm=512, bn=512, name="ffn_down")


def _rope_tables(pos_ref, invf_ref, width):
    ang = pos_ref[...].astype(F32) * invf_ref[...]
    lane = lax.broadcasted_iota(jnp.int32, ang.shape, 1)
    first_half = (lane & (MLA_ROPE_DIM - 1)) < MLA_ROPE_DIM // 2
    valid = lane < width
    sin = jnp.sin(ang)
    cos_t = jnp.where(valid, jnp.cos(ang), 0.0)
    sin_t = jnp.where(valid, jnp.where(first_half, -sin, sin), 0.0)
    return cos_t, sin_t, first_half


def _rope_apply(t, cos_t, sin_t, first_half):
    half = MLA_ROPE_DIM // 2
    partner = jnp.where(first_half, pltpu.roll(t, LANES - half, 1), pltpu.roll(t, half, 1))
    return t * cos_t + partner * sin_t


def _inv_freq_lanes(width):
    half = MLA_ROPE_DIM // 2
    inv_freq = 1.0 / (ROPE_THETA ** (jnp.arange(half, dtype=F32) / half))
    return jnp.concatenate([jnp.tile(inv_freq, width // half), jnp.zeros((LANES - width,), F32)]).reshape(1, LANES)


def _qb_kernel(cq_ref, g_ref, wn_ref, wp_ref, pos_ref, invf_ref, o_ref):
    cn = _rms(cq_ref[...].astype(F32), g_ref[...]).astype(BF16)
    qn = jnp.dot(cn, wn_ref[...], preferred_element_type=F32)
    qp = jnp.dot(cn, wp_ref[...], preferred_element_type=F32)
    cos_t, sin_t, first_half = _rope_tables(pos_ref, invf_ref, LANES)
    low = lax.broadcasted_iota(jnp.int32, cos_t.shape, 1) < MLA_ROPE_DIM
    for pair in range(MLA_HEADS // 2):
        r = _rope_apply(qp[:, pair * LANES:(pair + 1) * LANES], cos_t, sin_t, first_half)
        pe = (jnp.where(low, r, 0.0), jnp.where(low, pltpu.roll(r, LANES - MLA_ROPE_DIM, 1), 0.0))
        for k in range(2):
            h = 2 * pair + k
            c0 = h * MLA_QK_PAD
            o_ref[:, c0:c0 + MLA_NOPE_DIM] = qn[:, h * MLA_NOPE_DIM:(h + 1) * MLA_NOPE_DIM].astype(o_ref.dtype)
            o_ref[:, c0 + MLA_NOPE_DIM:c0 + MLA_QK_PAD] = pe[k].astype(o_ref.dtype)


def q_b_proj(qkv, g_q_a, w_q_b, pos, invf, bm=512):
    m = qkv.shape[0]
    bm = min(bm, m)
    h, dq = MLA_HEADS, MLA_NOPE_DIM + MLA_ROPE_DIM
    w = w_q_b.reshape(MLA_Q_RANK, h, dq) * (LOG2_E * float(dq) ** -0.5)
    w_nope = w[:, :, :MLA_NOPE_DIM].reshape(MLA_Q_RANK, h * MLA_NOPE_DIM).astype(BF16)
    w_pe = w[:, :, MLA_NOPE_DIM:].reshape(MLA_Q_RANK, h * MLA_ROPE_DIM).astype(BF16)
    cq_block = (3 * NA_WIDTH) // MLA_Q_RANK
    return pl.pallas_call(
        _qb_kernel,
        grid=(m // bm,),
        in_specs=[pl.BlockSpec((bm, MLA_Q_RANK), lambda i: (i, cq_block)),
                  pl.BlockSpec((1, MLA_Q_RANK), lambda i: (0, 0)),
                  pl.BlockSpec((MLA_Q_RANK, h * MLA_NOPE_DIM), lambda i: (0, 0)),
                  pl.BlockSpec((MLA_Q_RANK, h * MLA_ROPE_DIM), lambda i: (0, 0)),
                  pl.BlockSpec((bm, 1), lambda i: (i, 0)),
                  pl.BlockSpec((1, LANES), lambda i: (0, 0))],
        out_specs=pl.BlockSpec((bm, h * MLA_QK_PAD), lambda i: (i, 0)),
        out_shape=jax.ShapeDtypeStruct((m, h * MLA_QK_PAD), BF16),
        compiler_params=_cparams(1, VMEM_MIB),
        name="mla_q_proj",
    )(qkv, _row(g_q_a), w_nope, w_pe, pos, invf)


def _kvb_kernel(ckv_ref, g_ref, wk_ref, wvt_ref, u_ref, wt_ref, pos_ref, invf_ref, k_ref, vt_ref):
    cn = _rms(ckv_ref[...].astype(F32), g_ref[...]).astype(BF16)
    kn = jnp.dot(cn, wk_ref[...], preferred_element_type=F32)
    kr = jnp.dot(u_ref[...], wt_ref[...], preferred_element_type=F32)
    cos_t, sin_t, first_half = _rope_tables(pos_ref, invf_ref, MLA_ROPE_DIM)
    kpe = _rope_apply(kr, cos_t, sin_t, first_half).astype(k_ref.dtype)
    for h in range(MLA_HEADS):
        k_ref[:, h * MLA_QK_PAD:h * MLA_QK_PAD + MLA_NOPE_DIM] = (
            kn[:, h * MLA_NOPE_DIM:(h + 1) * MLA_NOPE_DIM].astype(k_ref.dtype))
        k_ref[:, h * MLA_QK_PAD + MLA_NOPE_DIM:(h + 1) * MLA_QK_PAD] = kpe
    vt = lax.dot_general(wvt_ref[...], cn, (((1,), (1,)), ((), ())), preferred_element_type=F32)
    ones = jnp.ones((MLA_VT_ROWS - MLA_V_DIM, vt.shape[1]), vt_ref.dtype)
    for h in range(MLA_HEADS):
        r0 = h * MLA_VT_ROWS
        vt_ref[r0:r0 + MLA_V_DIM, :] = vt[h * MLA_V_DIM:(h + 1) * MLA_V_DIM].astype(vt_ref.dtype)
        vt_ref[r0 + MLA_V_DIM:r0 + MLA_VT_ROWS, :] = ones


def kv_b_proj(qkv, g_kv_a, w_kv_b, u, w_rope_in, pos, invf, bm=512):
    m, d = u.shape
    bm = min(bm, m)
    h = MLA_HEADS
    ckv_block = (3 * NA_WIDTH + MLA_Q_RANK) // MLA_KV_RANK
    wt = jnp.pad(w_rope_in, ((0, 0), (0, LANES - MLA_ROPE_DIM))).astype(BF16)
    w3 = w_kv_b.reshape(MLA_KV_RANK, h, MLA_NOPE_DIM + MLA_V_DIM)
    w_k = w3[:, :, :MLA_NOPE_DIM].reshape(MLA_KV_RANK, h * MLA_NOPE_DIM).astype(BF16)
    w_vt = w3[:, :, MLA_NOPE_DIM:].reshape(MLA_KV_RANK, h * MLA_V_DIM).T.astype(BF16)
    return pl.pallas_call(
        _kvb_kernel,
        grid=(m // bm,),
        in_specs=[pl.BlockSpec((bm, MLA_KV_RANK), lambda i: (i, ckv_block)),
                  pl.BlockSpec((1, MLA_KV_RANK), lambda i: (0, 0)),
                  pl.BlockSpec((MLA_KV_RANK, h * MLA_NOPE_DIM), lambda i: (0, 0)),
                  pl.BlockSpec((h * MLA_V_DIM, MLA_KV_RANK), lambda i: (0, 0)),
                  pl.BlockSpec((bm, d), lambda i: (i, 0)),
                  pl.BlockSpec((d, LANES), lambda i: (0, 0)),
                  pl.BlockSpec((bm, 1), lambda i: (i, 0)),
                  pl.BlockSpec((1, LANES), lambda i: (0, 0))],
        out_specs=[pl.BlockSpec((bm, h * MLA_QK_PAD), lambda i: (i, 0)),
                   pl.BlockSpec((h * MLA_VT_ROWS, bm), lambda i: (0, i))],
        out_shape=[jax.ShapeDtypeStruct((m, h * MLA_QK_PAD), BF16),
                   jax.ShapeDtypeStruct((h * MLA_VT_ROWS, m), BF16)],
        compiler_params=_cparams(1, VMEM_MIB),
        name="mla_kv_proj",
    )(qkv, _row(g_kv_a), w_k, w_vt, u, wt, pos, invf)


def _mla_kernel(q_ref, qn_ref, k_ref, kn_ref, vt_ref, o_ref, s_scr, *, heads, bq):
    def score_stage(q_tile_ref, row0, keys_ref, slot):
        for h in range(heads):
            qk = slice(h * MLA_QK_PAD, (h + 1) * MLA_QK_PAD)
            s_scr[slot, h] = lax.dot_general(keys_ref[:, qk], q_tile_ref[row0:row0 + bq, qk],
                                             (((1,), (1,)), ((), ())), preferred_element_type=F32)

    def value_stage(slot, row0):
        for h in range(heads):
            s = s_scr[slot, h]
            p = jnp.exp2(s - jnp.max(s, axis=0, keepdims=True)).astype(BF16)
            ot = jnp.dot(vt_ref[h * MLA_VT_ROWS:(h + 1) * MLA_VT_ROWS, :], p,
                         preferred_element_type=F32)
            o = ot[:MLA_V_DIM] / ot[MLA_V_DIM:MLA_V_DIM + 1]
            o_ref[row0:row0 + bq, h * MLA_V_DIM:(h + 1) * MLA_V_DIM] = o.T.astype(o_ref.dtype)

    first_step = (pl.program_id(0) == 0) & (pl.program_id(1) == 0) & (pl.program_id(2) == 0)

    @pl.when(first_step)
    def _():
        score_stage(q_ref, 0, k_ref, 0)

    score_stage(q_ref, bq, k_ref, 1)
    value_stage(0, 0)
    score_stage(qn_ref, 0, kn_ref, 0)
    value_stage(1, bq)


def mla_attention(q, k, vt, batch, seq, bq=512, heads_per_step=2):
    bq = min(bq, seq // 2)
    pairs = seq // (2 * bq)
    g = heads_per_step
    groups = MLA_HEADS // g
    n_steps = batch * groups * pairs

    def next_step(b, h, i):
        flat = jnp.minimum((b * groups + h) * pairs + i + 1, n_steps - 1)
        return flat // (groups * pairs), (flat // pairs) % groups, flat % pairs

    def q_next_map(b, h, i):
        nb, nh, ni = next_step(b, h, i)
        return (nb * pairs + ni) * 2, nh

    def k_next_map(b, h, i):
        nb, nh, _ = next_step(b, h, i)
        return nb, nh

    return pl.pallas_call(
        functools.partial(_mla_kernel, heads=g, bq=bq),
        grid=(batch, groups, pairs),
        in_specs=[pl.BlockSpec((2 * bq, g * MLA_QK_PAD), lambda b, h, i: (b * pairs + i, h)),
                  pl.BlockSpec((bq, g * MLA_QK_PAD), q_next_map),
                  pl.BlockSpec((seq, g * MLA_QK_PAD), lambda b, h, i: (b, h)),
                  pl.BlockSpec((seq, g * MLA_QK_PAD), k_next_map),
                  pl.BlockSpec((g * MLA_VT_ROWS, seq), lambda b, h, i: (h, b))],
        out_specs=pl.BlockSpec((2 * bq, g * MLA_V_DIM), lambda b, h, i: (b * pairs + i, h)),
        out_shape=jax.ShapeDtypeStruct((batch * seq, MLA_WIDTH), BF16),
        scratch_shapes=[pltpu.VMEM((2, g, seq, bq), F32)],
        compiler_params=pltpu.CompilerParams(dimension_semantics=("arbitrary",) * 3,
                                             vmem_limit_bytes=VMEM_MIB * MIB),
        name="mla_attention",
    )(q, q, k, k, vt)


NA_BIAS_ROWS = 2 * NA_KH - 1
NA_BIAS_COLS = 2 * NA_KW - 1
NA_BIAS_PAIRS = NA_BIAS_ROWS - 1


def _na_bias_kernel(rpb_ref, o_ref):
    row = lax.broadcasted_iota(jnp.int32, (GRID_W, LANES), 0)
    lane = lax.broadcasted_iota(jnp.int32, (GRID_W, LANES), 1)
    kc = lane & (GRID_W - 1)
    left = lane < GRID_W
    cstart = jnp.clip(row - NA_KW // 2, 0, GRID_W - NA_KW)
    in_window = (kc >= cstart) & (kc < cstart + NA_KW)
    toeplitz = []
    for d in range(NA_BIAS_ROWS):
        vec = jnp.broadcast_to(rpb_ref[d:d + 1, :], (GRID_W, LANES))
        toeplitz.append(pltpu.roll(vec, LANES - (NA_KW - 1), 1, stride=1, stride_axis=0))
    for d in range(NA_BIAS_PAIRS):
        tile = jnp.where(left, toeplitz[d], pltpu.roll(toeplitz[d + 1], GRID_W, 1))
        o_ref[0, d] = jnp.where(in_window, tile * LOG2_E, NEG_INF)


def na_bias_table(rpb):
    rpb_lanes = jnp.pad(rpb.astype(F32), ((0, 0), (0, 0), (0, LANES - NA_BIAS_COLS)))
    return pl.pallas_call(
        _na_bias_kernel,
        grid=(NA_HEADS,),
        in_specs=[pl.BlockSpec((None, NA_BIAS_ROWS, LANES), lambda h: (h, 0, 0))],
        out_specs=pl.BlockSpec((1, NA_BIAS_PAIRS, GRID_W, LANES), lambda h: (h, 0, 0, 0)),
        out_shape=jax.ShapeDtypeStruct((NA_HEADS, NA_BIAS_PAIRS, GRID_W, LANES), F32),
        compiler_params=_cparams(1, VMEM_SMALL_MIB),
        name="na_bias_table",
    )(rpb_lanes)


def _na_kernel(q_ref, k_ref, v_ref, bias_ref, o_ref, s_scr, *, rows, heads, rows_per_step):
    nk = NA_KH * GRID_W
    n_steps = rows // rows_per_step

    def row_geometry(step, rr):
        r = step * rows_per_step + rr
        rs = jnp.clip(r - NA_KH // 2, 0, rows - NA_KH)
        d0 = rs - r + (NA_KH - 1)
        return pl.multiple_of(r * GRID_W, GRID_W), pl.multiple_of(rs * GRID_W, GRID_W), d0

    def score_stage(step, slot):
        for rr in range(rows_per_step):
            q0, k0, d0 = row_geometry(step, rr)
            for h in range(heads):
                cols = slice(h * NA_HEAD_DIM, (h + 1) * NA_HEAD_DIM)
                q = q_ref[pl.ds(q0, GRID_W), cols]
                k = k_ref[pl.ds(k0, nk), cols]
                s = lax.dot_general(q, k, (((1,), (1,)), ((), ())), preferred_element_type=F32)
                bias = jnp.concatenate([bias_ref[h, d0 + 2 * p] for p in range(NA_KH // 2)], axis=1)
                s_scr[slot, rr * heads + h] = s + bias

    def value_stage(step, slot):
        for rr in range(rows_per_step):
            q0, k0, _ = row_geometry(step, rr)
            for h in range(heads):
                cols = slice(h * NA_HEAD_DIM, (h + 1) * NA_HEAD_DIM)
                s = s_scr[slot, rr * heads + h]
                p = jnp.exp2(s - jnp.max(s, axis=-1, keepdims=True))
                l = jnp.sum(p, axis=-1, keepdims=True)
                o = jnp.dot(p.astype(BF16), v_ref[pl.ds(k0, nk), cols], preferred_element_type=F32)
                o_ref[pl.ds(q0, GRID_W), cols] = (o / l).astype(o_ref.dtype)

    score_stage(0, 0)

    def body(u, carry):
        score_stage(2 * u + 1, 1)
        value_stage(2 * u, 0)
        score_stage(2 * u + 2, 0)
        value_stage(2 * u + 1, 1)
        return carry

    lax.fori_loop(0, n_steps // 2 - 1, body, 0)
    score_stage(n_steps - 1, 1)
    value_stage(n_steps - 2, 0)
    value_stage(n_steps - 1, 1)


def na_attention(qkv, bias, batch, seq, heads_per_step=4, rows_per_step=1):
    g = heads_per_step
    w = g * NA_HEAD_DIM
    groups = NA_HEADS // g
    rows = seq // GRID_W
    assert rows % rows_per_step == 0 and rows >= NA_KH
    return pl.pallas_call(
        functools.partial(_na_kernel, rows=rows, heads=g, rows_per_step=rows_per_step),
        grid=(batch, groups),
        in_specs=[pl.BlockSpec((seq, w), lambda b, j: (b, j)),
                  pl.BlockSpec((seq, w), lambda b, j: (b, groups + j)),
                  pl.BlockSpec((seq, w), lambda b, j: (b, 2 * groups + j)),
                  pl.BlockSpec((g, NA_BIAS_PAIRS, GRID_W, LANES), lambda b, j: (j, 0, 0, 0))],
        out_specs=pl.BlockSpec((seq, w), lambda b, j: (b, j)),
        out_shape=jax.ShapeDtypeStruct((batch * seq, NA_WIDTH), BF16),
        scratch_shapes=[pltpu.VMEM((2, rows_per_step * g, GRID_W, NA_KH * GRID_W), F32)],
        compiler_params=_cparams(2, VMEM_MIB),
        name="na_attention",
    )(qkv, qkv, qkv, bias)


def _mem_block_kernel(h_ref, o_ref, g_mix_ref, g_pre_ref, wq_ref, kv_ref, wo_ref, g_post_ref, g_next_ref,
                      ho_ref, uo_ref, *, scale):
    h1 = h_ref[...] + _rms(o_ref[...].astype(F32), g_mix_ref[...])
    u = _rms(h1, g_pre_ref[...]).astype(BF16)
    q = (jnp.dot(u, wq_ref[...], preferred_element_type=F32) * scale).astype(BF16)
    outs = []
    for hd in range(MEM_HEADS):
        qh = q[:, hd * MEM_HEAD_DIM:(hd + 1) * MEM_HEAD_DIM]
        k = kv_ref[:, 2 * hd * MEM_HEAD_DIM:(2 * hd + 1) * MEM_HEAD_DIM]
        v = kv_ref[:, (2 * hd + 1) * MEM_HEAD_DIM:(2 * hd + 2) * MEM_HEAD_DIM]
        s = lax.dot_general(qh, k, (((1,), (1,)), ((), ())), preferred_element_type=F32)
        p = jnp.exp2(s - jnp.max(s, axis=-1, keepdims=True))
        l = jnp.sum(p, axis=-1, keepdims=True)
        outs.append((jnp.dot(p.astype(BF16), v, preferred_element_type=F32) / l).astype(BF16))
    a = jnp.dot(jnp.concatenate(outs, axis=1), wo_ref[...], preferred_element_type=F32)
    h2 = h1 + _rms(a, g_post_ref[...])
    ho_ref[...] = h2
    uo_ref[...] = _rms(h2, g_next_ref[...]).astype(uo_ref.dtype)


def mem_block(h, o, g_mix_post, g_pre, w_q, kv, w_o, g_post, g_next, batch, seq, bq=256):
    mem_len = kv.shape[0] // batch
    d = h.shape[1]
    bq = min(bq, seq)
    nq = seq // bq
    width = MEM_HEADS * MEM_HEAD_DIM
    row = pl.BlockSpec((bq, d), lambda b, i: (b * nq + i, 0))
    vec = pl.BlockSpec((1, d), lambda b, i: (0, 0))
    once = pl.Buffered(1)
    return pl.pallas_call(
        functools.partial(_mem_block_kernel, scale=LOG2_E * float(MEM_HEAD_DIM) ** -0.5),
        grid=(batch, nq),
        in_specs=[row, row, vec, vec,
                  pl.BlockSpec((d, width), lambda b, i: (0, 0), pipeline_mode=once),
                  pl.BlockSpec((mem_len, 2 * width), lambda b, i: (b, 0)),
                  pl.BlockSpec((width, d), lambda b, i: (0, 0), pipeline_mode=once),
                  vec, vec],
        out_specs=[row, row],
        out_shape=[jax.ShapeDtypeStruct((batch * seq, d), F32), jax.ShapeDtypeStruct((batch * seq, d), BF16)],
        compiler_params=_cparams(2, VMEM_MIB),
        name="mem_block",
    )(h, o, _row(g_mix_post), _row(g_pre), w_q.astype(BF16), kv, w_o.astype(BF16), _row(g_post), _row(g_next))


def kernel(x, mem, positions, ffn1_w_gate, ffn1_w_up, ffn1_w_down, g_ffn1, w_in, g_q_a, w_q_b, g_kv_a, w_kv_b, na_rpb, w_out, g_mix, g_mem_in, w_mem_q, w_mem_kv, w_mem_o, g_mem_attn, ffn2_w_gate, ffn2_w_up, ffn2_w_down, g_ffn2, g_final):
    batch, seq, d = x.shape
    m = batch * seq
    depth = ffn1_w_gate.shape[0]
    pos = positions.reshape(m, 1).astype(jnp.int32)
    invf_q, invf_k = _inv_freq_lanes(LANES), _inv_freq_lanes(MLA_ROPE_DIM)
    mem2 = mem.reshape(-1, d)

    h = x.reshape(m, d)
    u = norm_cast(h, g_ffn1[0, 0])
    out = None
    for l in range(depth):
        f = ffn(u, ffn1_w_gate, ffn1_w_up, ffn1_w_down, l)
        h, u = resid_norm(h, f, g_ffn1[l, 1], FFN_RES_WEIGHT, g_mix[l, 0])

        qkv = mm_nt(u, jnp.swapaxes(w_in, 1, 2), l, BF16, bm=2048, bn=512, n_cols=IN_PROJ_MAIN,
                    scaled_cols=NA_WIDTH, scale=LOG2_E * float(NA_HEAD_DIM) ** -0.5,
                    single_buffer_x=True, name="in_proj")
        q_cat = q_b_proj(qkv, g_q_a[l], w_q_b[l], pos, invf_q)
        k_cat, v_mla = kv_b_proj(qkv, g_kv_a[l], w_kv_b[l], u, w_in[l, :, IN_PROJ_MAIN:], pos, invf_k)
        o_mla = mla_attention(q_cat, k_cat, v_mla, batch, seq)
        o_na = na_attention(qkv, na_bias_table(na_rpb[l]), batch, seq)
        o = mm2(o_na, o_mla, w_out, l, BF16, bm=2048, bn=512, single_buffer_x=True, name="out_proj")

        mem_n = norm_cast(mem2, g_mem_in[l])
        kv_mem = mm(mem_n, w_mem_kv, l, BF16, bm=1024, bn=512, name="mem_kv_proj")
        h, u = mem_block(h, o, g_mix[l, 1], g_mem_attn[l, 0], w_mem_q[l], kv_mem, w_mem_o[l],
                         g_mem_attn[l, 1], g_ffn2[l, 0], batch, seq)

        f = ffn(u, ffn2_w_gate, ffn2_w_up, ffn2_w_down, l)
        if l + 1 < depth:
            h = resid_final(h, f, g_ffn2[l, 1], FFN_RES_WEIGHT, g_final[l])
            u = norm_cast(h, g_ffn1[l + 1, 0])
        else:
            out = resid_final(h, f, g_ffn2[l, 1], FFN_RES_WEIGHT, g_final[l])
    return out.reshape(batch, seq, d)
```

```python
import functools

import jax
import jax.numpy as jnp
from jax import lax
from jax.experimental import pallas as pl
from jax.experimental.pallas import tpu as pltpu

F32 = jnp.float32
BF16 = jnp.bfloat16

GRID_W = 64
NA_HEADS = 16
NA_HEAD_DIM = 128
NA_KH = 8
NA_KW = 16
MLA_HEADS = 16
MLA_Q_RANK = 1024
MLA_KV_RANK = 512
MLA_NOPE_DIM = 128
MLA_ROPE_DIM = 64
MLA_V_DIM = 128
MLA_QK_PAD = 256
MLA_VT_ROWS = MLA_V_DIM + 16
LOG2_E = 1.4426950408889634
ROPE_THETA = 10000.0
MEM_HEADS = 4
MEM_HEAD_DIM = 128
FFN_RES_WEIGHT = 0.5
NORM_EPS = 1e-6
NEG_INF = -1e30
NA_WIDTH = NA_HEADS * NA_HEAD_DIM
MLA_WIDTH = MLA_HEADS * MLA_V_DIM
IN_PROJ_MAIN = 3 * NA_WIDTH + MLA_Q_RANK + MLA_KV_RANK
LANES = 128
MIB = 1024 * 1024
VMEM_SMALL_MIB = 32
VMEM_MIB = 56
VMEM_MAX_MIB = 60


def _cparams(n_axes, vmem_mib):
    return pltpu.CompilerParams(dimension_semantics=("parallel",) * n_axes,
                                vmem_limit_bytes=vmem_mib * MIB)


def _rms(x, g):
    return x * lax.rsqrt(jnp.mean(x * x, axis=-1, keepdims=True) + NORM_EPS) * g


def _row(v):
    return v.reshape(1, -1).astype(F32)


def _norm_cast_kernel(x_ref, g_ref, o_ref):
    o_ref[...] = _rms(x_ref[...], g_ref[...]).astype(o_ref.dtype)


def norm_cast(x, g, bm=512):
    m, d = x.shape
    bm = min(bm, m)
    return pl.pallas_call(
        _norm_cast_kernel,
        grid=(m // bm,),
        in_specs=[pl.BlockSpec((bm, d), lambda i: (i, 0)), pl.BlockSpec((1, d), lambda i: (0, 0))],
        out_specs=pl.BlockSpec((bm, d), lambda i: (i, 0)),
        out_shape=jax.ShapeDtypeStruct((m, d), BF16),
        compiler_params=_cparams(1, VMEM_SMALL_MIB),
        name="norm_cast",
    )(x, _row(g))


def _resid_norm_kernel(h_ref, f_ref, gp_ref, gn_ref, ho_ref, uo_ref, *, weight):
    h = h_ref[...] + weight * _rms(f_ref[...].astype(F32), gp_ref[...])
    ho_ref[...] = h
    uo_ref[...] = _rms(h, gn_ref[...]).astype(uo_ref.dtype)


def resid_norm(h, f, g_post, weight, g_next, bm=512):
    m, d = h.shape
    bm = min(bm, m)
    row = pl.BlockSpec((bm, d), lambda i: (i, 0))
    vec = pl.BlockSpec((1, d), lambda i: (0, 0))
    return pl.pallas_call(
        functools.partial(_resid_norm_kernel, weight=weight),
        grid=(m // bm,),
        in_specs=[row, row, vec, vec],
        out_specs=[row, row],
        out_shape=[jax.ShapeDtypeStruct((m, d), F32), jax.ShapeDtypeStruct((m, d), BF16)],
        compiler_params=_cparams(1, VMEM_MAX_MIB),
        name="resid_norm",
    )(h, f, _row(g_post), _row(g_next))


def _resid_final_kernel(h_ref, f_ref, gp_ref, gn_ref, o_ref, *, weight):
    h = h_ref[...] + weight * _rms(f_ref[...].astype(F32), gp_ref[...])
    o_ref[...] = _rms(h, gn_ref[...])


def resid_final(h, f, g_post, weight, g_final, bm=512):
    m, d = h.shape
    bm = min(bm, m)
    row = pl.BlockSpec((bm, d), lambda i: (i, 0))
    vec = pl.BlockSpec((1, d), lambda i: (0, 0))
    return pl.pallas_call(
        functools.partial(_resid_final_kernel, weight=weight),
        grid=(m // bm,),
        in_specs=[row, row, vec, vec],
        out_specs=row,
        out_shape=jax.ShapeDtypeStruct((m, d), F32),
        compiler_params=_cparams(1, VMEM_MIB),
        name="resid_final",
    )(h, f, _row(g_post), _row(g_final))


def _x_spec(bm, k, single_buffer):
    if single_buffer:
        return pl.BlockSpec((bm, k), lambda i, j: (i, 0), pipeline_mode=pl.Buffered(1))
    return pl.BlockSpec((bm, k), lambda i, j: (i, 0))


def _w_spec(k, bn, layer):
    return pl.BlockSpec((None, k, bn), lambda i, j: (layer, 0, j))


def _mm_kernel(x_ref, w_ref, o_ref):
    o_ref[...] = jnp.dot(x_ref[...], w_ref[...].astype(BF16),
                         preferred_element_type=F32).astype(o_ref.dtype)


def mm(x, w, layer, out_dtype, bm, bn, n_cols=None, single_buffer_x=False, vmem_mib=VMEM_MIB, name="mm"):
    m, k = x.shape
    n = w.shape[2] if n_cols is None else n_cols
    bm, bn = min(bm, m), min(bn, n)
    assert m % bm == 0 and n % bn == 0 and w.shape[1] == k
    return pl.pallas_call(
        _mm_kernel,
        grid=(m // bm, n // bn),
        in_specs=[_x_spec(bm, k, single_buffer_x), _w_spec(k, bn, layer)],
        out_specs=pl.BlockSpec((bm, bn), lambda i, j: (i, j)),
        out_shape=jax.ShapeDtypeStruct((m, n), out_dtype),
        compiler_params=_cparams(2, vmem_mib),
        name=name,
    )(x, w)


def _mm_nt_kernel(x_ref, wt_ref, o_ref, *, scaled_blocks, scale):
    acc = lax.dot_general(x_ref[...], wt_ref[...].astype(BF16), (((1,), (1,)), ((), ())),
                          preferred_element_type=F32)
    if scaled_blocks:
        acc = acc * jnp.where(pl.program_id(1) < scaled_blocks, scale, 1.0)
    o_ref[...] = acc.astype(o_ref.dtype)


def mm_nt(x, wt, layer, out_dtype, bm, bn, n_cols=None, scaled_cols=0, scale=1.0, single_buffer_x=False,
          vmem_mib=VMEM_MIB, name="mm_nt"):
    m, k = x.shape
    n = wt.shape[1] if n_cols is None else n_cols
    bm, bn = min(bm, m), min(bn, n)
    assert m % bm == 0 and n % bn == 0 and wt.shape[2] == k and scaled_cols % bn == 0
    return pl.pallas_call(
        functools.partial(_mm_nt_kernel, scaled_blocks=scaled_cols // bn, scale=scale),
        grid=(m // bm, n // bn),
        in_specs=[_x_spec(bm, k, single_buffer_x), pl.BlockSpec((None, bn, k), lambda i, j: (layer, j, 0))],
        out_specs=pl.BlockSpec((bm, bn), lambda i, j: (i, j)),
        out_shape=jax.ShapeDtypeStruct((m, n), out_dtype),
        compiler_params=_cparams(2, vmem_mib),
        name=name,
    )(x, wt)


def _mm2_kernel(xa_ref, xb_ref, w_ref, o_ref):
    ka = xa_ref.shape[1]
    w = w_ref[...].astype(BF16)
    acc = jnp.dot(xa_ref[...], w[:ka], preferred_element_type=F32)
    acc = acc + jnp.dot(xb_ref[...], w[ka:], preferred_element_type=F32)
    o_ref[...] = acc.astype(o_ref.dtype)


def mm2(xa, xb, w, layer, out_dtype, bm, bn, single_buffer_x=False, vmem_mib=VMEM_MIB, name="mm2"):
    m, ka = xa.shape
    kb = xb.shape[1]
    n = w.shape[2]
    bm, bn = min(bm, m), min(bn, n)
    assert m % bm == 0 and n % bn == 0 and w.shape[1] == ka + kb
    return pl.pallas_call(
        _mm2_kernel,
        grid=(m // bm, n // bn),
        in_specs=[_x_spec(bm, ka, single_buffer_x), _x_spec(bm, kb, single_buffer_x),
                  _w_spec(ka + kb, bn, layer)],
        out_specs=pl.BlockSpec((bm, bn), lambda i, j: (i, j)),
        out_shape=jax.ShapeDtypeStruct((m, n), out_dtype),
        compiler_params=_cparams(2, vmem_mib),
        name=name,
    )(xa, xb, w)


def _gateup_kernel(x_ref, wg_ref, wu_ref, wd_ref, o_ref, wdo_ref):
    x = x_ref[...]
    g = jnp.dot(x, wg_ref[...].astype(BF16), preferred_element_type=F32)
    u = jnp.dot(x, wu_ref[...].astype(BF16), preferred_element_type=F32)
    o_ref[...] = (g * jax.nn.sigmoid(g) * u).astype(o_ref.dtype)
    wdo_ref[...] = wd_ref[...].astype(wdo_ref.dtype)


def gateup(x, w_gate, w_up, w_down, layer, bm=2048, bf=256):
    m, k = x.shape
    f = w_gate.shape[2]
    d_out = w_down.shape[2]
    bm, bf = min(bm, m), min(bf, f)
    assert m % bm == 0 and f % bf == 0 and w_down.shape[1] == f
    n_j = f // bf
    slab = f // ((m // bm) * n_j)
    assert slab * (m // bm) * n_j == f and slab % 16 == 0
    return pl.pallas_call(
        _gateup_kernel,
        grid=(m // bm, n_j),
        in_specs=[_x_spec(bm, k, True), _w_spec(k, bf, layer), _w_spec(k, bf, layer),
                  pl.BlockSpec((None, slab, d_out), lambda i, j: (layer, i * n_j + j, 0))],
        out_specs=[pl.BlockSpec((bm, bf), lambda i, j: (i, j)),
                   pl.BlockSpec((None, slab, d_out), lambda i, j: (0, i * n_j + j, 0))],
        out_shape=[jax.ShapeDtypeStruct((m, f), BF16), jax.ShapeDtypeStruct((1, f, d_out), BF16)],
        compiler_params=_cparams(2, VMEM_MIB),
        name="ffn_gateup",
    )(x, w_gate, w_up, w_down)


def ffn(u, w_gate, w_up, w_down, layer):
    hidden, w_down_bf16 = gateup(u, w_gate, w_up, w_down, layer)
    return mm(hidden, w_down_bf16, 0, BF16, bm=512, bn=512, name="ffn_down")


def _rope_tables(pos_ref, invf_ref, width):
    ang = pos_ref[...].astype(F32) * invf_ref[...]
    lane = lax.broadcasted_iota(jnp.int32, ang.shape, 1)
    first_half = (lane & (MLA_ROPE_DIM - 1)) < MLA_ROPE_DIM // 2
    valid = lane < width
    sin = jnp.sin(ang)
    cos_t = jnp.where(valid, jnp.cos(ang), 0.0)
    sin_t = jnp.where(valid, jnp.where(first_half, -sin, sin), 0.0)
    return cos_t, sin_t, first_half


def _rope_apply(t, cos_t, sin_t, first_half):
    half = MLA_ROPE_DIM // 2
    partner = jnp.where(first_half, pltpu.roll(t, LANES - half, 1), pltpu.roll(t, half, 1))
    return t * cos_t + partner * sin_t


def _inv_freq_lanes(width):
    half = MLA_ROPE_DIM // 2
    inv_freq = 1.0 / (ROPE_THETA ** (jnp.arange(half, dtype=F32) / half))
    return jnp.concatenate([jnp.tile(inv_freq, width // half), jnp.zeros((LANES - width,), F32)]).reshape(1, LANES)


def _qb_kernel(cq_ref, g_ref, wn_ref, wp_ref, pos_ref, invf_ref, o_ref):
    cn = _rms(cq_ref[...].astype(F32), g_ref[...]).astype(BF16)
    qn = jnp.dot(cn, wn_ref[...], preferred_element_type=F32)
    qp = jnp.dot(cn, wp_ref[...], preferred_element_type=F32)
    cos_t, sin_t, first_half = _rope_tables(pos_ref, invf_ref, LANES)
    low = lax.broadcasted_iota(jnp.int32, cos_t.shape, 1) < MLA_ROPE_DIM
    for pair in range(MLA_HEADS // 2):
        r = _rope_apply(qp[:, pair * LANES:(pair + 1) * LANES], cos_t, sin_t, first_half)
        pe = (jnp.where(low, r, 0.0), jnp.where(low, pltpu.roll(r, LANES - MLA_ROPE_DIM, 1), 0.0))
        for k in range(2):
            h = 2 * pair + k
            c0 = h * MLA_QK_PAD
            o_ref[:, c0:c0 + MLA_NOPE_DIM] = qn[:, h * MLA_NOPE_DIM:(h + 1) * MLA_NOPE_DIM].astype(o_ref.dtype)
            o_ref[:, c0 + MLA_NOPE_DIM:c0 + MLA_QK_PAD] = pe[k].astype(o_ref.dtype)


def q_b_proj(qkv, g_q_a, w_q_b, pos, invf, bm=512):
    m = qkv.shape[0]
    bm = min(bm, m)
    h, dq = MLA_HEADS, MLA_NOPE_DIM + MLA_ROPE_DIM
    w = w_q_b.reshape(MLA_Q_RANK, h, dq) * (LOG2_E * float(dq) ** -0.5)
    w_nope = w[:, :, :MLA_NOPE_DIM].reshape(MLA_Q_RANK, h * MLA_NOPE_DIM).astype(BF16)
    w_pe = w[:, :, MLA_NOPE_DIM:].reshape(MLA_Q_RANK, h * MLA_ROPE_DIM).astype(BF16)
    cq_block = (3 * NA_WIDTH) // MLA_Q_RANK
    return pl.pallas_call(
        _qb_kernel,
        grid=(m // bm,),
        in_specs=[pl.BlockSpec((bm, MLA_Q_RANK), lambda i: (i, cq_block)),
                  pl.BlockSpec((1, MLA_Q_RANK), lambda i: (0, 0)),
                  pl.BlockSpec((MLA_Q_RANK, h * MLA_NOPE_DIM), lambda i: (0, 0)),
                  pl.BlockSpec((MLA_Q_RANK, h * MLA_ROPE_DIM), lambda i: (0, 0)),
                  pl.BlockSpec((bm, 1), lambda i: (i, 0)),
                  pl.BlockSpec((1, LANES), lambda i: (0, 0))],
        out_specs=pl.BlockSpec((bm, h * MLA_QK_PAD), lambda i: (i, 0)),
        out_shape=jax.ShapeDtypeStruct((m, h * MLA_QK_PAD), BF16),
        compiler_params=_cparams(1, VMEM_MIB),
        name="mla_q_proj",
    )(qkv, _row(g_q_a), w_nope, w_pe, pos, invf)


def _kvb_kernel(ckv_ref, g_ref, wk_ref, wvt_ref, u_ref, wt_ref, pos_ref, invf_ref, k_ref, vt_ref):
    cn = _rms(ckv_ref[...].astype(F32), g_ref[...]).astype(BF16)
    kn = jnp.dot(cn, wk_ref[...], preferred_element_type=F32)
    kr = jnp.dot(u_ref[...], wt_ref[...], preferred_element_type=F32)
    cos_t, sin_t, first_half = _rope_tables(pos_ref, invf_ref, MLA_ROPE_DIM)
    kpe = _rope_apply(kr, cos_t, sin_t, first_half).astype(k_ref.dtype)
    for h in range(MLA_HEADS):
        k_ref[:, h * MLA_QK_PAD:h * MLA_QK_PAD + MLA_NOPE_DIM] = (
            kn[:, h * MLA_NOPE_DIM:(h + 1) * MLA_NOPE_DIM].astype(k_ref.dtype))
        k_ref[:, h * MLA_QK_PAD + MLA_NOPE_DIM:(h + 1) * MLA_QK_PAD] = kpe
    vt = lax.dot_general(wvt_ref[...], cn, (((1,), (1,)), ((), ())), preferred_element_type=F32)
    ones = jnp.ones((MLA_VT_ROWS - MLA_V_DIM, vt.shape[1]), vt_ref.dtype)
    for h in range(MLA_HEADS):
        r0 = h * MLA_VT_ROWS
        vt_ref[r0:r0 + MLA_V_DIM, :] = vt[h * MLA_V_DIM:(h + 1) * MLA_V_DIM].astype(vt_ref.dtype)
        vt_ref[r0 + MLA_V_DIM:r0 + MLA_VT_ROWS, :] = ones


def kv_b_proj(qkv, g_kv_a, w_kv_b, u, w_rope_in, pos, invf, bm=512):
    m, d = u.shape
    bm = min(bm, m)
    h = MLA_HEADS
    ckv_block = (3 * NA_WIDTH + MLA_Q_RANK) // MLA_KV_RANK
    wt = jnp.pad(w_rope_in, ((0, 0), (0, LANES - MLA_ROPE_DIM))).astype(BF16)
    w3 = w_kv_b.reshape(MLA_KV_RANK, h, MLA_NOPE_DIM + MLA_V_DIM)
    w_k = w3[:, :, :MLA_NOPE_DIM].reshape(MLA_KV_RANK, h * MLA_NOPE_DIM).astype(BF16)
    w_vt = w3[:, :, MLA_NOPE_DIM:].reshape(MLA_KV_RANK, h * MLA_V_DIM).T.astype(BF16)
    return pl.pallas_call(
        _kvb_kernel,
        grid=(m // bm,),
        in_specs=[pl.BlockSpec((bm, MLA_KV_RANK), lambda i: (i, ckv_block)),
                  pl.BlockSpec((1, MLA_KV_RANK), lambda i: (0, 0)),
                  pl.BlockSpec((MLA_KV_RANK, h * MLA_NOPE_DIM), lambda i: (0, 0)),
                  pl.BlockSpec((h * MLA_V_DIM, MLA_KV_RANK), lambda i: (0, 0)),
                  pl.BlockSpec((bm, d), lambda i: (i, 0)),
                  pl.BlockSpec((d, LANES), lambda i: (0, 0)),
                  pl.BlockSpec((bm, 1), lambda i: (i, 0)),
                  pl.BlockSpec((1, LANES), lambda i: (0, 0))],
        out_specs=[pl.BlockSpec((bm, h * MLA_QK_PAD), lambda i: (i, 0)),
                   pl.BlockSpec((h * MLA_VT_ROWS, bm), lambda i: (0, i))],
        out_shape=[jax.ShapeDtypeStruct((m, h * MLA_QK_PAD), BF16),
                   jax.ShapeDtypeStruct((h * MLA_VT_ROWS, m), BF16)],
        compiler_params=_cparams(1, VMEM_MIB),
        name="mla_kv_proj",
    )(qkv, _row(g_kv_a), w_k, w_vt, u, wt, pos, invf)


def _mla_kernel(q_ref, qn_ref, k_ref, kn_ref, vt_ref, o_ref, s_scr, *, heads, bq):
    def scores(q_tile_ref, row0, keys_ref, slot, h):
        qk = slice(h * MLA_QK_PAD, (h + 1) * MLA_QK_PAD)
        s_scr[slot, h] = lax.dot_general(keys_ref[:, qk], q_tile_ref[row0:row0 + bq, qk],
                                         (((1,), (1,)), ((), ())), preferred_element_type=F32)

    def values(slot, row0, h):
        s = s_scr[slot, h]
        p = jnp.exp2(s - jnp.max(s, axis=0, keepdims=True)).astype(BF16)
        ot = jnp.dot(vt_ref[h * MLA_VT_ROWS:(h + 1) * MLA_VT_ROWS, :], p,
                     preferred_element_type=F32)
        o = ot[:MLA_V_DIM] / ot[MLA_V_DIM:MLA_V_DIM + 1]
        o_ref[row0:row0 + bq, h * MLA_V_DIM:(h + 1) * MLA_V_DIM] = o.T.astype(o_ref.dtype)

    first_step = (pl.program_id(0) == 0) & (pl.program_id(1) == 0) & (pl.program_id(2) == 0)

    @pl.when(first_step)
    def _():
        for h in range(heads):
            scores(q_ref, 0, k_ref, 0, h)

    for h in range(heads):
        scores(q_ref, bq, k_ref, 1, h)
        values(0, 0, h)
    for h in range(heads):
        scores(qn_ref, 0, kn_ref, 0, h)
        values(1, bq, h)


def mla_attention(q, k, vt, batch, seq, bq=512, heads_per_step=2):
    bq = min(bq, seq // 2)
    pairs = seq // (2 * bq)
    g = heads_per_step
    groups = MLA_HEADS // g
    n_steps = batch * groups * pairs

    def next_step(b, h, i):
        flat = jnp.minimum((b * groups + h) * pairs + i + 1, n_steps - 1)
        return flat // (groups * pairs), (flat // pairs) % groups, flat % pairs

    def q_next_map(b, h, i):
        nb, nh, ni = next_step(b, h, i)
        return (nb * pairs + ni) * 2, nh

    def k_next_map(b, h, i):
        nb, nh, _ = next_step(b, h, i)
        return nb, nh

    return pl.pallas_call(
        functools.partial(_mla_kernel, heads=g, bq=bq),
        grid=(batch, groups, pairs),
        in_specs=[pl.BlockSpec((2 * bq, g * MLA_QK_PAD), lambda b, h, i: (b * pairs + i, h)),
                  pl.BlockSpec((bq, g * MLA_QK_PAD), q_next_map),
                  pl.BlockSpec((seq, g * MLA_QK_PAD), lambda b, h, i: (b, h)),
                  pl.BlockSpec((seq, g * MLA_QK_PAD), k_next_map),
                  pl.BlockSpec((g * MLA_VT_ROWS, seq), lambda b, h, i: (h, b))],
        out_specs=pl.BlockSpec((2 * bq, g * MLA_V_DIM), lambda b, h, i: (b * pairs + i, h)),
        out_shape=jax.ShapeDtypeStruct((batch * seq, MLA_WIDTH), BF16),
        scratch_shapes=[pltpu.VMEM((2, g, seq, bq), F32)],
        compiler_params=pltpu.CompilerParams(dimension_semantics=("arbitrary",) * 3,
                                             vmem_limit_bytes=VMEM_MIB * MIB),
        name="mla_attention",
    )(q, q, k, k, vt)


NA_BIAS_ROWS = 2 * NA_KH - 1
NA_BIAS_COLS = 2 * NA_KW - 1
NA_BIAS_PAIRS = NA_BIAS_ROWS - 1


def _na_bias_kernel(rpb_ref, o_ref):
    row = lax.broadcasted_iota(jnp.int32, (GRID_W, LANES), 0)
    lane = lax.broadcasted_iota(jnp.int32, (GRID_W, LANES), 1)
    kc = lane & (GRID_W - 1)
    left = lane < GRID_W
    cstart = jnp.clip(row - NA_KW // 2, 0, GRID_W - NA_KW)
    in_window = (kc >= cstart) & (kc < cstart + NA_KW)
    toeplitz = []
    for d in range(NA_BIAS_ROWS):
        vec = jnp.broadcast_to(rpb_ref[d:d + 1, :], (GRID_W, LANES))
        toeplitz.append(pltpu.roll(vec, LANES - (NA_KW - 1), 1, stride=1, stride_axis=0))
    for d in range(NA_BIAS_PAIRS):
        tile = jnp.where(left, toeplitz[d], pltpu.roll(toeplitz[d + 1], GRID_W, 1))
        o_ref[0, d] = jnp.where(in_window, tile * LOG2_E, NEG_INF)


def na_bias_table(rpb):
    rpb_lanes = jnp.pad(rpb.astype(F32), ((0, 0), (0, 0), (0, LANES - NA_BIAS_COLS)))
    return pl.pallas_call(
        _na_bias_kernel,
        grid=(NA_HEADS,),
        in_specs=[pl.BlockSpec((None, NA_BIAS_ROWS, LANES), lambda h: (h, 0, 0))],
        out_specs=pl.BlockSpec((1, NA_BIAS_PAIRS, GRID_W, LANES), lambda h: (h, 0, 0, 0)),
        out_shape=jax.ShapeDtypeStruct((NA_HEADS, NA_BIAS_PAIRS, GRID_W, LANES), F32),
        compiler_params=_cparams(1, VMEM_SMALL_MIB),
        name="na_bias_table",
    )(rpb_lanes)


def _na_kernel(q_ref, k_ref, v_ref, bias_ref, o_ref, s_scr, *, rows, heads, rows_per_step):
    nk = NA_KH * GRID_W
    n_steps = rows // rows_per_step

    def row_geometry(step, rr):
        r = step * rows_per_step + rr
        rs = jnp.clip(r - NA_KH // 2, 0, rows - NA_KH)
        d0 = rs - r + (NA_KH - 1)
        return pl.multiple_of(r * GRID_W, GRID_W), pl.multiple_of(rs * GRID_W, GRID_W), d0

    def score_stage(step, slot):
        for rr in range(rows_per_step):
            q0, k0, d0 = row_geometry(step, rr)
            for h in range(heads):
                cols = slice(h * NA_HEAD_DIM, (h + 1) * NA_HEAD_DIM)
                q = q_ref[pl.ds(q0, GRID_W), cols]
                k = k_ref[pl.ds(k0, nk), cols]
                s = lax.dot_general(q, k, (((1,), (1,)), ((), ())), preferred_element_type=F32)
                bias = jnp.concatenate([bias_ref[h, d0 + 2 * p] for p in range(NA_KH // 2)], axis=1)
                s_scr[slot, rr * heads + h] = s + bias

    def value_stage(step, slot):
        for rr in range(rows_per_step):
            q0, k0, _ = row_geometry(step, rr)
            for h in range(heads):
                cols = slice(h * NA_HEAD_DIM, (h + 1) * NA_HEAD_DIM)
                s = s_scr[slot, rr * heads + h]
                p = jnp.exp2(s - jnp.max(s, axis=-1, keepdims=True))
                l = jnp.sum(p, axis=-1, keepdims=True)
                o = jnp.dot(p.astype(BF16), v_ref[pl.ds(k0, nk), cols], preferred_element_type=F32)
                o_ref[pl.ds(q0, GRID_W), cols] = (o / l).astype(o_ref.dtype)

    score_stage(0, 0)

    def body(u, carry):
        score_stage(2 * u + 1, 1)
        value_stage(2 * u, 0)
        score_stage(2 * u + 2, 0)
        value_stage(2 * u + 1, 1)
        return carry

    lax.fori_loop(0, n_steps // 2 - 1, body, 0)
    score_stage(n_steps - 1, 1)
    value_stage(n_steps - 2, 0)
    value_stage(n_steps - 1, 1)


def na_attention(qkv, bias, batch, seq, heads_per_step=4, rows_per_step=2):
    g = heads_per_step
    w = g * NA_HEAD_DIM
    groups = NA_HEADS // g
    rows = seq // GRID_W
    assert rows % rows_per_step == 0 and rows >= NA_KH
    return pl.pallas_call(
        functools.partial(_na_kernel, rows=rows, heads=g, rows_per_step=rows_per_step),
        grid=(batch, groups),
        in_specs=[pl.BlockSpec((seq, w), lambda b, j: (b, j)),
                  pl.BlockSpec((seq, w), lambda b, j: (b, groups + j)),
                  pl.BlockSpec((seq, w), lambda b, j: (b, 2 * groups + j)),
                  pl.BlockSpec((g, NA_BIAS_PAIRS, GRID_W, LANES), lambda b, j: (j, 0, 0, 0))],
        out_specs=pl.BlockSpec((seq, w), lambda b, j: (b, j)),
        out_shape=jax.ShapeDtypeStruct((batch * seq, NA_WIDTH), BF16),
        scratch_shapes=[pltpu.VMEM((2, rows_per_step * g, GRID_W, NA_KH * GRID_W), F32)],
        compiler_params=_cparams(2, VMEM_MIB),
        name="na_attention",
    )(qkv, qkv, qkv, bias)


def _mem_block_kernel(h_ref, o_ref, g_mix_ref, g_pre_ref, wq_ref, kv_ref, wo_ref, g_post_ref, g_next_ref,
                      ho_ref, uo_ref, *, scale):
    h1 = h_ref[...] + _rms(o_ref[...].astype(F32), g_mix_ref[...])
    u = _rms(h1, g_pre_ref[...]).astype(BF16)
    q = (jnp.dot(u, wq_ref[...], preferred_element_type=F32) * scale).astype(BF16)
    outs = []
    for hd in range(MEM_HEADS):
        qh = q[:, hd * MEM_HEAD_DIM:(hd + 1) * MEM_HEAD_DIM]
        k = kv_ref[:, 2 * hd * MEM_HEAD_DIM:(2 * hd + 1) * MEM_HEAD_DIM]
        v = kv_ref[:, (2 * hd + 1) * MEM_HEAD_DIM:(2 * hd + 2) * MEM_HEAD_DIM]
        s = lax.dot_general(qh, k, (((1,), (1,)), ((), ())), preferred_element_type=F32)
        p = jnp.exp2(s - jnp.max(s, axis=-1, keepdims=True))
        l = jnp.sum(p, axis=-1, keepdims=True)
        outs.append((jnp.dot(p.astype(BF16), v, preferred_element_type=F32) / l).astype(BF16))
    a = jnp.dot(jnp.concatenate(outs, axis=1), wo_ref[...], preferred_element_type=F32)
    h2 = h1 + _rms(a, g_post_ref[...])
    ho_ref[...] = h2
    uo_ref[...] = _rms(h2, g_next_ref[...]).astype(uo_ref.dtype)


def mem_block(h, o, g_mix_post, g_pre, w_q, kv, w_o, g_post, g_next, batch, seq, bq=256):
    mem_len = kv.shape[0] // batch
    d = h.shape[1]
    bq = min(bq, seq)
    nq = seq // bq
    width = MEM_HEADS * MEM_HEAD_DIM
    row = pl.BlockSpec((bq, d), lambda b, i: (b * nq + i, 0))
    vec = pl.BlockSpec((1, d), lambda b, i: (0, 0))
    once = pl.Buffered(1)
    return pl.pallas_call(
        functools.partial(_mem_block_kernel, scale=LOG2_E * float(MEM_HEAD_DIM) ** -0.5),
        grid=(batch, nq),
        in_specs=[row, row, vec, vec,
                  pl.BlockSpec((d, width), lambda b, i: (0, 0), pipeline_mode=once),
                  pl.BlockSpec((mem_len, 2 * width), lambda b, i: (b, 0)),
                  pl.BlockSpec((width, d), lambda b, i: (0, 0), pipeline_mode=once),
                  vec, vec],
        out_specs=[row, row],
        out_shape=[jax.ShapeDtypeStruct((batch * seq, d), F32), jax.ShapeDtypeStruct((batch * seq, d), BF16)],
        compiler_params=_cparams(2, VMEM_MIB),
        name="mem_block",
    )(h, o, _row(g_mix_post), _row(g_pre), w_q.astype(BF16), kv, w_o.astype(BF16), _row(g_post), _row(g_next))


def kernel(x, mem, positions, ffn1_w_gate, ffn1_w_up, ffn1_w_down, g_ffn1, w_in, g_q_a, w_q_b, g_kv_a, w_kv_b, na_rpb, w_out, g_mix, g_mem_in, w_mem_q, w_mem_kv, w_mem_o, g_mem_attn, ffn2_w_gate, ffn2_w_up, ffn2_w_down, g_ffn2, g_final):
    batch, seq, d = x.shape
    m = batch * seq
    depth = ffn1_w_gate.shape[0]
    pos = positions.reshape(m, 1).astype(jnp.int32)
    invf_q, invf_k = _inv_freq_lanes(LANES), _inv_freq_lanes(MLA_ROPE_DIM)
    mem2 = mem.reshape(-1, d)

    h = x.reshape(m, d)
    u = norm_cast(h, g_ffn1[0, 0])
    out = None
    for l in range(depth):
        f = ffn(u, ffn1_w_gate, ffn1_w_up, ffn1_w_down, l)
        h, u = resid_norm(h, f, g_ffn1[l, 1], FFN_RES_WEIGHT, g_mix[l, 0])

        qkv = mm_nt(u, jnp.swapaxes(w_in, 1, 2), l, BF16, bm=2048, bn=512, n_cols=IN_PROJ_MAIN,
                    scaled_cols=NA_WIDTH, scale=LOG2_E * float(NA_HEAD_DIM) ** -0.5,
                    single_buffer_x=True, name="in_proj")
        q_cat = q_b_proj(qkv, g_q_a[l], w_q_b[l], pos, invf_q)
        k_cat, v_mla = kv_b_proj(qkv, g_kv_a[l], w_kv_b[l], u, w_in[l, :, IN_PROJ_MAIN:], pos, invf_k)
        o_mla = mla_attention(q_cat, k_cat, v_mla, batch, seq)
        o_na = na_attention(qkv, na_bias_table(na_rpb[l]), batch, seq)
        o = mm2(o_na, o_mla, w_out, l, BF16, bm=2048, bn=512, single_buffer_x=True, name="out_proj")

        mem_n = norm_cast(mem2, g_mem_in[l])
        kv_mem = mm(mem_n, w_mem_kv, l, BF16, bm=1024, bn=512, name="mem_kv_proj")
        h, u = mem_block(h, o, g_mix[l, 1], g_mem_attn[l, 0], w_mem_q[l], kv_mem, w_mem_o[l],
                         g_mem_attn[l, 1], g_ffn2[l, 0], batch, seq)

        f = ffn(u, ffn2_w_gate, ffn2_w_up, ffn2_w_down, l)
        if l + 1 < depth:
            h = resid_final(h, f, g_ffn2[l, 1], FFN_RES_WEIGHT, g_final[l])
            u = norm_cast(h, g_ffn1[l + 1, 0])
        else:
            out = resid_final(h, f, g_ffn2[l, 1], FFN_RES_WEIGHT, g_final[l])
    return out.reshape(batch, seq, d)
```

```python
import functools

import jax
import jax.numpy as jnp
from jax import lax
from jax.experimental import pallas as pl
from jax.experimental.pallas import tpu as pltpu

F32 = jnp.float32
BF16 = jnp.bfloat16

GRID_W = 64
NA_HEADS = 16
NA_HEAD_DIM = 128
NA_KH = 8
NA_KW = 16
MLA_HEADS = 16
MLA_Q_RANK = 1024
MLA_KV_RANK = 512
MLA_NOPE_DIM = 128
MLA_ROPE_DIM = 64
MLA_V_DIM = 128
MLA_QK_PAD = 256
MLA_VT_ROWS = MLA_V_DIM + 16
LOG2_E = 1.4426950408889634
ROPE_THETA = 10000.0
MEM_HEADS = 4
MEM_HEAD_DIM = 128
FFN_RES_WEIGHT = 0.5
NORM_EPS = 1e-6
NEG_INF = -1e30
NA_WIDTH = NA_HEADS * NA_HEAD_DIM
MLA_WIDTH = MLA_HEADS * MLA_V_DIM
IN_PROJ_MAIN = 3 * NA_WIDTH + MLA_Q_RANK + MLA_KV_RANK
LANES = 128
MIB = 1024 * 1024
VMEM_SMALL_MIB = 32
VMEM_MIB = 56
VMEM_MAX_MIB = 60


def _cparams(n_axes, vmem_mib):
    return pltpu.CompilerParams(dimension_semantics=("parallel",) * n_axes,
                                vmem_limit_bytes=vmem_mib * MIB)


def _rms(x, g):
    return x * lax.rsqrt(jnp.mean(x * x, axis=-1, keepdims=True) + NORM_EPS) * g


def _row(v):
    return v.reshape(1, -1).astype(F32)


def _norm_cast_kernel(x_ref, g_ref, o_ref):
    o_ref[...] = _rms(x_ref[...], g_ref[...]).astype(o_ref.dtype)


def norm_cast(x, g, bm=512):
    m, d = x.shape
    bm = min(bm, m)
    return pl.pallas_call(
        _norm_cast_kernel,
        grid=(m // bm,),
        in_specs=[pl.BlockSpec((bm, d), lambda i: (i, 0)), pl.BlockSpec((1, d), lambda i: (0, 0))],
        out_specs=pl.BlockSpec((bm, d), lambda i: (i, 0)),
        out_shape=jax.ShapeDtypeStruct((m, d), BF16),
        compiler_params=_cparams(1, VMEM_SMALL_MIB),
        name="norm_cast",
    )(x, _row(g))


def _resid_norm_kernel(h_ref, f_ref, gp_ref, gn_ref, ho_ref, uo_ref, *, weight):
    h = h_ref[...] + weight * _rms(f_ref[...].astype(F32), gp_ref[...])
    ho_ref[...] = h
    uo_ref[...] = _rms(h, gn_ref[...]).astype(uo_ref.dtype)


def resid_norm(h, f, g_post, weight, g_next, bm=512):
    m, d = h.shape
    bm = min(bm, m)
    row = pl.BlockSpec((bm, d), lambda i: (i, 0))
    vec = pl.BlockSpec((1, d), lambda i: (0, 0))
    return pl.pallas_call(
        functools.partial(_resid_norm_kernel, weight=weight),
        grid=(m // bm,),
        in_specs=[row, row, vec, vec],
        out_specs=[row, row],
        out_shape=[jax.ShapeDtypeStruct((m, d), F32), jax.ShapeDtypeStruct((m, d), BF16)],
        compiler_params=_cparams(1, VMEM_MAX_MIB),
        name="resid_norm",
    )(h, f, _row(g_post), _row(g_next))


def _resid_final_kernel(h_ref, f_ref, gp_ref, gn_ref, o_ref, *, weight):
    h = h_ref[...] + weight * _rms(f_ref[...].astype(F32), gp_ref[...])
    o_ref[...] = _rms(h, gn_ref[...])


def resid_final(h, f, g_post, weight, g_final, bm=512):
    m, d = h.shape
    bm = min(bm, m)
    row = pl.BlockSpec((bm, d), lambda i: (i, 0))
    vec = pl.BlockSpec((1, d), lambda i: (0, 0))
    return pl.pallas_call(
        functools.partial(_resid_final_kernel, weight=weight),
        grid=(m // bm,),
        in_specs=[row, row, vec, vec],
        out_specs=row,
        out_shape=jax.ShapeDtypeStruct((m, d), F32),
        compiler_params=_cparams(1, VMEM_MIB),
        name="resid_final",
    )(h, f, _row(g_post), _row(g_final))


def _x_spec(bm, k, single_buffer):
    if single_buffer:
        return pl.BlockSpec((bm, k), lambda i, j: (i, 0), pipeline_mode=pl.Buffered(1))
    return pl.BlockSpec((bm, k), lambda i, j: (i, 0))


def _w_spec(k, bn, layer):
    return pl.BlockSpec((None, k, bn), lambda i, j: (layer, 0, j))


def _mm_kernel(x_ref, w_ref, o_ref):
    o_ref[...] = jnp.dot(x_ref[...], w_ref[...].astype(BF16),
                         preferred_element_type=F32).astype(o_ref.dtype)


def mm(x, w, layer, out_dtype, bm, bn, n_cols=None, single_buffer_x=False, vmem_mib=VMEM_MIB, name="mm"):
    m, k = x.shape
    n = w.shape[2] if n_cols is None else n_cols
    bm, bn = min(bm, m), min(bn, n)
    assert m % bm == 0 and n % bn == 0 and w.shape[1] == k
    return pl.pallas_call(
        _mm_kernel,
        grid=(m // bm, n // bn),
        in_specs=[_x_spec(bm, k, single_buffer_x), _w_spec(k, bn, layer)],
        out_specs=pl.BlockSpec((bm, bn), lambda i, j: (i, j)),
        out_shape=jax.ShapeDtypeStruct((m, n), out_dtype),
        compiler_params=_cparams(2, vmem_mib),
        name=name,
    )(x, w)


def _mm_nt_kernel(x_ref, wt_ref, o_ref, *, scaled_blocks, scale):
    acc = lax.dot_general(x_ref[...], wt_ref[...].astype(BF16), (((1,), (1,)), ((), ())),
                          preferred_element_type=F32)
    if scaled_blocks:
        acc = acc * jnp.where(pl.program_id(1) < scaled_blocks, scale, 1.0)
    o_ref[...] = acc.astype(o_ref.dtype)


def mm_nt(x, wt, layer, out_dtype, bm, bn, n_cols=None, scaled_cols=0, scale=1.0, single_buffer_x=False,
          vmem_mib=VMEM_MIB, name="mm_nt"):
    m, k = x.shape
    n = wt.shape[1] if n_cols is None else n_cols
    bm, bn = min(bm, m), min(bn, n)
    assert m % bm == 0 and n % bn == 0 and wt.shape[2] == k and scaled_cols % bn == 0
    return pl.pallas_call(
        functools.partial(_mm_nt_kernel, scaled_blocks=scaled_cols // bn, scale=scale),
        grid=(m // bm, n // bn),
        in_specs=[_x_spec(bm, k, single_buffer_x), pl.BlockSpec((None, bn, k), lambda i, j: (layer, j, 0))],
        out_specs=pl.BlockSpec((bm, bn), lambda i, j: (i, j)),
        out_shape=jax.ShapeDtypeStruct((m, n), out_dtype),
        compiler_params=_cparams(2, vmem_mib),
        name=name,
    )(x, wt)


def _mm2_kernel(xa_ref, xb_ref, w_ref, o_ref):
    ka = xa_ref.shape[1]
    w = w_ref[...].astype(BF16)
    acc = jnp.dot(xa_ref[...], w[:ka], preferred_element_type=F32)
    acc = acc + jnp.dot(xb_ref[...], w[ka:], preferred_element_type=F32)
    o_ref[...] = acc.astype(o_ref.dtype)


def mm2(xa, xb, w, layer, out_dtype, bm, bn, single_buffer_x=False, vmem_mib=VMEM_MIB, name="mm2"):
    m, ka = xa.shape
    kb = xb.shape[1]
    n = w.shape[2]
    bm, bn = min(bm, m), min(bn, n)
    assert m % bm == 0 and n % bn == 0 and w.shape[1] == ka + kb
    return pl.pallas_call(
        _mm2_kernel,
        grid=(m // bm, n // bn),
        in_specs=[_x_spec(bm, ka, single_buffer_x), _x_spec(bm, kb, single_buffer_x),
                  _w_spec(ka + kb, bn, layer)],
        out_specs=pl.BlockSpec((bm, bn), lambda i, j: (i, j)),
        out_shape=jax.ShapeDtypeStruct((m, n), out_dtype),
        compiler_params=_cparams(2, vmem_mib),
        name=name,
    )(xa, xb, w)


def _gateup_kernel(x_ref, wg_ref, wu_ref, wd_ref, o_ref, wdo_ref):
    x = x_ref[...]
    g = jnp.dot(x, wg_ref[...].astype(BF16), preferred_element_type=F32)
    u = jnp.dot(x, wu_ref[...].astype(BF16), preferred_element_type=F32)
    o_ref[...] = (g * jax.nn.sigmoid(g) * u).astype(o_ref.dtype)
    wdo_ref[...] = wd_ref[...].astype(wdo_ref.dtype)


def gateup(x, w_gate, w_up, w_down, layer, bm=2048, bf=256):
    m, k = x.shape
    f = w_gate.shape[2]
    d_out = w_down.shape[2]
    bm, bf = min(bm, m), min(bf, f)
    assert m % bm == 0 and f % bf == 0 and w_down.shape[1] == f
    n_j = f // bf
    slab = f // ((m // bm) * n_j)
    assert slab * (m // bm) * n_j == f and slab % 16 == 0
    return pl.pallas_call(
        _gateup_kernel,
        grid=(m // bm, n_j),
        in_specs=[_x_spec(bm, k, True), _w_spec(k, bf, layer), _w_spec(k, bf, layer),
                  pl.BlockSpec((None, slab, d_out), lambda i, j: (layer, i * n_j + j, 0))],
        out_specs=[pl.BlockSpec((bm, bf), lambda i, j: (i, j)),
                   pl.BlockSpec((None, slab, d_out), lambda i, j: (0, i * n_j + j, 0))],
        out_shape=[jax.ShapeDtypeStruct((m, f), BF16), jax.ShapeDtypeStruct((1, f, d_out), BF16)],
        compiler_params=_cparams(2, VMEM_MIB),
        name="ffn_gateup",
    )(x, w_gate, w_up, w_down)


def ffn(u, w_gate, w_up, w_down, layer):
    hidden, w_down_bf16 = gateup(u, w_gate, w_up, w_down, layer)
    return mm(hidden, w_down_bf16, 0, BF16, bm=512, bn=512, name="ffn_down")


def _rope_tables(pos_ref, invf_ref, width):
    ang = pos_ref[...].astype(F32) * invf_ref[...]
    lane = lax.broadcasted_iota(jnp.int32, ang.shape, 1)
    first_half = (lane & (MLA_ROPE_DIM - 1)) < MLA_ROPE_DIM // 2
    valid = lane < width
    sin = jnp.sin(ang)
    cos_t = jnp.where(valid, jnp.cos(ang), 0.0)
    sin_t = jnp.where(valid, jnp.where(first_half, -sin, sin), 0.0)
    return cos_t, sin_t, first_half


def _rope_apply(t, cos_t, sin_t, first_half):
    half = MLA_ROPE_DIM // 2
    partner = jnp.where(first_half, pltpu.roll(t, LANES - half, 1), pltpu.roll(t, half, 1))
    return t * cos_t + partner * sin_t


def _inv_freq_lanes(width):
    half = MLA_ROPE_DIM // 2
    inv_freq = 1.0 / (ROPE_THETA ** (jnp.arange(half, dtype=F32) / half))
    return jnp.concatenate([jnp.tile(inv_freq, width // half), jnp.zeros((LANES - width,), F32)]).reshape(1, LANES)


def _qb_kernel(cq_ref, g_ref, wn_ref, wp_ref, pos_ref, invf_ref, o_ref):
    cn = _rms(cq_ref[...].astype(F32), g_ref[...]).astype(BF16)
    qn = jnp.dot(cn, wn_ref[...], preferred_element_type=F32)
    qp = jnp.dot(cn, wp_ref[...], preferred_element_type=F32)
    cos_t, sin_t, first_half = _rope_tables(pos_ref, invf_ref, LANES)
    low = lax.broadcasted_iota(jnp.int32, cos_t.shape, 1) < MLA_ROPE_DIM
    for pair in range(MLA_HEADS // 2):
        r = _rope_apply(qp[:, pair * LANES:(pair + 1) * LANES], cos_t, sin_t, first_half)
        pe = (jnp.where(low, r, 0.0), jnp.where(low, pltpu.roll(r, LANES - MLA_ROPE_DIM, 1), 0.0))
        for k in range(2):
            h = 2 * pair + k
            c0 = h * MLA_QK_PAD
            o_ref[:, c0:c0 + MLA_NOPE_DIM] = qn[:, h * MLA_NOPE_DIM:(h + 1) * MLA_NOPE_DIM].astype(o_ref.dtype)
            o_ref[:, c0 + MLA_NOPE_DIM:c0 + MLA_QK_PAD] = pe[k].astype(o_ref.dtype)


def q_b_proj(qkv, g_q_a, w_q_b, pos, invf, bm=512):
    m = qkv.shape[0]
    bm = min(bm, m)
    h, dq = MLA_HEADS, MLA_NOPE_DIM + MLA_ROPE_DIM
    w = w_q_b.reshape(MLA_Q_RANK, h, dq) * (LOG2_E * float(dq) ** -0.5)
    w_nope = w[:, :, :MLA_NOPE_DIM].reshape(MLA_Q_RANK, h * MLA_NOPE_DIM).astype(BF16)
    w_pe = w[:, :, MLA_NOPE_DIM:].reshape(MLA_Q_RANK, h * MLA_ROPE_DIM).astype(BF16)
    cq_block = (3 * NA_WIDTH) // MLA_Q_RANK
    return pl.pallas_call(
        _qb_kernel,
        grid=(m // bm,),
        in_specs=[pl.BlockSpec((bm, MLA_Q_RANK), lambda i: (i, cq_block)),
                  pl.BlockSpec((1, MLA_Q_RANK), lambda i: (0, 0)),
                  pl.BlockSpec((MLA_Q_RANK, h * MLA_NOPE_DIM), lambda i: (0, 0)),
                  pl.BlockSpec((MLA_Q_RANK, h * MLA_ROPE_DIM), lambda i: (0, 0)),
                  pl.BlockSpec((bm, 1), lambda i: (i, 0)),
                  pl.BlockSpec((1, LANES), lambda i: (0, 0))],
        out_specs=pl.BlockSpec((bm, h * MLA_QK_PAD), lambda i: (i, 0)),
        out_shape=jax.ShapeDtypeStruct((m, h * MLA_QK_PAD), BF16),
        compiler_params=_cparams(1, VMEM_MIB),
        name="mla_q_proj",
    )(qkv, _row(g_q_a), w_nope, w_pe, pos, invf)


def _kvb_kernel(ckv_ref, g_ref, wk_ref, wvt_ref, u_ref, wt_ref, pos_ref, invf_ref, k_ref, vt_ref):
    cn = _rms(ckv_ref[...].astype(F32), g_ref[...]).astype(BF16)
    kn = jnp.dot(cn, wk_ref[...], preferred_element_type=F32)
    kr = jnp.dot(u_ref[...], wt_ref[...], preferred_element_type=F32)
    cos_t, sin_t, first_half = _rope_tables(pos_ref, invf_ref, MLA_ROPE_DIM)
    kpe = _rope_apply(kr, cos_t, sin_t, first_half).astype(k_ref.dtype)
    for h in range(MLA_HEADS):
        k_ref[:, h * MLA_QK_PAD:h * MLA_QK_PAD + MLA_NOPE_DIM] = (
            kn[:, h * MLA_NOPE_DIM:(h + 1) * MLA_NOPE_DIM].astype(k_ref.dtype))
        k_ref[:, h * MLA_QK_PAD + MLA_NOPE_DIM:(h + 1) * MLA_QK_PAD] = kpe
    vt = lax.dot_general(wvt_ref[...], cn, (((1,), (1,)), ((), ())), preferred_element_type=F32)
    ones = jnp.ones((MLA_VT_ROWS - MLA_V_DIM, vt.shape[1]), vt_ref.dtype)
    for h in range(MLA_HEADS):
        r0 = h * MLA_VT_ROWS
        vt_ref[r0:r0 + MLA_V_DIM, :] = vt[h * MLA_V_DIM:(h + 1) * MLA_V_DIM].astype(vt_ref.dtype)
        vt_ref[r0 + MLA_V_DIM:r0 + MLA_VT_ROWS, :] = ones


def kv_b_proj(qkv, g_kv_a, w_kv_b, u, w_rope_in, pos, invf, bm=512):
    m, d = u.shape
    bm = min(bm, m)
    h = MLA_HEADS
    ckv_block = (3 * NA_WIDTH + MLA_Q_RANK) // MLA_KV_RANK
    wt = jnp.pad(w_rope_in, ((0, 0), (0, LANES - MLA_ROPE_DIM))).astype(BF16)
    w3 = w_kv_b.reshape(MLA_KV_RANK, h, MLA_NOPE_DIM + MLA_V_DIM)
    w_k = w3[:, :, :MLA_NOPE_DIM].reshape(MLA_KV_RANK, h * MLA_NOPE_DIM).astype(BF16)
    w_vt = w3[:, :, MLA_NOPE_DIM:].reshape(MLA_KV_RANK, h * MLA_V_DIM).T.astype(BF16)
    return pl.pallas_call(
        _kvb_kernel,
        grid=(m // bm,),
        in_specs=[pl.BlockSpec((bm, MLA_KV_RANK), lambda i: (i, ckv_block)),
                  pl.BlockSpec((1, MLA_KV_RANK), lambda i: (0, 0)),
                  pl.BlockSpec((MLA_KV_RANK, h * MLA_NOPE_DIM), lambda i: (0, 0)),
                  pl.BlockSpec((h * MLA_V_DIM, MLA_KV_RANK), lambda i: (0, 0)),
                  pl.BlockSpec((bm, d), lambda i: (i, 0)),
                  pl.BlockSpec((d, LANES), lambda i: (0, 0)),
                  pl.BlockSpec((bm, 1), lambda i: (i, 0)),
                  pl.BlockSpec((1, LANES), lambda i: (0, 0))],
        out_specs=[pl.BlockSpec((bm, h * MLA_QK_PAD), lambda i: (i, 0)),
                   pl.BlockSpec((h * MLA_VT_ROWS, bm), lambda i: (0, i))],
        out_shape=[jax.ShapeDtypeStruct((m, h * MLA_QK_PAD), BF16),
                   jax.ShapeDtypeStruct((h * MLA_VT_ROWS, m), BF16)],
        compiler_params=_cparams(1, VMEM_MIB),
        name="mla_kv_proj",
    )(qkv, _row(g_kv_a), w_k, w_vt, u, wt, pos, invf)


def _mla_kernel(q_ref, qn_ref, k_ref, kn_ref, vt_ref, o_ref, s_scr, *, heads, bq):
    def scores(q_tile_ref, row0, keys_ref, slot, h):
        qk = slice(h * MLA_QK_PAD, (h + 1) * MLA_QK_PAD)
        s_scr[slot, h] = lax.dot_general(keys_ref[:, qk], q_tile_ref[row0:row0 + bq, qk],
                                         (((1,), (1,)), ((), ())), preferred_element_type=F32)

    def values(slot, row0, h):
        s = s_scr[slot, h]
        p = jnp.exp2(s - jnp.max(s, axis=0, keepdims=True)).astype(BF16)
        ot = jnp.dot(vt_ref[h * MLA_VT_ROWS:(h + 1) * MLA_VT_ROWS, :], p,
                     preferred_element_type=F32)
        o = ot[:MLA_V_DIM] / ot[MLA_V_DIM:MLA_V_DIM + 1]
        o_ref[row0:row0 + bq, h * MLA_V_DIM:(h + 1) * MLA_V_DIM] = o.T.astype(o_ref.dtype)

    first_step = (pl.program_id(0) == 0) & (pl.program_id(1) == 0) & (pl.program_id(2) == 0)

    @pl.when(first_step)
    def _():
        for h in range(heads):
            scores(q_ref, 0, k_ref, 0, h)

    for h in range(heads):
        scores(q_ref, bq, k_ref, 1, h)
        values(0, 0, h)
    for h in range(heads):
        scores(qn_ref, 0, kn_ref, 0, h)
        values(1, bq, h)


def mla_attention(q, k, vt, batch, seq, bq=512, heads_per_step=2):
    bq = min(bq, seq // 2)
    pairs = seq // (2 * bq)
    g = heads_per_step
    groups = MLA_HEADS // g
    n_steps = batch * groups * pairs

    def next_step(b, h, i):
        flat = jnp.minimum((b * groups + h) * pairs + i + 1, n_steps - 1)
        return flat // (groups * pairs), (flat // pairs) % groups, flat % pairs

    def q_next_map(b, h, i):
        nb, nh, ni = next_step(b, h, i)
        return (nb * pairs + ni) * 2, nh

    def k_next_map(b, h, i):
        nb, nh, _ = next_step(b, h, i)
        return nb, nh

    return pl.pallas_call(
        functools.partial(_mla_kernel, heads=g, bq=bq),
        grid=(batch, groups, pairs),
        in_specs=[pl.BlockSpec((2 * bq, g * MLA_QK_PAD), lambda b, h, i: (b * pairs + i, h)),
                  pl.BlockSpec((bq, g * MLA_QK_PAD), q_next_map),
                  pl.BlockSpec((seq, g * MLA_QK_PAD), lambda b, h, i: (b, h)),
                  pl.BlockSpec((seq, g * MLA_QK_PAD), k_next_map),
                  pl.BlockSpec((g * MLA_VT_ROWS, seq), lambda b, h, i: (h, b))],
        out_specs=pl.BlockSpec((2 * bq, g * MLA_V_DIM), lambda b, h, i: (b * pairs + i, h)),
        out_shape=jax.ShapeDtypeStruct((batch * seq, MLA_WIDTH), BF16),
        scratch_shapes=[pltpu.VMEM((2, g, seq, bq), F32)],
        compiler_params=pltpu.CompilerParams(dimension_semantics=("arbitrary",) * 3,
                                             vmem_limit_bytes=VMEM_MIB * MIB),
        name="mla_attention",
    )(q, q, k, k, vt)


NA_BIAS_ROWS = 2 * NA_KH - 1
NA_BIAS_COLS = 2 * NA_KW - 1
NA_BIAS_PAIRS = NA_BIAS_ROWS - 1


def _na_bias_kernel(rpb_ref, o_ref):
    row = lax.broadcasted_iota(jnp.int32, (GRID_W, LANES), 0)
    lane = lax.broadcasted_iota(jnp.int32, (GRID_W, LANES), 1)
    kc = lane & (GRID_W - 1)
    left = lane < GRID_W
    cstart = jnp.clip(row - NA_KW // 2, 0, GRID_W - NA_KW)
    in_window = (kc >= cstart) & (kc < cstart + NA_KW)
    toeplitz = []
    for d in range(NA_BIAS_ROWS):
        vec = jnp.broadcast_to(rpb_ref[d:d + 1, :], (GRID_W, LANES))
        toeplitz.append(pltpu.roll(vec, LANES - (NA_KW - 1), 1, stride=1, stride_axis=0))
    for d in range(NA_BIAS_PAIRS):
        tile = jnp.where(left, toeplitz[d], pltpu.roll(toeplitz[d + 1], GRID_W, 1))
        o_ref[0, d] = jnp.where(in_window, tile * LOG2_E, NEG_INF)


def na_bias_table(rpb):
    rpb_lanes = jnp.pad(rpb.astype(F32), ((0, 0), (0, 0), (0, LANES - NA_BIAS_COLS)))
    return pl.pallas_call(
        _na_bias_kernel,
        grid=(NA_HEADS,),
        in_specs=[pl.BlockSpec((None, NA_BIAS_ROWS, LANES), lambda h: (h, 0, 0))],
        out_specs=pl.BlockSpec((1, NA_BIAS_PAIRS, GRID_W, LANES), lambda h: (h, 0, 0, 0)),
        out_shape=jax.ShapeDtypeStruct((NA_HEADS, NA_BIAS_PAIRS, GRID_W, LANES), F32),
        compiler_params=_cparams(1, VMEM_SMALL_MIB),
        name="na_bias_table",
    )(rpb_lanes)


def _na_kernel(q_ref, k_ref, v_ref, bias_ref, o_ref, s_scr, *, rows, heads, rows_per_step):
    nk = NA_KH * GRID_W
    n_steps = rows // rows_per_step

    def row_geometry(step, rr):
        r = step * rows_per_step + rr
        rs = jnp.clip(r - NA_KH // 2, 0, rows - NA_KH)
        d0 = rs - r + (NA_KH - 1)
        return pl.multiple_of(r * GRID_W, GRID_W), pl.multiple_of(rs * GRID_W, GRID_W), d0

    chains = [(rr, h) for rr in range(rows_per_step) for h in range(heads)]

    def scores(step, slot, rr, h):
        q0, k0, d0 = row_geometry(step, rr)
        cols = slice(h * NA_HEAD_DIM, (h + 1) * NA_HEAD_DIM)
        q = q_ref[pl.ds(q0, GRID_W), cols]
        k = k_ref[pl.ds(k0, nk), cols]
        s = lax.dot_general(q, k, (((1,), (1,)), ((), ())), preferred_element_type=F32)
        bias = jnp.concatenate([bias_ref[h, d0 + 2 * p] for p in range(NA_KH // 2)], axis=1)
        s_scr[slot, rr * heads + h] = s + bias

    def values(step, slot, rr, h):
        q0, k0, _ = row_geometry(step, rr)
        cols = slice(h * NA_HEAD_DIM, (h + 1) * NA_HEAD_DIM)
        s = s_scr[slot, rr * heads + h]
        p = jnp.exp2(s - jnp.max(s, axis=-1, keepdims=True))
        l = jnp.sum(p, axis=-1, keepdims=True)
        o = jnp.dot(p.astype(BF16), v_ref[pl.ds(k0, nk), cols], preferred_element_type=F32)
        o_ref[pl.ds(q0, GRID_W), cols] = (o / l).astype(o_ref.dtype)

    def pipelined(score_step, score_slot, value_step, value_slot):
        for rr, h in chains:
            scores(score_step, score_slot, rr, h)
            values(value_step, value_slot, rr, h)

    for rr, h in chains:
        scores(0, 0, rr, h)

    def body(u, carry):
        pipelined(2 * u + 1, 1, 2 * u, 0)
        pipelined(2 * u + 2, 0, 2 * u + 1, 1)
        return carry

    lax.fori_loop(0, n_steps // 2 - 1, body, 0)
    pipelined(n_steps - 1, 1, n_steps - 2, 0)
    for rr, h in chains:
        values(n_steps - 1, 1, rr, h)


def na_attention(qkv, bias, batch, seq, heads_per_step=4, rows_per_step=2):
    g = heads_per_step
    w = g * NA_HEAD_DIM
    groups = NA_HEADS // g
    rows = seq // GRID_W
    assert rows % rows_per_step == 0 and rows >= NA_KH
    return pl.pallas_call(
        functools.partial(_na_kernel, rows=rows, heads=g, rows_per_step=rows_per_step),
        grid=(batch, groups),
        in_specs=[pl.BlockSpec((seq, w), lambda b, j: (b, j)),
                  pl.BlockSpec((seq, w), lambda b, j: (b, groups + j)),
                  pl.BlockSpec((seq, w), lambda b, j: (b, 2 * groups + j)),
                  pl.BlockSpec((g, NA_BIAS_PAIRS, GRID_W, LANES), lambda b, j: (j, 0, 0, 0))],
        out_specs=pl.BlockSpec((seq, w), lambda b, j: (b, j)),
        out_shape=jax.ShapeDtypeStruct((batch * seq, NA_WIDTH), BF16),
        scratch_shapes=[pltpu.VMEM((2, rows_per_step * g, GRID_W, NA_KH * GRID_W), F32)],
        compiler_params=_cparams(2, VMEM_MIB),
        name="na_attention",
    )(qkv, qkv, qkv, bias)


def _mem_block_kernel(h_ref, o_ref, g_mix_ref, g_pre_ref, wq_ref, kv_ref, wo_ref, g_post_ref, g_next_ref,
                      ho_ref, uo_ref, *, scale):
    h1 = h_ref[...] + _rms(o_ref[...].astype(F32), g_mix_ref[...])
    u = _rms(h1, g_pre_ref[...]).astype(BF16)
    q = (jnp.dot(u, wq_ref[...], preferred_element_type=F32) * scale).astype(BF16)
    outs = []
    for hd in range(MEM_HEADS):
        qh = q[:, hd * MEM_HEAD_DIM:(hd + 1) * MEM_HEAD_DIM]
        k = kv_ref[:, 2 * hd * MEM_HEAD_DIM:(2 * hd + 1) * MEM_HEAD_DIM]
        v = kv_ref[:, (2 * hd + 1) * MEM_HEAD_DIM:(2 * hd + 2) * MEM_HEAD_DIM]
        s = lax.dot_general(qh, k, (((1,), (1,)), ((), ())), preferred_element_type=F32)
        p = jnp.exp2(s - jnp.max(s, axis=-1, keepdims=True))
        l = jnp.sum(p, axis=-1, keepdims=True)
        outs.append((jnp.dot(p.astype(BF16), v, preferred_element_type=F32) / l).astype(BF16))
    a = jnp.dot(jnp.concatenate(outs, axis=1), wo_ref[...], preferred_element_type=F32)
    h2 = h1 + _rms(a, g_post_ref[...])
    ho_ref[...] = h2
    uo_ref[...] = _rms(h2, g_next_ref[...]).astype(uo_ref.dtype)


def mem_block(h, o, g_mix_post, g_pre, w_q, kv, w_o, g_post, g_next, batch, seq, bq=256):
    mem_len = kv.shape[0] // batch
    d = h.shape[1]
    bq = min(bq, seq)
    nq = seq // bq
    width = MEM_HEADS * MEM_HEAD_DIM
    row = pl.BlockSpec((bq, d), lambda b, i: (b * nq + i, 0))
    vec = pl.BlockSpec((1, d), lambda b, i: (0, 0))
    once = pl.Buffered(1)
    return pl.pallas_call(
        functools.partial(_mem_block_kernel, scale=LOG2_E * float(MEM_HEAD_DIM) ** -0.5),
        grid=(batch, nq),
        in_specs=[row, row, vec, vec,
                  pl.BlockSpec((d, width), lambda b, i: (0, 0), pipeline_mode=once),
                  pl.BlockSpec((mem_len, 2 * width), lambda b, i: (b, 0)),
                  pl.BlockSpec((width, d), lambda b, i: (0, 0), pipeline_mode=once),
                  vec, vec],
        out_specs=[row, row],
        out_shape=[jax.ShapeDtypeStruct((batch * seq, d), F32), jax.ShapeDtypeStruct((batch * seq, d), BF16)],
        compiler_params=_cparams(2, VMEM_MIB),
        name="mem_block",
    )(h, o, _row(g_mix_post), _row(g_pre), w_q.astype(BF16), kv, w_o.astype(BF16), _row(g_post), _row(g_next))


def kernel(x, mem, positions, ffn1_w_gate, ffn1_w_up, ffn1_w_down, g_ffn1, w_in, g_q_a, w_q_b, g_kv_a, w_kv_b, na_rpb, w_out, g_mix, g_mem_in, w_mem_q, w_mem_kv, w_mem_o, g_mem_attn, ffn2_w_gate, ffn2_w_up, ffn2_w_down, g_ffn2, g_final):
    batch, seq, d = x.shape
    m = batch * seq
    depth = ffn1_w_gate.shape[0]
    pos = positions.reshape(m, 1).astype(jnp.int32)
    invf_q, invf_k = _inv_freq_lanes(LANES), _inv_freq_lanes(MLA_ROPE_DIM)
    mem2 = mem.reshape(-1, d)

    h = x.reshape(m, d)
    u = norm_cast(h, g_ffn1[0, 0])
    out = None
    for l in range(depth):
        f = ffn(u, ffn1_w_gate, ffn1_w_up, ffn1_w_down, l)
        h, u = resid_norm(h, f, g_ffn1[l, 1], FFN_RES_WEIGHT, g_mix[l, 0])

        qkv = mm_nt(u, jnp.swapaxes(w_in, 1, 2), l, BF16, bm=2048, bn=512, n_cols=IN_PROJ_MAIN,
                    scaled_cols=NA_WIDTH, scale=LOG2_E * float(NA_HEAD_DIM) ** -0.5,
                    single_buffer_x=True, name="in_proj")
        q_cat = q_b_proj(qkv, g_q_a[l], w_q_b[l], pos, invf_q)
        k_cat, v_mla = kv_b_proj(qkv, g_kv_a[l], w_kv_b[l], u, w_in[l, :, IN_PROJ_MAIN:], pos, invf_k)
        o_mla = mla_attention(q_cat, k_cat, v_mla, batch, seq)
        o_na = na_attention(qkv, na_bias_table(na_rpb[l]), batch, seq)
        o = mm2(o_na, o_mla, w_out, l, BF16, bm=2048, bn=512, single_buffer_x=True, name="out_proj")

        mem_n = norm_cast(mem2, g_mem_in[l])
        kv_mem = mm(mem_n, w_mem_kv, l, BF16, bm=1024, bn=512, name="mem_kv_proj")
        h, u = mem_block(h, o, g_mix[l, 1], g_mem_attn[l, 0], w_mem_q[l], kv_mem, w_mem_o[l],
                         g_mem_attn[l, 1], g_ffn2[l, 0], batch, seq)

        f = ffn(u, ffn2_w_gate, ffn2_w_up, ffn2_w_down, l)
        if l + 1 < depth:
            h = resid_final(h, f, g_ffn2[l, 1], FFN_RES_WEIGHT, g_final[l])
            u = norm_cast(h, g_ffn1[l + 1, 0])
        else:
            out = resid_final(h, f, g_ffn2[l, 1], FFN_RES_WEIGHT, g_final[l])
    return out.reshape(batch, seq, d)
```

```python
import functools

import jax
import jax.numpy as jnp
from jax import lax
from jax.experimental import pallas as pl
from jax.experimental.pallas import tpu as pltpu

F32 = jnp.float32
BF16 = jnp.bfloat16

GRID_W = 64
NA_HEADS = 16
NA_HEAD_DIM = 128
NA_KH = 8
NA_KW = 16
MLA_HEADS = 16
MLA_Q_RANK = 1024
MLA_KV_RANK = 512
MLA_NOPE_DIM = 128
MLA_ROPE_DIM = 64
MLA_V_DIM = 128
MLA_QK_PAD = 256
MLA_VT_ROWS = MLA_V_DIM + 16
LOG2_E = 1.4426950408889634
ROPE_THETA = 10000.0
MEM_HEADS = 4
MEM_HEAD_DIM = 128
FFN_RES_WEIGHT = 0.5
NORM_EPS = 1e-6
NEG_INF = -1e30
NA_WIDTH = NA_HEADS * NA_HEAD_DIM
MLA_WIDTH = MLA_HEADS * MLA_V_DIM
IN_PROJ_MAIN = 3 * NA_WIDTH + MLA_Q_RANK + MLA_KV_RANK
IN_PROJ_BN = 512
LANES = 128
MIB = 1024 * 1024
VMEM_SMALL_MIB = 32
VMEM_MIB = 56
VMEM_MAX_MIB = 60


def _cparams(n_axes, vmem_mib):
    return pltpu.CompilerParams(dimension_semantics=("parallel",) * n_axes,
                                vmem_limit_bytes=vmem_mib * MIB)


def _rms(x, g):
    return x * lax.rsqrt(jnp.mean(x * x, axis=-1, keepdims=True) + NORM_EPS) * g


def _row(v):
    return v.reshape(1, -1).astype(F32)


def _norm_cast_kernel(x_ref, g_ref, o_ref):
    o_ref[...] = _rms(x_ref[...], g_ref[...]).astype(o_ref.dtype)


def norm_cast(x, g, bm=512):
    m, d = x.shape
    bm = min(bm, m)
    return pl.pallas_call(
        _norm_cast_kernel,
        grid=(m // bm,),
        in_specs=[pl.BlockSpec((bm, d), lambda i: (i, 0)), pl.BlockSpec((1, d), lambda i: (0, 0))],
        out_specs=pl.BlockSpec((bm, d), lambda i: (i, 0)),
        out_shape=jax.ShapeDtypeStruct((m, d), BF16),
        compiler_params=_cparams(1, VMEM_SMALL_MIB),
        name="norm_cast",
    )(x, _row(g))


def _resid_norm_kernel(h_ref, f_ref, gp_ref, gn_ref, ho_ref, uo_ref, *, weight):
    h = h_ref[...] + weight * _rms(f_ref[...].astype(F32), gp_ref[...])
    ho_ref[...] = h
    uo_ref[...] = _rms(h, gn_ref[...]).astype(uo_ref.dtype)


def resid_norm(h, f, g_post, weight, g_next, bm=512):
    m, d = h.shape
    bm = min(bm, m)
    row = pl.BlockSpec((bm, d), lambda i: (i, 0))
    vec = pl.BlockSpec((1, d), lambda i: (0, 0))
    return pl.pallas_call(
        functools.partial(_resid_norm_kernel, weight=weight),
        grid=(m // bm,),
        in_specs=[row, row, vec, vec],
        out_specs=[row, row],
        out_shape=[jax.ShapeDtypeStruct((m, d), F32), jax.ShapeDtypeStruct((m, d), BF16)],
        compiler_params=_cparams(1, VMEM_MAX_MIB),
        name="resid_norm",
    )(h, f, _row(g_post), _row(g_next))


def _resid_final_kernel(h_ref, f_ref, gp_ref, gn_ref, o_ref, *, weight):
    h = h_ref[...] + weight * _rms(f_ref[...].astype(F32), gp_ref[...])
    o_ref[...] = _rms(h, gn_ref[...])


def resid_final(h, f, g_post, weight, g_final, bm=512):
    m, d = h.shape
    bm = min(bm, m)
    row = pl.BlockSpec((bm, d), lambda i: (i, 0))
    vec = pl.BlockSpec((1, d), lambda i: (0, 0))
    return pl.pallas_call(
        functools.partial(_resid_final_kernel, weight=weight),
        grid=(m // bm,),
        in_specs=[row, row, vec, vec],
        out_specs=row,
        out_shape=jax.ShapeDtypeStruct((m, d), F32),
        compiler_params=_cparams(1, VMEM_MIB),
        name="resid_final",
    )(h, f, _row(g_post), _row(g_final))


def _x_spec(bm, k, single_buffer):
    if single_buffer:
        return pl.BlockSpec((bm, k), lambda i, j: (i, 0), pipeline_mode=pl.Buffered(1))
    return pl.BlockSpec((bm, k), lambda i, j: (i, 0))


def _w_spec(k, bn, layer):
    return pl.BlockSpec((None, k, bn), lambda i, j: (layer, 0, j))


def _mm_kernel(x_ref, w_ref, o_ref):
    o_ref[...] = jnp.dot(x_ref[...], w_ref[...].astype(BF16),
                         preferred_element_type=F32).astype(o_ref.dtype)


def mm(x, w, layer, out_dtype, bm, bn, n_cols=None, single_buffer_x=False, vmem_mib=VMEM_MIB, name="mm"):
    m, k = x.shape
    n = w.shape[2] if n_cols is None else n_cols
    bm, bn = min(bm, m), min(bn, n)
    assert m % bm == 0 and n % bn == 0 and w.shape[1] == k
    return pl.pallas_call(
        _mm_kernel,
        grid=(m // bm, n // bn),
        in_specs=[_x_spec(bm, k, single_buffer_x), _w_spec(k, bn, layer)],
        out_specs=pl.BlockSpec((bm, bn), lambda i, j: (i, j)),
        out_shape=jax.ShapeDtypeStruct((m, n), out_dtype),
        compiler_params=_cparams(2, vmem_mib),
        name=name,
    )(x, w)


def _mm_nt_kernel(x_ref, wt_ref, o_ref, *, scaled_blocks, scale):
    acc = lax.dot_general(x_ref[...], wt_ref[...].astype(BF16), (((1,), (1,)), ((), ())),
                          preferred_element_type=F32)
    if scaled_blocks:
        acc = acc * jnp.where(pl.program_id(1) < scaled_blocks, scale, 1.0)
    o_ref[...] = acc.astype(o_ref.dtype)


def mm_nt(x, wt, layer, out_dtype, bm, bn, n_cols=None, scaled_cols=0, scale=1.0, single_buffer_x=False,
          vmem_mib=VMEM_MIB, name="mm_nt"):
    m, k = x.shape
    n = wt.shape[1] if n_cols is None else n_cols
    bm = min(bm, m)
    assert m % bm == 0 and n % bn == 0 and wt.shape[2] == k and scaled_cols % bn == 0
    return pl.pallas_call(
        functools.partial(_mm_nt_kernel, scaled_blocks=scaled_cols // bn, scale=scale),
        grid=(m // bm, n // bn),
        in_specs=[_x_spec(bm, k, single_buffer_x), pl.BlockSpec((None, bn, k), lambda i, j: (layer, j, 0))],
        out_specs=pl.BlockSpec((None, bm, bn), lambda i, j: (j, i, 0)),
        out_shape=jax.ShapeDtypeStruct((n // bn, m, bn), out_dtype),
        compiler_params=_cparams(2, vmem_mib),
        name=name,
    )(x, wt)


def _mm2_kernel(xa_ref, xb_ref, w_ref, o_ref):
    ka = xa_ref.shape[1]
    w = w_ref[...].astype(BF16)
    acc = jnp.dot(xa_ref[...], w[:ka], preferred_element_type=F32)
    acc = acc + jnp.dot(xb_ref[...], w[ka:], preferred_element_type=F32)
    o_ref[...] = acc.astype(o_ref.dtype)


def mm2(xa, xb, w, layer, out_dtype, bm, bn, single_buffer_x=False, vmem_mib=VMEM_MIB, name="mm2"):
    m, ka = xa.shape
    kb = xb.shape[1]
    n = w.shape[2]
    bm, bn = min(bm, m), min(bn, n)
    assert m % bm == 0 and n % bn == 0 and w.shape[1] == ka + kb
    return pl.pallas_call(
        _mm2_kernel,
        grid=(m // bm, n // bn),
        in_specs=[_x_spec(bm, ka, single_buffer_x), _x_spec(bm, kb, single_buffer_x),
                  _w_spec(ka + kb, bn, layer)],
        out_specs=pl.BlockSpec((bm, bn), lambda i, j: (i, j)),
        out_shape=jax.ShapeDtypeStruct((m, n), out_dtype),
        compiler_params=_cparams(2, vmem_mib),
        name=name,
    )(xa, xb, w)


def _gateup_kernel(x_ref, wg_ref, wu_ref, wd_ref, o_ref, wdo_ref):
    x = x_ref[...]
    g = jnp.dot(x, wg_ref[...].astype(BF16), preferred_element_type=F32)
    u = jnp.dot(x, wu_ref[...].astype(BF16), preferred_element_type=F32)
    o_ref[...] = (g * jax.nn.sigmoid(g) * u).astype(o_ref.dtype)
    wdo_ref[...] = wd_ref[...].astype(wdo_ref.dtype)


def gateup(x, w_gate, w_up, w_down, layer, bm=2048, bf=256):
    m, k = x.shape
    f = w_gate.shape[2]
    d_out = w_down.shape[2]
    bm, bf = min(bm, m), min(bf, f)
    assert m % bm == 0 and f % bf == 0 and w_down.shape[1] == f
    n_j = f // bf
    slab = f // ((m // bm) * n_j)
    assert slab * (m // bm) * n_j == f and slab % 16 == 0
    return pl.pallas_call(
        _gateup_kernel,
        grid=(m // bm, n_j),
        in_specs=[_x_spec(bm, k, True), _w_spec(k, bf, layer), _w_spec(k, bf, layer),
                  pl.BlockSpec((None, slab, d_out), lambda i, j: (layer, i * n_j + j, 0))],
        out_specs=[pl.BlockSpec((bm, bf), lambda i, j: (i, j)),
                   pl.BlockSpec((None, slab, d_out), lambda i, j: (0, i * n_j + j, 0))],
        out_shape=[jax.ShapeDtypeStruct((m, f), BF16), jax.ShapeDtypeStruct((1, f, d_out), BF16)],
        compiler_params=_cparams(2, VMEM_MIB),
        name="ffn_gateup",
    )(x, w_gate, w_up, w_down)


def ffn(u, w_gate, w_up, w_down, layer):
    hidden, w_down_bf16 = gateup(u, w_gate, w_up, w_down, layer)
    return mm(hidden, w_down_bf16, 0, BF16, bm=512, bn=512, name="ffn_down")


def _rope_tables(pos_ref, invf_ref, width):
    ang = pos_ref[...].astype(F32) * invf_ref[...]
    lane = lax.broadcasted_iota(jnp.int32, ang.shape, 1)
    first_half = (lane & (MLA_ROPE_DIM - 1)) < MLA_ROPE_DIM // 2
    valid = lane < width
    sin = jnp.sin(ang)
    cos_t = jnp.where(valid, jnp.cos(ang), 0.0)
    sin_t = jnp.where(valid, jnp.where(first_half, -sin, sin), 0.0)
    return cos_t, sin_t, first_half


def _rope_apply(t, cos_t, sin_t, first_half):
    half = MLA_ROPE_DIM // 2
    partner = jnp.where(first_half, pltpu.roll(t, LANES - half, 1), pltpu.roll(t, half, 1))
    return t * cos_t + partner * sin_t


def _inv_freq_lanes(width):
    half = MLA_ROPE_DIM // 2
    inv_freq = 1.0 / (ROPE_THETA ** (jnp.arange(half, dtype=F32) / half))
    return jnp.concatenate([jnp.tile(inv_freq, width // half), jnp.zeros((LANES - width,), F32)]).reshape(1, LANES)


def _qb_kernel(cq_ref, g_ref, wn_ref, wp_ref, pos_ref, invf_ref, o_ref):
    cq = jnp.concatenate([cq_ref[b] for b in range(cq_ref.shape[0])], axis=1)
    cn = _rms(cq.astype(F32), g_ref[...]).astype(BF16)
    qn = jnp.dot(cn, wn_ref[...], preferred_element_type=F32)
    qp = jnp.dot(cn, wp_ref[...], preferred_element_type=F32)
    cos_t, sin_t, first_half = _rope_tables(pos_ref, invf_ref, LANES)
    low = lax.broadcasted_iota(jnp.int32, cos_t.shape, 1) < MLA_ROPE_DIM
    for pair in range(MLA_HEADS // 2):
        r = _rope_apply(qp[:, pair * LANES:(pair + 1) * LANES], cos_t, sin_t, first_half)
        pe = (jnp.where(low, r, 0.0), jnp.where(low, pltpu.roll(r, LANES - MLA_ROPE_DIM, 1), 0.0))
        for k in range(2):
            h = 2 * pair + k
            c0 = h * MLA_QK_PAD
            o_ref[:, c0:c0 + MLA_NOPE_DIM] = qn[:, h * MLA_NOPE_DIM:(h + 1) * MLA_NOPE_DIM].astype(o_ref.dtype)
            o_ref[:, c0 + MLA_NOPE_DIM:c0 + MLA_QK_PAD] = pe[k].astype(o_ref.dtype)


def q_b_proj(qkv, g_q_a, w_q_b, pos, invf, bm=512):
    m = qkv.shape[1]
    bm = min(bm, m)
    h, dq = MLA_HEADS, MLA_NOPE_DIM + MLA_ROPE_DIM
    w = w_q_b.reshape(MLA_Q_RANK, h, dq) * (LOG2_E * float(dq) ** -0.5)
    w_nope = w[:, :, :MLA_NOPE_DIM].reshape(MLA_Q_RANK, h * MLA_NOPE_DIM).astype(BF16)
    w_pe = w[:, :, MLA_NOPE_DIM:].reshape(MLA_Q_RANK, h * MLA_ROPE_DIM).astype(BF16)
    cq_blocks = MLA_Q_RANK // IN_PROJ_BN
    cq_first = (3 * NA_WIDTH) // MLA_Q_RANK
    return pl.pallas_call(
        _qb_kernel,
        grid=(m // bm,),
        in_specs=[pl.BlockSpec((cq_blocks, bm, IN_PROJ_BN), lambda i: (cq_first, i, 0)),
                  pl.BlockSpec((1, MLA_Q_RANK), lambda i: (0, 0)),
                  pl.BlockSpec((MLA_Q_RANK, h * MLA_NOPE_DIM), lambda i: (0, 0)),
                  pl.BlockSpec((MLA_Q_RANK, h * MLA_ROPE_DIM), lambda i: (0, 0)),
                  pl.BlockSpec((bm, 1), lambda i: (i, 0)),
                  pl.BlockSpec((1, LANES), lambda i: (0, 0))],
        out_specs=pl.BlockSpec((bm, h * MLA_QK_PAD), lambda i: (i, 0)),
        out_shape=jax.ShapeDtypeStruct((m, h * MLA_QK_PAD), BF16),
        compiler_params=_cparams(1, VMEM_MIB),
        name="mla_q_proj",
    )(qkv, _row(g_q_a), w_nope, w_pe, pos, invf)


def _kvb_kernel(ckv_ref, g_ref, wk_ref, wvt_ref, u_ref, wt_ref, pos_ref, invf_ref, k_ref, vt_ref):
    cn = _rms(ckv_ref[...].astype(F32), g_ref[...]).astype(BF16)
    kn = jnp.dot(cn, wk_ref[...], preferred_element_type=F32)
    kr = jnp.dot(u_ref[...], wt_ref[...], preferred_element_type=F32)
    cos_t, sin_t, first_half = _rope_tables(pos_ref, invf_ref, MLA_ROPE_DIM)
    kpe = _rope_apply(kr, cos_t, sin_t, first_half).astype(k_ref.dtype)
    for h in range(MLA_HEADS):
        k_ref[:, h * MLA_QK_PAD:h * MLA_QK_PAD + MLA_NOPE_DIM] = (
            kn[:, h * MLA_NOPE_DIM:(h + 1) * MLA_NOPE_DIM].astype(k_ref.dtype))
        k_ref[:, h * MLA_QK_PAD + MLA_NOPE_DIM:(h + 1) * MLA_QK_PAD] = kpe
    vt = lax.dot_general(wvt_ref[...], cn, (((1,), (1,)), ((), ())), preferred_element_type=F32)
    ones = jnp.ones((MLA_VT_ROWS - MLA_V_DIM, vt.shape[1]), vt_ref.dtype)
    for h in range(MLA_HEADS):
        r0 = h * MLA_VT_ROWS
        vt_ref[r0:r0 + MLA_V_DIM, :] = vt[h * MLA_V_DIM:(h + 1) * MLA_V_DIM].astype(vt_ref.dtype)
        vt_ref[r0 + MLA_V_DIM:r0 + MLA_VT_ROWS, :] = ones


def kv_b_proj(qkv, g_kv_a, w_kv_b, u, w_rope_in, pos, invf, bm=512):
    m, d = u.shape
    bm = min(bm, m)
    h = MLA_HEADS
    assert MLA_KV_RANK == IN_PROJ_BN
    ckv_block = (3 * NA_WIDTH + MLA_Q_RANK) // IN_PROJ_BN
    wt = jnp.pad(w_rope_in, ((0, 0), (0, LANES - MLA_ROPE_DIM))).astype(BF16)
    w3 = w_kv_b.reshape(MLA_KV_RANK, h, MLA_NOPE_DIM + MLA_V_DIM)
    w_k = w3[:, :, :MLA_NOPE_DIM].reshape(MLA_KV_RANK, h * MLA_NOPE_DIM).astype(BF16)
    w_vt = w3[:, :, MLA_NOPE_DIM:].reshape(MLA_KV_RANK, h * MLA_V_DIM).T.astype(BF16)
    return pl.pallas_call(
        _kvb_kernel,
        grid=(m // bm,),
        in_specs=[pl.BlockSpec((None, bm, IN_PROJ_BN), lambda i: (ckv_block, i, 0)),
                  pl.BlockSpec((1, MLA_KV_RANK), lambda i: (0, 0)),
                  pl.BlockSpec((MLA_KV_RANK, h * MLA_NOPE_DIM), lambda i: (0, 0)),
                  pl.BlockSpec((h * MLA_V_DIM, MLA_KV_RANK), lambda i: (0, 0)),
                  pl.BlockSpec((bm, d), lambda i: (i, 0)),
                  pl.BlockSpec((d, LANES), lambda i: (0, 0)),
                  pl.BlockSpec((bm, 1), lambda i: (i, 0)),
                  pl.BlockSpec((1, LANES), lambda i: (0, 0))],
        out_specs=[pl.BlockSpec((bm, h * MLA_QK_PAD), lambda i: (i, 0)),
                   pl.BlockSpec((h * MLA_VT_ROWS, bm), lambda i: (0, i))],
        out_shape=[jax.ShapeDtypeStruct((m, h * MLA_QK_PAD), BF16),
                   jax.ShapeDtypeStruct((h * MLA_VT_ROWS, m), BF16)],
        compiler_params=_cparams(1, VMEM_MIB),
        name="mla_kv_proj",
    )(qkv, _row(g_kv_a), w_k, w_vt, u, wt, pos, invf)


def _mla_kernel(q_ref, qn_ref, k_ref, kn_ref, vt_ref, o_ref, s_scr, *, heads, bq):
    def scores(q_tile_ref, row0, keys_ref, slot, h):
        qk = slice(h * MLA_QK_PAD, (h + 1) * MLA_QK_PAD)
        s_scr[slot, h] = lax.dot_general(keys_ref[:, qk], q_tile_ref[row0:row0 + bq, qk],
                                         (((1,), (1,)), ((), ())), preferred_element_type=F32)

    def values(slot, row0, h):
        s = s_scr[slot, h]
        p = jnp.exp2(s - jnp.max(s, axis=0, keepdims=True)).astype(BF16)
        ot = jnp.dot(vt_ref[h * MLA_VT_ROWS:(h + 1) * MLA_VT_ROWS, :], p,
                     preferred_element_type=F32)
        o = ot[:MLA_V_DIM] / ot[MLA_V_DIM:MLA_V_DIM + 1]
        o_ref[row0:row0 + bq, h * MLA_V_DIM:(h + 1) * MLA_V_DIM] = o.T.astype(o_ref.dtype)

    first_step = (pl.program_id(0) == 0) & (pl.program_id(1) == 0) & (pl.program_id(2) == 0)

    @pl.when(first_step)
    def _():
        for h in range(heads):
            scores(q_ref, 0, k_ref, 0, h)

    for h in range(heads):
        scores(q_ref, bq, k_ref, 1, h)
        values(0, 0, h)
    for h in range(heads):
        scores(qn_ref, 0, kn_ref, 0, h)
        values(1, bq, h)


def mla_attention(q, k, vt, batch, seq, bq=512, heads_per_step=2):
    bq = min(bq, seq // 2)
    pairs = seq // (2 * bq)
    g = heads_per_step
    groups = MLA_HEADS // g
    n_steps = batch * groups * pairs

    def next_step(b, h, i):
        flat = jnp.minimum((b * groups + h) * pairs + i + 1, n_steps - 1)
        return flat // (groups * pairs), (flat // pairs) % groups, flat % pairs

    def q_next_map(b, h, i):
        nb, nh, ni = next_step(b, h, i)
        return (nb * pairs + ni) * 2, nh

    def k_next_map(b, h, i):
        nb, nh, _ = next_step(b, h, i)
        return nb, nh

    return pl.pallas_call(
        functools.partial(_mla_kernel, heads=g, bq=bq),
        grid=(batch, groups, pairs),
        in_specs=[pl.BlockSpec((2 * bq, g * MLA_QK_PAD), lambda b, h, i: (b * pairs + i, h)),
                  pl.BlockSpec((bq, g * MLA_QK_PAD), q_next_map),
                  pl.BlockSpec((seq, g * MLA_QK_PAD), lambda b, h, i: (b, h)),
                  pl.BlockSpec((seq, g * MLA_QK_PAD), k_next_map),
                  pl.BlockSpec((g * MLA_VT_ROWS, seq), lambda b, h, i: (h, b))],
        out_specs=pl.BlockSpec((2 * bq, g * MLA_V_DIM), lambda b, h, i: (b * pairs + i, h)),
        out_shape=jax.ShapeDtypeStruct((batch * seq, MLA_WIDTH), BF16),
        scratch_shapes=[pltpu.VMEM((2, g, seq, bq), F32)],
        compiler_params=pltpu.CompilerParams(dimension_semantics=("arbitrary",) * 3,
                                             vmem_limit_bytes=VMEM_MIB * MIB),
        name="mla_attention",
    )(q, q, k, k, vt)


NA_BIAS_ROWS = 2 * NA_KH - 1
NA_BIAS_COLS = 2 * NA_KW - 1
NA_BIAS_PAIRS = NA_BIAS_ROWS - 1


def _na_bias_kernel(rpb_ref, o_ref):
    row = lax.broadcasted_iota(jnp.int32, (GRID_W, LANES), 0)
    lane = lax.broadcasted_iota(jnp.int32, (GRID_W, LANES), 1)
    kc = lane & (GRID_W - 1)
    left = lane < GRID_W
    cstart = jnp.clip(row - NA_KW // 2, 0, GRID_W - NA_KW)
    in_window = (kc >= cstart) & (kc < cstart + NA_KW)
    toeplitz = []
    for d in range(NA_BIAS_ROWS):
        vec = jnp.broadcast_to(rpb_ref[d:d + 1, :], (GRID_W, LANES))
        toeplitz.append(pltpu.roll(vec, LANES - (NA_KW - 1), 1, stride=1, stride_axis=0))
    for d in range(NA_BIAS_PAIRS):
        tile = jnp.where(left, toeplitz[d], pltpu.roll(toeplitz[d + 1], GRID_W, 1))
        o_ref[0, d] = jnp.where(in_window, tile * LOG2_E, NEG_INF)


def na_bias_table(rpb):
    rpb_lanes = jnp.pad(rpb.astype(F32), ((0, 0), (0, 0), (0, LANES - NA_BIAS_COLS)))
    return pl.pallas_call(
        _na_bias_kernel,
        grid=(NA_HEADS,),
        in_specs=[pl.BlockSpec((None, NA_BIAS_ROWS, LANES), lambda h: (h, 0, 0))],
        out_specs=pl.BlockSpec((1, NA_BIAS_PAIRS, GRID_W, LANES), lambda h: (h, 0, 0, 0)),
        out_shape=jax.ShapeDtypeStruct((NA_HEADS, NA_BIAS_PAIRS, GRID_W, LANES), F32),
        compiler_params=_cparams(1, VMEM_SMALL_MIB),
        name="na_bias_table",
    )(rpb_lanes)


def _na_kernel(q_ref, k_ref, v_ref, bias_ref, o_ref, s_scr, *, rows, heads, rows_per_step):
    nk = NA_KH * GRID_W
    n_steps = rows // rows_per_step

    def row_geometry(step, rr):
        r = step * rows_per_step + rr
        rs = jnp.clip(r - NA_KH // 2, 0, rows - NA_KH)
        d0 = rs - r + (NA_KH - 1)
        return pl.multiple_of(r * GRID_W, GRID_W), pl.multiple_of(rs * GRID_W, GRID_W), d0

    def score_stage(step, slot):
        for rr in range(rows_per_step):
            q0, k0, d0 = row_geometry(step, rr)
            for h in range(heads):
                cols = slice(h * NA_HEAD_DIM, (h + 1) * NA_HEAD_DIM)
                q = q_ref[pl.ds(q0, GRID_W), cols]
                k = k_ref[pl.ds(k0, nk), cols]
                s = lax.dot_general(q, k, (((1,), (1,)), ((), ())), preferred_element_type=F32)
                bias = jnp.concatenate([bias_ref[h, d0 + 2 * p] for p in range(NA_KH // 2)], axis=1)
                s_scr[slot, rr * heads + h] = s + bias

    def value_stage(step, slot):
        for rr in range(rows_per_step):
            q0, k0, _ = row_geometry(step, rr)
            for h in range(heads):
                cols = slice(h * NA_HEAD_DIM, (h + 1) * NA_HEAD_DIM)
                s = s_scr[slot, rr * heads + h]
                p = jnp.exp2(s - jnp.max(s, axis=-1, keepdims=True))
                l = jnp.sum(p, axis=-1, keepdims=True)
                o = jnp.dot(p.astype(BF16), v_ref[pl.ds(k0, nk), cols], preferred_element_type=F32)
                o_ref[pl.ds(q0, GRID_W), cols] = (o / l).astype(o_ref.dtype)

    score_stage(0, 0)

    def body(u, carry):
        score_stage(2 * u + 1, 1)
        value_stage(2 * u, 0)
        score_stage(2 * u + 2, 0)
        value_stage(2 * u + 1, 1)
        return carry

    lax.fori_loop(0, n_steps // 2 - 1, body, 0)
    score_stage(n_steps - 1, 1)
    value_stage(n_steps - 2, 0)
    value_stage(n_steps - 1, 1)


def na_attention(qkv, bias, batch, seq, heads_per_step=4, rows_per_step=2):
    g = heads_per_step
    w = g * NA_HEAD_DIM
    groups = NA_HEADS // g
    rows = seq // GRID_W
    assert rows % rows_per_step == 0 and rows >= NA_KH and w == IN_PROJ_BN
    return pl.pallas_call(
        functools.partial(_na_kernel, rows=rows, heads=g, rows_per_step=rows_per_step),
        grid=(batch, groups),
        in_specs=[pl.BlockSpec((None, seq, w), lambda b, j: (j, b, 0)),
                  pl.BlockSpec((None, seq, w), lambda b, j: (groups + j, b, 0)),
                  pl.BlockSpec((None, seq, w), lambda b, j: (2 * groups + j, b, 0)),
                  pl.BlockSpec((g, NA_BIAS_PAIRS, GRID_W, LANES), lambda b, j: (j, 0, 0, 0))],
        out_specs=pl.BlockSpec((seq, w), lambda b, j: (b, j)),
        out_shape=jax.ShapeDtypeStruct((batch * seq, NA_WIDTH), BF16),
        scratch_shapes=[pltpu.VMEM((2, rows_per_step * g, GRID_W, NA_KH * GRID_W), F32)],
        compiler_params=_cparams(2, VMEM_MIB),
        name="na_attention",
    )(qkv, qkv, qkv, bias)


def _mem_block_kernel(h_ref, o_ref, g_mix_ref, g_pre_ref, wq_ref, kv_ref, wo_ref, g_post_ref, g_next_ref,
                      ho_ref, uo_ref, *, scale):
    h1 = h_ref[...] + _rms(o_ref[...].astype(F32), g_mix_ref[...])
    u = _rms(h1, g_pre_ref[...]).astype(BF16)
    q = (jnp.dot(u, wq_ref[...], preferred_element_type=F32) * scale).astype(BF16)
    outs = []
    for hd in range(MEM_HEADS):
        qh = q[:, hd * MEM_HEAD_DIM:(hd + 1) * MEM_HEAD_DIM]
        k = kv_ref[:, 2 * hd * MEM_HEAD_DIM:(2 * hd + 1) * MEM_HEAD_DIM]
        v = kv_ref[:, (2 * hd + 1) * MEM_HEAD_DIM:(2 * hd + 2) * MEM_HEAD_DIM]
        s = lax.dot_general(qh, k, (((1,), (1,)), ((), ())), preferred_element_type=F32)
        p = jnp.exp2(s - jnp.max(s, axis=-1, keepdims=True))
        l = jnp.sum(p, axis=-1, keepdims=True)
        outs.append((jnp.dot(p.astype(BF16), v, preferred_element_type=F32) / l).astype(BF16))
    a = jnp.dot(jnp.concatenate(outs, axis=1), wo_ref[...], preferred_element_type=F32)
    h2 = h1 + _rms(a, g_post_ref[...])
    ho_ref[...] = h2
    uo_ref[...] = _rms(h2, g_next_ref[...]).astype(uo_ref.dtype)


def mem_block(h, o, g_mix_post, g_pre, w_q, kv, w_o, g_post, g_next, batch, seq, bq=256):
    mem_len = kv.shape[0] // batch
    d = h.shape[1]
    bq = min(bq, seq)
    nq = seq // bq
    width = MEM_HEADS * MEM_HEAD_DIM
    row = pl.BlockSpec((bq, d), lambda b, i: (b * nq + i, 0))
    vec = pl.BlockSpec((1, d), lambda b, i: (0, 0))
    once = pl.Buffered(1)
    return pl.pallas_call(
        functools.partial(_mem_block_kernel, scale=LOG2_E * float(MEM_HEAD_DIM) ** -0.5),
        grid=(batch, nq),
        in_specs=[row, row, vec, vec,
                  pl.BlockSpec((d, width), lambda b, i: (0, 0), pipeline_mode=once),
                  pl.BlockSpec((mem_len, 2 * width), lambda b, i: (b, 0)),
                  pl.BlockSpec((width, d), lambda b, i: (0, 0), pipeline_mode=once),
                  vec, vec],
        out_specs=[row, row],
        out_shape=[jax.ShapeDtypeStruct((batch * seq, d), F32), jax.ShapeDtypeStruct((batch * seq, d), BF16)],
        compiler_params=_cparams(2, VMEM_MIB),
        name="mem_block",
    )(h, o, _row(g_mix_post), _row(g_pre), w_q.astype(BF16), kv, w_o.astype(BF16), _row(g_post), _row(g_next))


def kernel(x, mem, positions, ffn1_w_gate, ffn1_w_up, ffn1_w_down, g_ffn1, w_in, g_q_a, w_q_b, g_kv_a, w_kv_b, na_rpb, w_out, g_mix, g_mem_in, w_mem_q, w_mem_kv, w_mem_o, g_mem_attn, ffn2_w_gate, ffn2_w_up, ffn2_w_down, g_ffn2, g_final):
    batch, seq, d = x.shape
    m = batch * seq
    depth = ffn1_w_gate.shape[0]
    pos = positions.reshape(m, 1).astype(jnp.int32)
    invf_q, invf_k = _inv_freq_lanes(LANES), _inv_freq_lanes(MLA_ROPE_DIM)
    mem2 = mem.reshape(-1, d)

    h = x.reshape(m, d)
    u = norm_cast(h, g_ffn1[0, 0])
    out = None
    for l in range(depth):
        f = ffn(u, ffn1_w_gate, ffn1_w_up, ffn1_w_down, l)
        h, u = resid_norm(h, f, g_ffn1[l, 1], FFN_RES_WEIGHT, g_mix[l, 0])

        qkv = mm_nt(u, jnp.swapaxes(w_in, 1, 2), l, BF16, bm=2048, bn=IN_PROJ_BN, n_cols=IN_PROJ_MAIN,
                    scaled_cols=NA_WIDTH, scale=LOG2_E * float(NA_HEAD_DIM) ** -0.5,
                    single_buffer_x=True, name="in_proj")
        q_cat = q_b_proj(qkv, g_q_a[l], w_q_b[l], pos, invf_q)
        k_cat, v_mla = kv_b_proj(qkv, g_kv_a[l], w_kv_b[l], u, w_in[l, :, IN_PROJ_MAIN:], pos, invf_k)
        o_mla = mla_attention(q_cat, k_cat, v_mla, batch, seq)
        o_na = na_attention(qkv, na_bias_table(na_rpb[l]), batch, seq)
        o = mm2(o_na, o_mla, w_out, l, BF16, bm=2048, bn=512, single_buffer_x=True, name="out_proj")

        mem_n = norm_cast(mem2, g_mem_in[l])
        kv_mem = mm(mem_n, w_mem_kv, l, BF16, bm=1024, bn=512, name="mem_kv_proj")
        h, u = mem_block(h, o, g_mix[l, 1], g_mem_attn[l, 0], w_mem_q[l], kv_mem, w_mem_o[l],
                         g_mem_attn[l, 1], g_ffn2[l, 0], batch, seq)

        f = ffn(u, ffn2_w_gate, ffn2_w_up, ffn2_w_down, l)
        if l + 1 < depth:
            h = resid_final(h, f, g_ffn2[l, 1], FFN_RES_WEIGHT, g_final[l])
            u = norm_cast(h, g_ffn1[l + 1, 0])
        else:
            out = resid_final(h, f, g_ffn2[l, 1], FFN_RES_WEIGHT, g_final[l])
    return out.reshape(batch, seq, d)
```

```python
import functools

import jax
import jax.numpy as jnp
from jax import lax
from jax.experimental import pallas as pl
from jax.experimental.pallas import tpu as pltpu

F32 = jnp.float32
BF16 = jnp.bfloat16

GRID_W = 64
NA_HEADS = 16
NA_HEAD_DIM = 128
NA_KH = 8
NA_KW = 16
MLA_HEADS = 16
MLA_Q_RANK = 1024
MLA_KV_RANK = 512
MLA_NOPE_DIM = 128
MLA_ROPE_DIM = 64
MLA_V_DIM = 128
MLA_QK_PAD = 256
MLA_VT_ROWS = MLA_V_DIM + 16
LOG2_E = 1.4426950408889634
ROPE_THETA = 10000.0
MEM_HEADS = 4
MEM_HEAD_DIM = 128
FFN_RES_WEIGHT = 0.5
NORM_EPS = 1e-6
NEG_INF = -1e30
NA_WIDTH = NA_HEADS * NA_HEAD_DIM
MLA_WIDTH = MLA_HEADS * MLA_V_DIM
IN_PROJ_MAIN = 3 * NA_WIDTH + MLA_Q_RANK + MLA_KV_RANK
LANES = 128
MIB = 1024 * 1024
VMEM_SMALL_MIB = 32
VMEM_MIB = 56
VMEM_MAX_MIB = 60


def _cparams(n_axes, vmem_mib):
    return pltpu.CompilerParams(dimension_semantics=("parallel",) * n_axes,
                                vmem_limit_bytes=vmem_mib * MIB)


def _rms(x, g):
    return x * lax.rsqrt(jnp.mean(x * x, axis=-1, keepdims=True) + NORM_EPS) * g


def _row(v):
    return v.reshape(1, -1).astype(F32)


def _norm_cast_kernel(x_ref, g_ref, o_ref):
    o_ref[...] = _rms(x_ref[...], g_ref[...]).astype(o_ref.dtype)


def norm_cast(x, g, bm=512):
    m, d = x.shape
    bm = min(bm, m)
    return pl.pallas_call(
        _norm_cast_kernel,
        grid=(m // bm,),
        in_specs=[pl.BlockSpec((bm, d), lambda i: (i, 0)), pl.BlockSpec((1, d), lambda i: (0, 0))],
        out_specs=pl.BlockSpec((bm, d), lambda i: (i, 0)),
        out_shape=jax.ShapeDtypeStruct((m, d), BF16),
        compiler_params=_cparams(1, VMEM_SMALL_MIB),
        name="norm_cast",
    )(x, _row(g))


def _resid_norm_kernel(h_ref, f_ref, gp_ref, gn_ref, ho_ref, uo_ref, *, weight):
    h = h_ref[...] + weight * _rms(f_ref[...].astype(F32), gp_ref[...])
    ho_ref[...] = h
    uo_ref[...] = _rms(h, gn_ref[...]).astype(uo_ref.dtype)


def resid_norm(h, f, g_post, weight, g_next, bm=512):
    m, d = h.shape
    bm = min(bm, m)
    row = pl.BlockSpec((bm, d), lambda i: (i, 0))
    vec = pl.BlockSpec((1, d), lambda i: (0, 0))
    return pl.pallas_call(
        functools.partial(_resid_norm_kernel, weight=weight),
        grid=(m // bm,),
        in_specs=[row, row, vec, vec],
        out_specs=[row, row],
        out_shape=[jax.ShapeDtypeStruct((m, d), F32), jax.ShapeDtypeStruct((m, d), BF16)],
        compiler_params=_cparams(1, VMEM_MAX_MIB),
        name="resid_norm",
    )(h, f, _row(g_post), _row(g_next))


def _resid_final_kernel(h_ref, f_ref, gp_ref, gn_ref, o_ref, *, weight):
    h = h_ref[...] + weight * _rms(f_ref[...].astype(F32), gp_ref[...])
    o_ref[...] = _rms(h, gn_ref[...])


def resid_final(h, f, g_post, weight, g_final, bm=512):
    m, d = h.shape
    bm = min(bm, m)
    row = pl.BlockSpec((bm, d), lambda i: (i, 0))
    vec = pl.BlockSpec((1, d), lambda i: (0, 0))
    return pl.pallas_call(
        functools.partial(_resid_final_kernel, weight=weight),
        grid=(m // bm,),
        in_specs=[row, row, vec, vec],
        out_specs=row,
        out_shape=jax.ShapeDtypeStruct((m, d), F32),
        compiler_params=_cparams(1, VMEM_MIB),
        name="resid_final",
    )(h, f, _row(g_post), _row(g_final))


def _x_spec(bm, k):
    return pl.BlockSpec((bm, k), lambda i, j: (i, 0))


def _x_half_specs(bm, k):
    assert bm % 2 == 0
    return (pl.BlockSpec((bm // 2, k), lambda i, j: (2 * i, 0), pipeline_mode=pl.Buffered(1)),
            pl.BlockSpec((bm // 2, k), lambda i, j: (2 * i + 1, 0)))


def _w_spec(k, bn, layer):
    return pl.BlockSpec((None, k, bn), lambda i, j: (layer, 0, j))


def _mm_kernel(x_ref, w_ref, o_ref):
    o_ref[...] = jnp.dot(x_ref[...], w_ref[...].astype(BF16),
                         preferred_element_type=F32).astype(o_ref.dtype)


def mm(x, w, layer, out_dtype, bm, bn, n_cols=None, vmem_mib=VMEM_MIB, name="mm"):
    m, k = x.shape
    n = w.shape[2] if n_cols is None else n_cols
    bm, bn = min(bm, m), min(bn, n)
    assert m % bm == 0 and n % bn == 0 and w.shape[1] == k
    return pl.pallas_call(
        _mm_kernel,
        grid=(m // bm, n // bn),
        in_specs=[_x_spec(bm, k), _w_spec(k, bn, layer)],
        out_specs=pl.BlockSpec((bm, bn), lambda i, j: (i, j)),
        out_shape=jax.ShapeDtypeStruct((m, n), out_dtype),
        compiler_params=_cparams(2, vmem_mib),
        name=name,
    )(x, w)


def _mm_nt_kernel(*refs, scaled_blocks, scale):
    *x_refs, wt_ref, o_ref = refs
    wt = wt_ref[...].astype(BF16)
    r0 = 0
    for x_ref in x_refs:
        acc = lax.dot_general(x_ref[...], wt, (((1,), (1,)), ((), ())), preferred_element_type=F32)
        if scaled_blocks:
            acc = acc * jnp.where(pl.program_id(1) < scaled_blocks, scale, 1.0)
        o_ref[r0:r0 + x_ref.shape[0], :] = acc.astype(o_ref.dtype)
        r0 += x_ref.shape[0]


def mm_nt(x, wt, layer, out_dtype, bm, bn, n_cols=None, scaled_cols=0, scale=1.0, vmem_mib=VMEM_MIB,
          name="mm_nt"):
    m, k = x.shape
    n = wt.shape[1] if n_cols is None else n_cols
    bm, bn = min(bm, m), min(bn, n)
    assert m % bm == 0 and n % bn == 0 and wt.shape[2] == k and scaled_cols % bn == 0
    return pl.pallas_call(
        functools.partial(_mm_nt_kernel, scaled_blocks=scaled_cols // bn, scale=scale),
        grid=(m // bm, n // bn),
        in_specs=[*_x_half_specs(bm, k), pl.BlockSpec((None, bn, k), lambda i, j: (layer, j, 0))],
        out_specs=pl.BlockSpec((bm, bn), lambda i, j: (i, j)),
        out_shape=jax.ShapeDtypeStruct((m, n), out_dtype),
        compiler_params=_cparams(2, vmem_mib),
        name=name,
    )(x, x, wt)


def _mm2_kernel(xa_top_ref, xa_bot_ref, xb_top_ref, xb_bot_ref, w_ref, o_ref):
    ka = xa_top_ref.shape[1]
    w = w_ref[...].astype(BF16)
    r0 = 0
    for xa_ref, xb_ref in ((xa_top_ref, xb_top_ref), (xa_bot_ref, xb_bot_ref)):
        acc = jnp.dot(xa_ref[...], w[:ka], preferred_element_type=F32)
        acc = acc + jnp.dot(xb_ref[...], w[ka:], preferred_element_type=F32)
        o_ref[r0:r0 + xa_ref.shape[0], :] = acc.astype(o_ref.dtype)
        r0 += xa_ref.shape[0]


def mm2(xa, xb, w, layer, out_dtype, bm, bn, vmem_mib=VMEM_MIB, name="mm2"):
    m, ka = xa.shape
    kb = xb.shape[1]
    n = w.shape[2]
    bm, bn = min(bm, m), min(bn, n)
    assert m % bm == 0 and n % bn == 0 and w.shape[1] == ka + kb
    return pl.pallas_call(
        _mm2_kernel,
        grid=(m // bm, n // bn),
        in_specs=[*_x_half_specs(bm, ka), *_x_half_specs(bm, kb), _w_spec(ka + kb, bn, layer)],
        out_specs=pl.BlockSpec((bm, bn), lambda i, j: (i, j)),
        out_shape=jax.ShapeDtypeStruct((m, n), out_dtype),
        compiler_params=_cparams(2, vmem_mib),
        name=name,
    )(xa, xa, xb, xb, w)


def _gateup_kernel(xa_ref, xb_ref, wg_ref, wu_ref, wd_ref, o_ref, wdo_ref):
    wg = wg_ref[...].astype(BF16)
    wu = wu_ref[...].astype(BF16)
    half = xa_ref.shape[0]
    for x_ref, r0 in ((xa_ref, 0), (xb_ref, half)):
        x = x_ref[...]
        g = jnp.dot(x, wg, preferred_element_type=F32)
        u = jnp.dot(x, wu, preferred_element_type=F32)
        o_ref[r0:r0 + half, :] = (g * jax.nn.sigmoid(g) * u).astype(o_ref.dtype)
    wdo_ref[...] = wd_ref[...].astype(wdo_ref.dtype)


def gateup(x, w_gate, w_up, w_down, layer, bm=2048, bf=256):
    m, k = x.shape
    f = w_gate.shape[2]
    d_out = w_down.shape[2]
    bm, bf = min(bm, m), min(bf, f)
    assert m % bm == 0 and f % bf == 0 and w_down.shape[1] == f
    n_j = f // bf
    slab = f // ((m // bm) * n_j)
    assert slab * (m // bm) * n_j == f and slab % 16 == 0
    return pl.pallas_call(
        _gateup_kernel,
        grid=(m // bm, n_j),
        in_specs=[*_x_half_specs(bm, k), _w_spec(k, bf, layer), _w_spec(k, bf, layer),
                  pl.BlockSpec((None, slab, d_out), lambda i, j: (layer, i * n_j + j, 0))],
        out_specs=[pl.BlockSpec((bm, bf), lambda i, j: (i, j)),
                   pl.BlockSpec((None, slab, d_out), lambda i, j: (0, i * n_j + j, 0))],
        out_shape=[jax.ShapeDtypeStruct((m, f), BF16), jax.ShapeDtypeStruct((1, f, d_out), BF16)],
        compiler_params=_cparams(2, VMEM_MIB),
        name="ffn_gateup",
    )(x, x, w_gate, w_up, w_down)


def ffn(u, w_gate, w_up, w_down, layer):
    hidden, w_down_bf16 = gateup(u, w_gate, w_up, w_down, layer)
    return mm(hidden, w_down_bf16, 0, BF16, bm=512, bn=512, name="ffn_down")


def _rope_tables(pos_ref, invf_ref, width):
    ang = pos_ref[...].astype(F32) * invf_ref[...]
    lane = lax.broadcasted_iota(jnp.int32, ang.shape, 1)
    first_half = (lane & (MLA_ROPE_DIM - 1)) < MLA_ROPE_DIM // 2
    valid = lane < width
    sin = jnp.sin(ang)
    cos_t = jnp.where(valid, jnp.cos(ang), 0.0)
    sin_t = jnp.where(valid, jnp.where(first_half, -sin, sin), 0.0)
    return cos_t, sin_t, first_half


def _rope_apply(t, cos_t, sin_t, first_half):
    half = MLA_ROPE_DIM // 2
    partner = jnp.where(first_half, pltpu.roll(t, LANES - half, 1), pltpu.roll(t, half, 1))
    return t * cos_t + partner * sin_t


def _inv_freq_lanes(width):
    half = MLA_ROPE_DIM // 2
    inv_freq = 1.0 / (ROPE_THETA ** (jnp.arange(half, dtype=F32) / half))
    return jnp.concatenate([jnp.tile(inv_freq, width // half), jnp.zeros((LANES - width,), F32)]).reshape(1, LANES)


def _qb_kernel(cq_ref, g_ref, wn_ref, wp_ref, pos_ref, invf_ref, o_ref):
    cn = _rms(cq_ref[...].astype(F32), g_ref[...]).astype(BF16)
    qn = jnp.dot(cn, wn_ref[...], preferred_element_type=F32)
    qp = jnp.dot(cn, wp_ref[...], preferred_element_type=F32)
    cos_t, sin_t, first_half = _rope_tables(pos_ref, invf_ref, LANES)
    low = lax.broadcasted_iota(jnp.int32, cos_t.shape, 1) < MLA_ROPE_DIM
    for pair in range(MLA_HEADS // 2):
        r = _rope_apply(qp[:, pair * LANES:(pair + 1) * LANES], cos_t, sin_t, first_half)
        pe = (jnp.where(low, r, 0.0), jnp.where(low, pltpu.roll(r, LANES - MLA_ROPE_DIM, 1), 0.0))
        for k in range(2):
            h = 2 * pair + k
            c0 = h * MLA_QK_PAD
            o_ref[:, c0:c0 + MLA_NOPE_DIM] = qn[:, h * MLA_NOPE_DIM:(h + 1) * MLA_NOPE_DIM].astype(o_ref.dtype)
            o_ref[:, c0 + MLA_NOPE_DIM:c0 + MLA_QK_PAD] = pe[k].astype(o_ref.dtype)


def q_b_proj(qkv, g_q_a, w_q_b, pos, invf, bm=512):
    m = qkv.shape[0]
    bm = min(bm, m)
    h, dq = MLA_HEADS, MLA_NOPE_DIM + MLA_ROPE_DIM
    w = w_q_b.reshape(MLA_Q_RANK, h, dq) * (LOG2_E * float(dq) ** -0.5)
    w_nope = w[:, :, :MLA_NOPE_DIM].reshape(MLA_Q_RANK, h * MLA_NOPE_DIM).astype(BF16)
    w_pe = w[:, :, MLA_NOPE_DIM:].reshape(MLA_Q_RANK, h * MLA_ROPE_DIM).astype(BF16)
    cq_block = (3 * NA_WIDTH) // MLA_Q_RANK
    return pl.pallas_call(
        _qb_kernel,
        grid=(m // bm,),
        in_specs=[pl.BlockSpec((bm, MLA_Q_RANK), lambda i: (i, cq_block)),
                  pl.BlockSpec((1, MLA_Q_RANK), lambda i: (0, 0)),
                  pl.BlockSpec((MLA_Q_RANK, h * MLA_NOPE_DIM), lambda i: (0, 0)),
                  pl.BlockSpec((MLA_Q_RANK, h * MLA_ROPE_DIM), lambda i: (0, 0)),
                  pl.BlockSpec((bm, 1), lambda i: (i, 0)),
                  pl.BlockSpec((1, LANES), lambda i: (0, 0))],
        out_specs=pl.BlockSpec((bm, h * MLA_QK_PAD), lambda i: (i, 0)),
        out_shape=jax.ShapeDtypeStruct((m, h * MLA_QK_PAD), BF16),
        compiler_params=_cparams(1, VMEM_MIB),
        name="mla_q_proj",
    )(qkv, _row(g_q_a), w_nope, w_pe, pos, invf)


def _kvb_kernel(ckv_ref, g_ref, wk_ref, wvt_ref, u_ref, wt_ref, pos_ref, invf_ref, k_ref, vt_ref):
    cn = _rms(ckv_ref[...].astype(F32), g_ref[...]).astype(BF16)
    kn = jnp.dot(cn, wk_ref[...], preferred_element_type=F32)
    kr = jnp.dot(u_ref[...], wt_ref[...], preferred_element_type=F32)
    cos_t, sin_t, first_half = _rope_tables(pos_ref, invf_ref, MLA_ROPE_DIM)
    kpe = _rope_apply(kr, cos_t, sin_t, first_half).astype(k_ref.dtype)
    for h in range(MLA_HEADS):
        k_ref[:, h * MLA_QK_PAD:h * MLA_QK_PAD + MLA_NOPE_DIM] = (
            kn[:, h * MLA_NOPE_DIM:(h + 1) * MLA_NOPE_DIM].astype(k_ref.dtype))
        k_ref[:, h * MLA_QK_PAD + MLA_NOPE_DIM:(h + 1) * MLA_QK_PAD] = kpe
    vt = lax.dot_general(wvt_ref[...], cn, (((1,), (1,)), ((), ())), preferred_element_type=F32)
    ones = jnp.ones((MLA_VT_ROWS - MLA_V_DIM, vt.shape[1]), vt_ref.dtype)
    for h in range(MLA_HEADS):
        r0 = h * MLA_VT_ROWS
        vt_ref[r0:r0 + MLA_V_DIM, :] = vt[h * MLA_V_DIM:(h + 1) * MLA_V_DIM].astype(vt_ref.dtype)
        vt_ref[r0 + MLA_V_DIM:r0 + MLA_VT_ROWS, :] = ones


def kv_b_proj(qkv, g_kv_a, w_kv_b, u, w_rope_in, pos, invf, bm=512):
    m, d = u.shape
    bm = min(bm, m)
    h = MLA_HEADS
    ckv_block = (3 * NA_WIDTH + MLA_Q_RANK) // MLA_KV_RANK
    wt = jnp.pad(w_rope_in, ((0, 0), (0, LANES - MLA_ROPE_DIM))).astype(BF16)
    w3 = w_kv_b.reshape(MLA_KV_RANK, h, MLA_NOPE_DIM + MLA_V_DIM)
    w_k = w3[:, :, :MLA_NOPE_DIM].reshape(MLA_KV_RANK, h * MLA_NOPE_DIM).astype(BF16)
    w_vt = w3[:, :, MLA_NOPE_DIM:].reshape(MLA_KV_RANK, h * MLA_V_DIM).T.astype(BF16)
    return pl.pallas_call(
        _kvb_kernel,
        grid=(m // bm,),
        in_specs=[pl.BlockSpec((bm, MLA_KV_RANK), lambda i: (i, ckv_block)),
                  pl.BlockSpec((1, MLA_KV_RANK), lambda i: (0, 0)),
                  pl.BlockSpec((MLA_KV_RANK, h * MLA_NOPE_DIM), lambda i: (0, 0)),
                  pl.BlockSpec((h * MLA_V_DIM, MLA_KV_RANK), lambda i: (0, 0)),
                  pl.BlockSpec((bm, d), lambda i: (i, 0)),
                  pl.BlockSpec((d, LANES), lambda i: (0, 0)),
                  pl.BlockSpec((bm, 1), lambda i: (i, 0)),
                  pl.BlockSpec((1, LANES), lambda i: (0, 0))],
        out_specs=[pl.BlockSpec((bm, h * MLA_QK_PAD), lambda i: (i, 0)),
                   pl.BlockSpec((h * MLA_VT_ROWS, bm), lambda i: (0, i))],
        out_shape=[jax.ShapeDtypeStruct((m, h * MLA_QK_PAD), BF16),
                   jax.ShapeDtypeStruct((h * MLA_VT_ROWS, m), BF16)],
        compiler_params=_cparams(1, VMEM_MIB),
        name="mla_kv_proj",
    )(qkv, _row(g_kv_a), w_k, w_vt, u, wt, pos, invf)


def _mla_kernel(q_ref, qn_ref, k_ref, kn_ref, vt_ref, o_ref, s_scr, *, heads, bq):
    def scores(q_tile_ref, row0, keys_ref, slot, h):
        qk = slice(h * MLA_QK_PAD, (h + 1) * MLA_QK_PAD)
        s_scr[slot, h] = lax.dot_general(keys_ref[:, qk], q_tile_ref[row0:row0 + bq, qk],
                                         (((1,), (1,)), ((), ())), preferred_element_type=F32)

    def values(slot, row0, h):
        s = s_scr[slot, h]
        p = jnp.exp2(s - jnp.max(s, axis=0, keepdims=True)).astype(BF16)
        ot = jnp.dot(vt_ref[h * MLA_VT_ROWS:(h + 1) * MLA_VT_ROWS, :], p,
                     preferred_element_type=F32)
        o = ot[:MLA_V_DIM] / ot[MLA_V_DIM:MLA_V_DIM + 1]
        o_ref[row0:row0 + bq, h * MLA_V_DIM:(h + 1) * MLA_V_DIM] = o.T.astype(o_ref.dtype)

    first_step = (pl.program_id(0) == 0) & (pl.program_id(1) == 0) & (pl.program_id(2) == 0)

    @pl.when(first_step)
    def _():
        for h in range(heads):
            scores(q_ref, 0, k_ref, 0, h)

    for h in range(heads):
        scores(q_ref, bq, k_ref, 1, h)
        values(0, 0, h)
    for h in range(heads):
        scores(qn_ref, 0, kn_ref, 0, h)
        values(1, bq, h)


def mla_attention(q, k, vt, batch, seq, bq=512, heads_per_step=2):
    bq = min(bq, seq // 2)
    pairs = seq // (2 * bq)
    g = heads_per_step
    groups = MLA_HEADS // g
    n_steps = batch * groups * pairs

    def next_step(b, h, i):
        flat = jnp.minimum((b * groups + h) * pairs + i + 1, n_steps - 1)
        return flat // (groups * pairs), (flat // pairs) % groups, flat % pairs

    def q_next_map(b, h, i):
        nb, nh, ni = next_step(b, h, i)
        return (nb * pairs + ni) * 2, nh

    def k_next_map(b, h, i):
        nb, nh, _ = next_step(b, h, i)
        return nb, nh

    return pl.pallas_call(
        functools.partial(_mla_kernel, heads=g, bq=bq),
        grid=(batch, groups, pairs),
        in_specs=[pl.BlockSpec((2 * bq, g * MLA_QK_PAD), lambda b, h, i: (b * pairs + i, h)),
                  pl.BlockSpec((bq, g * MLA_QK_PAD), q_next_map),
                  pl.BlockSpec((seq, g * MLA_QK_PAD), lambda b, h, i: (b, h)),
                  pl.BlockSpec((seq, g * MLA_QK_PAD), k_next_map),
                  pl.BlockSpec((g * MLA_VT_ROWS, seq), lambda b, h, i: (h, b))],
        out_specs=pl.BlockSpec((2 * bq, g * MLA_V_DIM), lambda b, h, i: (b * pairs + i, h)),
        out_shape=jax.ShapeDtypeStruct((batch * seq, MLA_WIDTH), BF16),
        scratch_shapes=[pltpu.VMEM((2, g, seq, bq), F32)],
        compiler_params=pltpu.CompilerParams(dimension_semantics=("arbitrary",) * 3,
                                             vmem_limit_bytes=VMEM_MIB * MIB),
        name="mla_attention",
    )(q, q, k, k, vt)


NA_BIAS_ROWS = 2 * NA_KH - 1
NA_BIAS_COLS = 2 * NA_KW - 1
NA_BIAS_PAIRS = NA_BIAS_ROWS - 1


def _na_bias_kernel(rpb_ref, o_ref):
    row = lax.broadcasted_iota(jnp.int32, (GRID_W, LANES), 0)
    lane = lax.broadcasted_iota(jnp.int32, (GRID_W, LANES), 1)
    kc = lane & (GRID_W - 1)
    left = lane < GRID_W
    cstart = jnp.clip(row - NA_KW // 2, 0, GRID_W - NA_KW)
    in_window = (kc >= cstart) & (kc < cstart + NA_KW)
    toeplitz = []
    for d in range(NA_BIAS_ROWS):
        vec = jnp.broadcast_to(rpb_ref[d:d + 1, :], (GRID_W, LANES))
        toeplitz.append(pltpu.roll(vec, LANES - (NA_KW - 1), 1, stride=1, stride_axis=0))
    for d in range(NA_BIAS_PAIRS):
        tile = jnp.where(left, toeplitz[d], pltpu.roll(toeplitz[d + 1], GRID_W, 1))
        o_ref[0, d] = jnp.where(in_window, tile * LOG2_E, NEG_INF)


def na_bias_table(rpb):
    rpb_lanes = jnp.pad(rpb.astype(F32), ((0, 0), (0, 0), (0, LANES - NA_BIAS_COLS)))
    return pl.pallas_call(
        _na_bias_kernel,
        grid=(NA_HEADS,),
        in_specs=[pl.BlockSpec((None, NA_BIAS_ROWS, LANES), lambda h: (h, 0, 0))],
        out_specs=pl.BlockSpec((1, NA_BIAS_PAIRS, GRID_W, LANES), lambda h: (h, 0, 0, 0)),
        out_shape=jax.ShapeDtypeStruct((NA_HEADS, NA_BIAS_PAIRS, GRID_W, LANES), F32),
        compiler_params=_cparams(1, VMEM_SMALL_MIB),
        name="na_bias_table",
    )(rpb_lanes)


def _na_kernel(q_ref, k_ref, v_ref, bias_ref, o_ref, s_scr, *, rows, heads, rows_per_step):
    nk = NA_KH * GRID_W
    n_steps = rows // rows_per_step

    def row_geometry(step, rr):
        r = step * rows_per_step + rr
        rs = jnp.clip(r - NA_KH // 2, 0, rows - NA_KH)
        d0 = rs - r + (NA_KH - 1)
        return pl.multiple_of(r * GRID_W, GRID_W), pl.multiple_of(rs * GRID_W, GRID_W), d0

    def score_stage(step, slot):
        for rr in range(rows_per_step):
            q0, k0, d0 = row_geometry(step, rr)
            for h in range(heads):
                cols = slice(h * NA_HEAD_DIM, (h + 1) * NA_HEAD_DIM)
                q = q_ref[pl.ds(q0, GRID_W), cols]
                k = k_ref[pl.ds(k0, nk), cols]
                s = lax.dot_general(q, k, (((1,), (1,)), ((), ())), preferred_element_type=F32)
                bias = jnp.concatenate([bias_ref[h, d0 + 2 * p] for p in range(NA_KH // 2)], axis=1)
                s_scr[slot, rr * heads + h] = s + bias

    def value_stage(step, slot):
        for rr in range(rows_per_step):
            q0, k0, _ = row_geometry(step, rr)
            for h in range(heads):
                cols = slice(h * NA_HEAD_DIM, (h + 1) * NA_HEAD_DIM)
                s = s_scr[slot, rr * heads + h]
                p = jnp.exp2(s - jnp.max(s, axis=-1, keepdims=True))
                l = jnp.sum(p, axis=-1, keepdims=True)
                o = jnp.dot(p.astype(BF16), v_ref[pl.ds(k0, nk), cols], preferred_element_type=F32)
                o_ref[pl.ds(q0, GRID_W), cols] = (o / l).astype(o_ref.dtype)

    score_stage(0, 0)

    def body(u, carry):
        score_stage(2 * u + 1, 1)
        value_stage(2 * u, 0)
        score_stage(2 * u + 2, 0)
        value_stage(2 * u + 1, 1)
        return carry

    lax.fori_loop(0, n_steps // 2 - 1, body, 0)
    score_stage(n_steps - 1, 1)
    value_stage(n_steps - 2, 0)
    value_stage(n_steps - 1, 1)


def na_attention(qkv, bias, batch, seq, heads_per_step=4, rows_per_step=2):
    g = heads_per_step
    w = g * NA_HEAD_DIM
    groups = NA_HEADS // g
    rows = seq // GRID_W
    assert rows % rows_per_step == 0 and rows >= NA_KH
    return pl.pallas_call(
        functools.partial(_na_kernel, rows=rows, heads=g, rows_per_step=rows_per_step),
        grid=(batch, groups),
        in_specs=[pl.BlockSpec((seq, w), lambda b, j: (b, j)),
                  pl.BlockSpec((seq, w), lambda b, j: (b, groups + j)),
                  pl.BlockSpec((seq, w), lambda b, j: (b, 2 * groups + j)),
                  pl.BlockSpec((g, NA_BIAS_PAIRS, GRID_W, LANES), lambda b, j: (j, 0, 0, 0))],
        out_specs=pl.BlockSpec((seq, w), lambda b, j: (b, j)),
        out_shape=jax.ShapeDtypeStruct((batch * seq, NA_WIDTH), BF16),
        scratch_shapes=[pltpu.VMEM((2, rows_per_step * g, GRID_W, NA_KH * GRID_W), F32)],
        compiler_params=_cparams(2, VMEM_MIB),
        name="na_attention",
    )(qkv, qkv, qkv, bias)


def _mem_block_kernel(h_ref, o_ref, g_mix_ref, g_pre_ref, wq_ref, kv_ref, wo_ref, g_post_ref, g_next_ref,
                      ho_ref, uo_ref, *, scale):
    h1 = h_ref[...] + _rms(o_ref[...].astype(F32), g_mix_ref[...])
    u = _rms(h1, g_pre_ref[...]).astype(BF16)
    q = (jnp.dot(u, wq_ref[...], preferred_element_type=F32) * scale).astype(BF16)
    outs = []
    for hd in range(MEM_HEADS):
        qh = q[:, hd * MEM_HEAD_DIM:(hd + 1) * MEM_HEAD_DIM]
        k = kv_ref[:, 2 * hd * MEM_HEAD_DIM:(2 * hd + 1) * MEM_HEAD_DIM]
        v = kv_ref[:, (2 * hd + 1) * MEM_HEAD_DIM:(2 * hd + 2) * MEM_HEAD_DIM]
        s = lax.dot_general(qh, k, (((1,), (1,)), ((), ())), preferred_element_type=F32)
        p = jnp.exp2(s - jnp.max(s, axis=-1, keepdims=True))
        l = jnp.sum(p, axis=-1, keepdims=True)
        outs.append((jnp.dot(p.astype(BF16), v, preferred_element_type=F32) / l).astype(BF16))
    a = jnp.dot(jnp.concatenate(outs, axis=1), wo_ref[...], preferred_element_type=F32)
    h2 = h1 + _rms(a, g_post_ref[...])
    ho_ref[...] = h2
    uo_ref[...] = _rms(h2, g_next_ref[...]).astype(uo_ref.dtype)


def mem_block(h, o, g_mix_post, g_pre, w_q, kv, w_o, g_post, g_next, batch, seq, bq=256):
    mem_len = kv.shape[0] // batch
    d = h.shape[1]
    bq = min(bq, seq)
    nq = seq // bq
    width = MEM_HEADS * MEM_HEAD_DIM
    row = pl.BlockSpec((bq, d), lambda b, i: (b * nq + i, 0))
    vec = pl.BlockSpec((1, d), lambda b, i: (0, 0))
    once = pl.Buffered(1)
    return pl.pallas_call(
        functools.partial(_mem_block_kernel, scale=LOG2_E * float(MEM_HEAD_DIM) ** -0.5),
        grid=(batch, nq),
        in_specs=[row, row, vec, vec,
                  pl.BlockSpec((d, width), lambda b, i: (0, 0), pipeline_mode=once),
                  pl.BlockSpec((mem_len, 2 * width), lambda b, i: (b, 0)),
                  pl.BlockSpec((width, d), lambda b, i: (0, 0), pipeline_mode=once),
                  vec, vec],
        out_specs=[row, row],
        out_shape=[jax.ShapeDtypeStruct((batch * seq, d), F32), jax.ShapeDtypeStruct((batch * seq, d), BF16)],
        compiler_params=_cparams(2, VMEM_MIB),
        name="mem_block",
    )(h, o, _row(g_mix_post), _row(g_pre), w_q.astype(BF16), kv, w_o.astype(BF16), _row(g_post), _row(g_next))


def kernel(x, mem, positions, ffn1_w_gate, ffn1_w_up, ffn1_w_down, g_ffn1, w_in, g_q_a, w_q_b, g_kv_a, w_kv_b, na_rpb, w_out, g_mix, g_mem_in, w_mem_q, w_mem_kv, w_mem_o, g_mem_attn, ffn2_w_gate, ffn2_w_up, ffn2_w_down, g_ffn2, g_final):
    batch, seq, d = x.shape
    m = batch * seq
    depth = ffn1_w_gate.shape[0]
    pos = positions.reshape(m, 1).astype(jnp.int32)
    invf_q, invf_k = _inv_freq_lanes(LANES), _inv_freq_lanes(MLA_ROPE_DIM)
    mem2 = mem.reshape(-1, d)

    h = x.reshape(m, d)
    u = norm_cast(h, g_ffn1[0, 0])
    out = None
    for l in range(depth):
        f = ffn(u, ffn1_w_gate, ffn1_w_up, ffn1_w_down, l)
        h, u = resid_norm(h, f, g_ffn1[l, 1], FFN_RES_WEIGHT, g_mix[l, 0])

        qkv = mm_nt(u, jnp.swapaxes(w_in, 1, 2), l, BF16, bm=2048, bn=512, n_cols=IN_PROJ_MAIN,
                    scaled_cols=NA_WIDTH, scale=LOG2_E * float(NA_HEAD_DIM) ** -0.5, name="in_proj")
        q_cat = q_b_proj(qkv, g_q_a[l], w_q_b[l], pos, invf_q)
        k_cat, v_mla = kv_b_proj(qkv, g_kv_a[l], w_kv_b[l], u, w_in[l, :, IN_PROJ_MAIN:], pos, invf_k)
        o_mla = mla_attention(q_cat, k_cat, v_mla, batch, seq)
        o_na = na_attention(qkv, na_bias_table(na_rpb[l]), batch, seq)
        o = mm2(o_na, o_mla, w_out, l, BF16, bm=2048, bn=512, name="out_proj")

        mem_n = norm_cast(mem2, g_mem_in[l])
        kv_mem = mm(mem_n, w_mem_kv, l, BF16, bm=1024, bn=512, name="mem_kv_proj")
        h, u = mem_block(h, o, g_mix[l, 1], g_mem_attn[l, 0], w_mem_q[l], kv_mem, w_mem_o[l],
                         g_mem_attn[l, 1], g_ffn2[l, 0], batch, seq)

        f = ffn(u, ffn2_w_gate, ffn2_w_up, ffn2_w_down, l)
        if l + 1 < depth:
            h = resid_final(h, f, g_ffn2[l, 1], FFN_RES_WEIGHT, g_final[l])
            u = norm_cast(h, g_ffn1[l + 1, 0])
        else:
            out = resid_final(h, f, g_ffn2[l, 1], FFN_RES_WEIGHT, g_final[l])
    return out.reshape(batch, seq, d)
```

```python
import functools

import jax
import jax.numpy as jnp
from jax import lax
from jax.experimental import pallas as pl
from jax.experimental.pallas import tpu as pltpu

F32 = jnp.float32
BF16 = jnp.bfloat16

GRID_W = 64
NA_HEADS = 16
NA_HEAD_DIM = 128
NA_KH = 8
NA_KW = 16
MLA_HEADS = 16
MLA_Q_RANK = 1024
MLA_KV_RANK = 512
MLA_NOPE_DIM = 128
MLA_ROPE_DIM = 64
MLA_V_DIM = 128
MLA_QK_PAD = 256
MLA_VT_ROWS = MLA_V_DIM + 16
LOG2_E = 1.4426950408889634
ROPE_THETA = 10000.0
MEM_HEADS = 4
MEM_HEAD_DIM = 128
FFN_RES_WEIGHT = 0.5
NORM_EPS = 1e-6
NEG_INF = -1e30
NA_WIDTH = NA_HEADS * NA_HEAD_DIM
MLA_WIDTH = MLA_HEADS * MLA_V_DIM
IN_PROJ_MAIN = 3 * NA_WIDTH + MLA_Q_RANK + MLA_KV_RANK
LANES = 128
MIB = 1024 * 1024
VMEM_SMALL_MIB = 32
VMEM_MIB = 56
VMEM_MAX_MIB = 60


def _cparams(n_axes, vmem_mib):
    return pltpu.CompilerParams(dimension_semantics=("parallel",) * n_axes,
                                vmem_limit_bytes=vmem_mib * MIB)


def _rms(x, g):
    return x * lax.rsqrt(jnp.mean(x * x, axis=-1, keepdims=True) + NORM_EPS) * g


def _row(v):
    return v.reshape(1, -1).astype(F32)


def _norm_cast_kernel(x_ref, g_ref, o_ref):
    o_ref[...] = _rms(x_ref[...], g_ref[...]).astype(o_ref.dtype)


def norm_cast(x, g, bm=512):
    m, d = x.shape
    bm = min(bm, m)
    return pl.pallas_call(
        _norm_cast_kernel,
        grid=(m // bm,),
        in_specs=[pl.BlockSpec((bm, d), lambda i: (i, 0)), pl.BlockSpec((1, d), lambda i: (0, 0))],
        out_specs=pl.BlockSpec((bm, d), lambda i: (i, 0)),
        out_shape=jax.ShapeDtypeStruct((m, d), BF16),
        compiler_params=_cparams(1, VMEM_SMALL_MIB),
        name="norm_cast",
    )(x, _row(g))


def _resid_norm_kernel(h_ref, f_ref, gp_ref, gn_ref, ho_ref, uo_ref, *, weight):
    h = h_ref[...] + weight * _rms(f_ref[...].astype(F32), gp_ref[...])
    ho_ref[...] = h
    uo_ref[...] = _rms(h, gn_ref[...]).astype(uo_ref.dtype)


def resid_norm(h, f, g_post, weight, g_next, bm=512):
    m, d = h.shape
    bm = min(bm, m)
    row = pl.BlockSpec((bm, d), lambda i: (i, 0))
    vec = pl.BlockSpec((1, d), lambda i: (0, 0))
    return pl.pallas_call(
        functools.partial(_resid_norm_kernel, weight=weight),
        grid=(m // bm,),
        in_specs=[row, row, vec, vec],
        out_specs=[row, row],
        out_shape=[jax.ShapeDtypeStruct((m, d), F32), jax.ShapeDtypeStruct((m, d), BF16)],
        compiler_params=_cparams(1, VMEM_MAX_MIB),
        name="resid_norm",
    )(h, f, _row(g_post), _row(g_next))


def _resid_final_kernel(h_ref, f_ref, gp_ref, gn_ref, o_ref, *, weight):
    h = h_ref[...] + weight * _rms(f_ref[...].astype(F32), gp_ref[...])
    o_ref[...] = _rms(h, gn_ref[...])


def resid_final(h, f, g_post, weight, g_final, bm=512):
    m, d = h.shape
    bm = min(bm, m)
    row = pl.BlockSpec((bm, d), lambda i: (i, 0))
    vec = pl.BlockSpec((1, d), lambda i: (0, 0))
    return pl.pallas_call(
        functools.partial(_resid_final_kernel, weight=weight),
        grid=(m // bm,),
        in_specs=[row, row, vec, vec],
        out_specs=row,
        out_shape=jax.ShapeDtypeStruct((m, d), F32),
        compiler_params=_cparams(1, VMEM_MIB),
        name="resid_final",
    )(h, f, _row(g_post), _row(g_final))


def _x_spec(bm, k):
    return pl.BlockSpec((bm, k), lambda i, j: (i, 0))


def _x_half_specs(bm, k):
    assert bm % 2 == 0
    return (pl.BlockSpec((bm // 2, k), lambda i, j: (2 * i, 0), pipeline_mode=pl.Buffered(1)),
            pl.BlockSpec((bm // 2, k), lambda i, j: (2 * i + 1, 0)))


def _w_spec(k, bn, layer):
    return pl.BlockSpec((None, k, bn), lambda i, j: (layer, 0, j))


def _mm_kernel(x_ref, w_ref, o_ref):
    o_ref[...] = jnp.dot(x_ref[...], w_ref[...].astype(BF16),
                         preferred_element_type=F32).astype(o_ref.dtype)


def mm(x, w, layer, out_dtype, bm, bn, n_cols=None, vmem_mib=VMEM_MIB, name="mm"):
    m, k = x.shape
    n = w.shape[2] if n_cols is None else n_cols
    bm, bn = min(bm, m), min(bn, n)
    assert m % bm == 0 and n % bn == 0 and w.shape[1] == k
    return pl.pallas_call(
        _mm_kernel,
        grid=(m // bm, n // bn),
        in_specs=[_x_spec(bm, k), _w_spec(k, bn, layer)],
        out_specs=pl.BlockSpec((bm, bn), lambda i, j: (i, j)),
        out_shape=jax.ShapeDtypeStruct((m, n), out_dtype),
        compiler_params=_cparams(2, vmem_mib),
        name=name,
    )(x, w)


def _mm_nt_kernel(*refs, scaled_blocks, scale):
    *x_refs, wt_ref, o_ref = refs
    wt = wt_ref[...].astype(BF16)
    r0 = 0
    for x_ref in x_refs:
        acc = lax.dot_general(x_ref[...], wt, (((1,), (1,)), ((), ())), preferred_element_type=F32)
        if scaled_blocks:
            acc = acc * jnp.where(pl.program_id(1) < scaled_blocks, scale, 1.0)
        o_ref[r0:r0 + x_ref.shape[0], :] = acc.astype(o_ref.dtype)
        r0 += x_ref.shape[0]


def mm_nt(x, wt, layer, out_dtype, bm, bn, n_cols=None, scaled_cols=0, scale=1.0, vmem_mib=VMEM_MIB,
          name="mm_nt"):
    m, k = x.shape
    n = wt.shape[1] if n_cols is None else n_cols
    bm, bn = min(bm, m), min(bn, n)
    assert m % bm == 0 and n % bn == 0 and wt.shape[2] == k and scaled_cols % bn == 0
    return pl.pallas_call(
        functools.partial(_mm_nt_kernel, scaled_blocks=scaled_cols // bn, scale=scale),
        grid=(m // bm, n // bn),
        in_specs=[*_x_half_specs(bm, k), pl.BlockSpec((None, bn, k), lambda i, j: (layer, j, 0))],
        out_specs=pl.BlockSpec((bm, bn), lambda i, j: (i, j)),
        out_shape=jax.ShapeDtypeStruct((m, n), out_dtype),
        compiler_params=_cparams(2, vmem_mib),
        name=name,
    )(x, x, wt)


def _mm2_kernel(xa_top_ref, xa_bot_ref, xb_top_ref, xb_bot_ref, w_ref, o_ref):
    ka = xa_top_ref.shape[1]
    w = w_ref[...].astype(BF16)
    r0 = 0
    for xa_ref, xb_ref in ((xa_top_ref, xb_top_ref), (xa_bot_ref, xb_bot_ref)):
        acc = jnp.dot(xa_ref[...], w[:ka], preferred_element_type=F32)
        acc = acc + jnp.dot(xb_ref[...], w[ka:], preferred_element_type=F32)
        o_ref[r0:r0 + xa_ref.shape[0], :] = acc.astype(o_ref.dtype)
        r0 += xa_ref.shape[0]


def mm2(xa, xb, w, layer, out_dtype, bm, bn, vmem_mib=VMEM_MIB, name="mm2"):
    m, ka = xa.shape
    kb = xb.shape[1]
    n = w.shape[2]
    bm, bn = min(bm, m), min(bn, n)
    assert m % bm == 0 and n % bn == 0 and w.shape[1] == ka + kb
    return pl.pallas_call(
        _mm2_kernel,
        grid=(m // bm, n // bn),
        in_specs=[*_x_half_specs(bm, ka), *_x_half_specs(bm, kb), _w_spec(ka + kb, bn, layer)],
        out_specs=pl.BlockSpec((bm, bn), lambda i, j: (i, j)),
        out_shape=jax.ShapeDtypeStruct((m, n), out_dtype),
        compiler_params=_cparams(2, vmem_mib),
        name=name,
    )(xa, xa, xb, xb, w)


def _gateup_kernel(xa_ref, xb_ref, wg_ref, wu_ref, wd_ref, o_ref, wdo_ref):
    wg = wg_ref[...].astype(BF16)
    wu = wu_ref[...].astype(BF16)
    half = xa_ref.shape[0]
    quarter = half // 2
    for x_ref, r0 in ((xa_ref, 0), (xb_ref, half)):
        for q0 in (0, quarter):
            x = x_ref[q0:q0 + quarter, :]
            g = jnp.dot(x, wg, preferred_element_type=F32)
            u = jnp.dot(x, wu, preferred_element_type=F32)
            o_ref[r0 + q0:r0 + q0 + quarter, :] = (g * jax.nn.sigmoid(g) * u).astype(o_ref.dtype)
    wdo_ref[...] = wd_ref[...].astype(wdo_ref.dtype)


def gateup(x, w_gate, w_up, w_down, layer, bm=2048, bf=256):
    m, k = x.shape
    f = w_gate.shape[2]
    d_out = w_down.shape[2]
    bm, bf = min(bm, m), min(bf, f)
    assert m % bm == 0 and f % bf == 0 and w_down.shape[1] == f
    n_j = f // bf
    slab = f // ((m // bm) * n_j)
    assert slab * (m // bm) * n_j == f and slab % 16 == 0
    return pl.pallas_call(
        _gateup_kernel,
        grid=(m // bm, n_j),
        in_specs=[*_x_half_specs(bm, k), _w_spec(k, bf, layer), _w_spec(k, bf, layer),
                  pl.BlockSpec((None, slab, d_out), lambda i, j: (layer, i * n_j + j, 0))],
        out_specs=[pl.BlockSpec((bm, bf), lambda i, j: (i, j)),
                   pl.BlockSpec((None, slab, d_out), lambda i, j: (0, i * n_j + j, 0))],
        out_shape=[jax.ShapeDtypeStruct((m, f), BF16), jax.ShapeDtypeStruct((1, f, d_out), BF16)],
        compiler_params=_cparams(2, VMEM_MIB),
        name="ffn_gateup",
    )(x, x, w_gate, w_up, w_down)


def ffn(u, w_gate, w_up, w_down, layer):
    hidden, w_down_bf16 = gateup(u, w_gate, w_up, w_down, layer)
    return mm(hidden, w_down_bf16, 0, BF16, bm=512, bn=512, name="ffn_down")


def _rope_tables(pos_ref, invf_ref, width):
    ang = pos_ref[...].astype(F32) * invf_ref[...]
    lane = lax.broadcasted_iota(jnp.int32, ang.shape, 1)
    first_half = (lane & (MLA_ROPE_DIM - 1)) < MLA_ROPE_DIM // 2
    valid = lane < width
    sin = jnp.sin(ang)
    cos_t = jnp.where(valid, jnp.cos(ang), 0.0)
    sin_t = jnp.where(valid, jnp.where(first_half, -sin, sin), 0.0)
    return cos_t, sin_t, first_half


def _rope_apply(t, cos_t, sin_t, first_half):
    half = MLA_ROPE_DIM // 2
    partner = jnp.where(first_half, pltpu.roll(t, LANES - half, 1), pltpu.roll(t, half, 1))
    return t * cos_t + partner * sin_t


def _inv_freq_lanes(width):
    half = MLA_ROPE_DIM // 2
    inv_freq = 1.0 / (ROPE_THETA ** (jnp.arange(half, dtype=F32) / half))
    return jnp.concatenate([jnp.tile(inv_freq, width // half), jnp.zeros((LANES - width,), F32)]).reshape(1, LANES)


def _qb_kernel(cq_ref, g_ref, wn_ref, wp_ref, pos_ref, invf_ref, o_ref):
    cn = _rms(cq_ref[...].astype(F32), g_ref[...]).astype(BF16)
    qn = jnp.dot(cn, wn_ref[...], preferred_element_type=F32)
    qp = jnp.dot(cn, wp_ref[...], preferred_element_type=F32)
    cos_t, sin_t, first_half = _rope_tables(pos_ref, invf_ref, LANES)
    low = lax.broadcasted_iota(jnp.int32, cos_t.shape, 1) < MLA_ROPE_DIM
    for pair in range(MLA_HEADS // 2):
        r = _rope_apply(qp[:, pair * LANES:(pair + 1) * LANES], cos_t, sin_t, first_half)
        pe = (jnp.where(low, r, 0.0), jnp.where(low, pltpu.roll(r, LANES - MLA_ROPE_DIM, 1), 0.0))
        for k in range(2):
            h = 2 * pair + k
            c0 = h * MLA_QK_PAD
            o_ref[:, c0:c0 + MLA_NOPE_DIM] = qn[:, h * MLA_NOPE_DIM:(h + 1) * MLA_NOPE_DIM].astype(o_ref.dtype)
            o_ref[:, c0 + MLA_NOPE_DIM:c0 + MLA_QK_PAD] = pe[k].astype(o_ref.dtype)


def q_b_proj(qkv, g_q_a, w_q_b, pos, invf, bm=512):
    m = qkv.shape[0]
    bm = min(bm, m)
    h, dq = MLA_HEADS, MLA_NOPE_DIM + MLA_ROPE_DIM
    w = w_q_b.reshape(MLA_Q_RANK, h, dq) * (LOG2_E * float(dq) ** -0.5)
    w_nope = w[:, :, :MLA_NOPE_DIM].reshape(MLA_Q_RANK, h * MLA_NOPE_DIM).astype(BF16)
    w_pe = w[:, :, MLA_NOPE_DIM:].reshape(MLA_Q_RANK, h * MLA_ROPE_DIM).astype(BF16)
    cq_block = (3 * NA_WIDTH) // MLA_Q_RANK
    return pl.pallas_call(
        _qb_kernel,
        grid=(m // bm,),
        in_specs=[pl.BlockSpec((bm, MLA_Q_RANK), lambda i: (i, cq_block)),
                  pl.BlockSpec((1, MLA_Q_RANK), lambda i: (0, 0)),
                  pl.BlockSpec((MLA_Q_RANK, h * MLA_NOPE_DIM), lambda i: (0, 0)),
                  pl.BlockSpec((MLA_Q_RANK, h * MLA_ROPE_DIM), lambda i: (0, 0)),
                  pl.BlockSpec((bm, 1), lambda i: (i, 0)),
                  pl.BlockSpec((1, LANES), lambda i: (0, 0))],
        out_specs=pl.BlockSpec((bm, h * MLA_QK_PAD), lambda i: (i, 0)),
        out_shape=jax.ShapeDtypeStruct((m, h * MLA_QK_PAD), BF16),
        compiler_params=_cparams(1, VMEM_MIB),
        name="mla_q_proj",
    )(qkv, _row(g_q_a), w_nope, w_pe, pos, invf)


def _kvb_kernel(ckv_ref, g_ref, wk_ref, wvt_ref, u_ref, wt_ref, pos_ref, invf_ref, k_ref, vt_ref):
    cn = _rms(ckv_ref[...].astype(F32), g_ref[...]).astype(BF16)
    kn = jnp.dot(cn, wk_ref[...], preferred_element_type=F32)
    kr = jnp.dot(u_ref[...], wt_ref[...], preferred_element_type=F32)
    cos_t, sin_t, first_half = _rope_tables(pos_ref, invf_ref, MLA_ROPE_DIM)
    kpe = _rope_apply(kr, cos_t, sin_t, first_half).astype(k_ref.dtype)
    for h in range(MLA_HEADS):
        k_ref[:, h * MLA_QK_PAD:h * MLA_QK_PAD + MLA_NOPE_DIM] = (
            kn[:, h * MLA_NOPE_DIM:(h + 1) * MLA_NOPE_DIM].astype(k_ref.dtype))
        k_ref[:, h * MLA_QK_PAD + MLA_NOPE_DIM:(h + 1) * MLA_QK_PAD] = kpe
    vt = lax.dot_general(wvt_ref[...], cn, (((1,), (1,)), ((), ())), preferred_element_type=F32)
    ones = jnp.ones((MLA_VT_ROWS - MLA_V_DIM, vt.shape[1]), vt_ref.dtype)
    for h in range(MLA_HEADS):
        r0 = h * MLA_VT_ROWS
        vt_ref[r0:r0 + MLA_V_DIM, :] = vt[h * MLA_V_DIM:(h + 1) * MLA_V_DIM].astype(vt_ref.dtype)
        vt_ref[r0 + MLA_V_DIM:r0 + MLA_VT_ROWS, :] = ones


def kv_b_proj(qkv, g_kv_a, w_kv_b, u, w_rope_in, pos, invf, bm=512):
    m, d = u.shape
    bm = min(bm, m)
    h = MLA_HEADS
    ckv_block = (3 * NA_WIDTH + MLA_Q_RANK) // MLA_KV_RANK
    wt = jnp.pad(w_rope_in, ((0, 0), (0, LANES - MLA_ROPE_DIM))).astype(BF16)
    w3 = w_kv_b.reshape(MLA_KV_RANK, h, MLA_NOPE_DIM + MLA_V_DIM)
    w_k = w3[:, :, :MLA_NOPE_DIM].reshape(MLA_KV_RANK, h * MLA_NOPE_DIM).astype(BF16)
    w_vt = w3[:, :, MLA_NOPE_DIM:].reshape(MLA_KV_RANK, h * MLA_V_DIM).T.astype(BF16)
    return pl.pallas_call(
        _kvb_kernel,
        grid=(m // bm,),
        in_specs=[pl.BlockSpec((bm, MLA_KV_RANK), lambda i: (i, ckv_block)),
                  pl.BlockSpec((1, MLA_KV_RANK), lambda i: (0, 0)),
                  pl.BlockSpec((MLA_KV_RANK, h * MLA_NOPE_DIM), lambda i: (0, 0)),
                  pl.BlockSpec((h * MLA_V_DIM, MLA_KV_RANK), lambda i: (0, 0)),
                  pl.BlockSpec((bm, d), lambda i: (i, 0)),
                  pl.BlockSpec((d, LANES), lambda i: (0, 0)),
                  pl.BlockSpec((bm, 1), lambda i: (i, 0)),
                  pl.BlockSpec((1, LANES), lambda i: (0, 0))],
        out_specs=[pl.BlockSpec((bm, h * MLA_QK_PAD), lambda i: (i, 0)),
                   pl.BlockSpec((h * MLA_VT_ROWS, bm), lambda i: (0, i))],
        out_shape=[jax.ShapeDtypeStruct((m, h * MLA_QK_PAD), BF16),
                   jax.ShapeDtypeStruct((h * MLA_VT_ROWS, m), BF16)],
        compiler_params=_cparams(1, VMEM_MIB),
        name="mla_kv_proj",
    )(qkv, _row(g_kv_a), w_k, w_vt, u, wt, pos, invf)


def _mla_kernel(q_ref, qn_ref, k_ref, kn_ref, vt_ref, o_ref, s_scr, *, heads, bq):
    def scores(q_tile_ref, row0, keys_ref, slot, h):
        qk = slice(h * MLA_QK_PAD, (h + 1) * MLA_QK_PAD)
        s_scr[slot, h] = lax.dot_general(keys_ref[:, qk], q_tile_ref[row0:row0 + bq, qk],
                                         (((1,), (1,)), ((), ())), preferred_element_type=F32)

    def values(slot, row0, h):
        s = s_scr[slot, h]
        p = jnp.exp2(s - jnp.max(s, axis=0, keepdims=True)).astype(BF16)
        ot = jnp.dot(vt_ref[h * MLA_VT_ROWS:(h + 1) * MLA_VT_ROWS, :], p,
                     preferred_element_type=F32)
        o = ot[:MLA_V_DIM] / ot[MLA_V_DIM:MLA_V_DIM + 1]
        o_ref[row0:row0 + bq, h * MLA_V_DIM:(h + 1) * MLA_V_DIM] = o.T.astype(o_ref.dtype)

    first_step = (pl.program_id(0) == 0) & (pl.program_id(1) == 0) & (pl.program_id(2) == 0)

    @pl.when(first_step)
    def _():
        for h in range(heads):
            scores(q_ref, 0, k_ref, 0, h)

    for h in range(heads):
        scores(q_ref, bq, k_ref, 1, h)
        values(0, 0, h)
    for h in range(heads):
        scores(qn_ref, 0, kn_ref, 0, h)
        values(1, bq, h)


def mla_attention(q, k, vt, batch, seq, bq=512, heads_per_step=2):
    bq = min(bq, seq // 2)
    pairs = seq // (2 * bq)
    g = heads_per_step
    groups = MLA_HEADS // g
    n_steps = batch * groups * pairs

    def next_step(b, h, i):
        flat = jnp.minimum((b * groups + h) * pairs + i + 1, n_steps - 1)
        return flat // (groups * pairs), (flat // pairs) % groups, flat % pairs

    def q_next_map(b, h, i):
        nb, nh, ni = next_step(b, h, i)
        return (nb * pairs + ni) * 2, nh

    def k_next_map(b, h, i):
        nb, nh, _ = next_step(b, h, i)
        return nb, nh

    return pl.pallas_call(
        functools.partial(_mla_kernel, heads=g, bq=bq),
        grid=(batch, groups, pairs),
        in_specs=[pl.BlockSpec((2 * bq, g * MLA_QK_PAD), lambda b, h, i: (b * pairs + i, h)),
                  pl.BlockSpec((bq, g * MLA_QK_PAD), q_next_map),
                  pl.BlockSpec((seq, g * MLA_QK_PAD), lambda b, h, i: (b, h)),
                  pl.BlockSpec((seq, g * MLA_QK_PAD), k_next_map),
                  pl.BlockSpec((g * MLA_VT_ROWS, seq), lambda b, h, i: (h, b))],
        out_specs=pl.BlockSpec((2 * bq, g * MLA_V_DIM), lambda b, h, i: (b * pairs + i, h)),
        out_shape=jax.ShapeDtypeStruct((batch * seq, MLA_WIDTH), BF16),
        scratch_shapes=[pltpu.VMEM((2, g, seq, bq), F32)],
        compiler_params=pltpu.CompilerParams(dimension_semantics=("arbitrary",) * 3,
                                             vmem_limit_bytes=VMEM_MIB * MIB),
        name="mla_attention",
    )(q, q, k, k, vt)


NA_BIAS_ROWS = 2 * NA_KH - 1
NA_BIAS_COLS = 2 * NA_KW - 1
NA_BIAS_PAIRS = NA_BIAS_ROWS - 1


def _na_bias_kernel(rpb_ref, o_ref):
    row = lax.broadcasted_iota(jnp.int32, (GRID_W, LANES), 0)
    lane = lax.broadcasted_iota(jnp.int32, (GRID_W, LANES), 1)
    kc = lane & (GRID_W - 1)
    left = lane < GRID_W
    cstart = jnp.clip(row - NA_KW // 2, 0, GRID_W - NA_KW)
    in_window = (kc >= cstart) & (kc < cstart + NA_KW)
    toeplitz = []
    for d in range(NA_BIAS_ROWS):
        vec = jnp.broadcast_to(rpb_ref[d:d + 1, :], (GRID_W, LANES))
        toeplitz.append(pltpu.roll(vec, LANES - (NA_KW - 1), 1, stride=1, stride_axis=0))
    for d in range(NA_BIAS_PAIRS):
        tile = jnp.where(left, toeplitz[d], pltpu.roll(toeplitz[d + 1], GRID_W, 1))
        o_ref[0, d] = jnp.where(in_window, tile * LOG2_E, NEG_INF)


def na_bias_table(rpb):
    rpb_lanes = jnp.pad(rpb.astype(F32), ((0, 0), (0, 0), (0, LANES - NA_BIAS_COLS)))
    return pl.pallas_call(
        _na_bias_kernel,
        grid=(NA_HEADS,),
        in_specs=[pl.BlockSpec((None, NA_BIAS_ROWS, LANES), lambda h: (h, 0, 0))],
        out_specs=pl.BlockSpec((1, NA_BIAS_PAIRS, GRID_W, LANES), lambda h: (h, 0, 0, 0)),
        out_shape=jax.ShapeDtypeStruct((NA_HEADS, NA_BIAS_PAIRS, GRID_W, LANES), F32),
        compiler_params=_cparams(1, VMEM_SMALL_MIB),
        name="na_bias_table",
    )(rpb_lanes)


def _na_kernel(q_ref, k_ref, v_ref, bias_ref, o_ref, s_scr, *, rows, heads, rows_per_step):
    nk = NA_KH * GRID_W
    n_steps = rows // rows_per_step

    def row_geometry(step, rr):
        r = step * rows_per_step + rr
        rs = jnp.clip(r - NA_KH // 2, 0, rows - NA_KH)
        d0 = rs - r + (NA_KH - 1)
        return pl.multiple_of(r * GRID_W, GRID_W), pl.multiple_of(rs * GRID_W, GRID_W), d0

    def score_stage(step, slot):
        for rr in range(rows_per_step):
            q0, k0, d0 = row_geometry(step, rr)
            for h in range(heads):
                cols = slice(h * NA_HEAD_DIM, (h + 1) * NA_HEAD_DIM)
                q = q_ref[pl.ds(q0, GRID_W), cols]
                k = k_ref[pl.ds(k0, nk), cols]
                s = lax.dot_general(q, k, (((1,), (1,)), ((), ())), preferred_element_type=F32)
                bias = jnp.concatenate([bias_ref[h, d0 + 2 * p] for p in range(NA_KH // 2)], axis=1)
                s_scr[slot, rr * heads + h] = s + bias

    def value_stage(step, slot):
        for rr in range(rows_per_step):
            q0, k0, _ = row_geometry(step, rr)
            for h in range(heads):
                cols = slice(h * NA_HEAD_DIM, (h + 1) * NA_HEAD_DIM)
                s = s_scr[slot, rr * heads + h]
                p = jnp.exp2(s - jnp.max(s, axis=-1, keepdims=True))
                l = jnp.sum(p, axis=-1, keepdims=True)
                o = jnp.dot(p.astype(BF16), v_ref[pl.ds(k0, nk), cols], preferred_element_type=F32)
                o_ref[pl.ds(q0, GRID_W), cols] = (o / l).astype(o_ref.dtype)

    score_stage(0, 0)

    def body(u, carry):
        score_stage(2 * u + 1, 1)
        value_stage(2 * u, 0)
        score_stage(2 * u + 2, 0)
        value_stage(2 * u + 1, 1)
        return carry

    lax.fori_loop(0, n_steps // 2 - 1, body, 0)
    score_stage(n_steps - 1, 1)
    value_stage(n_steps - 2, 0)
    value_stage(n_steps - 1, 1)


def na_attention(qkv, bias, batch, seq, heads_per_step=4, rows_per_step=2):
    g = heads_per_step
    w = g * NA_HEAD_DIM
    groups = NA_HEADS // g
    rows = seq // GRID_W
    assert rows % rows_per_step == 0 and rows >= NA_KH
    return pl.pallas_call(
        functools.partial(_na_kernel, rows=rows, heads=g, rows_per_step=rows_per_step),
        grid=(batch, groups),
        in_specs=[pl.BlockSpec((seq, w), lambda b, j: (b, j)),
                  pl.BlockSpec((seq, w), lambda b, j: (b, groups + j)),
                  pl.BlockSpec((seq, w), lambda b, j: (b, 2 * groups + j)),
                  pl.BlockSpec((g, NA_BIAS_PAIRS, GRID_W, LANES), lambda b, j: (j, 0, 0, 0))],
        out_specs=pl.BlockSpec((seq, w), lambda b, j: (b, j)),
        out_shape=jax.ShapeDtypeStruct((batch * seq, NA_WIDTH), BF16),
        scratch_shapes=[pltpu.VMEM((2, rows_per_step * g, GRID_W, NA_KH * GRID_W), F32)],
        compiler_params=_cparams(2, VMEM_MIB),
        name="na_attention",
    )(qkv, qkv, qkv, bias)


def _mem_block_kernel(h_ref, o_ref, g_mix_ref, g_pre_ref, wq_ref, kv_ref, wo_ref, g_post_ref, g_next_ref,
                      ho_ref, uo_ref, *, scale):
    h1 = h_ref[...] + _rms(o_ref[...].astype(F32), g_mix_ref[...])
    u = _rms(h1, g_pre_ref[...]).astype(BF16)
    q = (jnp.dot(u, wq_ref[...], preferred_element_type=F32) * scale).astype(BF16)
    outs = []
    for hd in range(MEM_HEADS):
        qh = q[:, hd * MEM_HEAD_DIM:(hd + 1) * MEM_HEAD_DIM]
        k = kv_ref[:, 2 * hd * MEM_HEAD_DIM:(2 * hd + 1) * MEM_HEAD_DIM]
        v = kv_ref[:, (2 * hd + 1) * MEM_HEAD_DIM:(2 * hd + 2) * MEM_HEAD_DIM]
        s = lax.dot_general(qh, k, (((1,), (1,)), ((), ())), preferred_element_type=F32)
        p = jnp.exp2(s - jnp.max(s, axis=-1, keepdims=True))
        l = jnp.sum(p, axis=-1, keepdims=True)
        outs.append((jnp.dot(p.astype(BF16), v, preferred_element_type=F32) / l).astype(BF16))
    a = jnp.dot(jnp.concatenate(outs, axis=1), wo_ref[...], preferred_element_type=F32)
    h2 = h1 + _rms(a, g_post_ref[...])
    ho_ref[...] = h2
    uo_ref[...] = _rms(h2, g_next_ref[...]).astype(uo_ref.dtype)


def mem_block(h, o, g_mix_post, g_pre, w_q, kv, w_o, g_post, g_next, batch, seq, bq=256):
    mem_len = kv.shape[0] // batch
    d = h.shape[1]
    bq = min(bq, seq)
    nq = seq // bq
    width = MEM_HEADS * MEM_HEAD_DIM
    row = pl.BlockSpec((bq, d), lambda b, i: (b * nq + i, 0))
    vec = pl.BlockSpec((1, d), lambda b, i: (0, 0))
    once = pl.Buffered(1)
    return pl.pallas_call(
        functools.partial(_mem_block_kernel, scale=LOG2_E * float(MEM_HEAD_DIM) ** -0.5),
        grid=(batch, nq),
        in_specs=[row, row, vec, vec,
                  pl.BlockSpec((d, width), lambda b, i: (0, 0), pipeline_mode=once),
                  pl.BlockSpec((mem_len, 2 * width), lambda b, i: (b, 0)),
                  pl.BlockSpec((width, d), lambda b, i: (0, 0), pipeline_mode=once),
                  vec, vec],
        out_specs=[row, row],
        out_shape=[jax.ShapeDtypeStruct((batch * seq, d), F32), jax.ShapeDtypeStruct((batch * seq, d), BF16)],
        compiler_params=_cparams(2, VMEM_MIB),
        name="mem_block",
    )(h, o, _row(g_mix_post), _row(g_pre), w_q.astype(BF16), kv, w_o.astype(BF16), _row(g_post), _row(g_next))


def kernel(x, mem, positions, ffn1_w_gate, ffn1_w_up, ffn1_w_down, g_ffn1, w_in, g_q_a, w_q_b, g_kv_a, w_kv_b, na_rpb, w_out, g_mix, g_mem_in, w_mem_q, w_mem_kv, w_mem_o, g_mem_attn, ffn2_w_gate, ffn2_w_up, ffn2_w_down, g_ffn2, g_final):
    batch, seq, d = x.shape
    m = batch * seq
    depth = ffn1_w_gate.shape[0]
    pos = positions.reshape(m, 1).astype(jnp.int32)
    invf_q, invf_k = _inv_freq_lanes(LANES), _inv_freq_lanes(MLA_ROPE_DIM)
    mem2 = mem.reshape(-1, d)

    h = x.reshape(m, d)
    u = norm_cast(h, g_ffn1[0, 0])
    out = None
    for l in range(depth):
        f = ffn(u, ffn1_w_gate, ffn1_w_up, ffn1_w_down, l)
        h, u = resid_norm(h, f, g_ffn1[l, 1], FFN_RES_WEIGHT, g_mix[l, 0])

        qkv = mm_nt(u, jnp.swapaxes(w_in, 1, 2), l, BF16, bm=2048, bn=512, n_cols=IN_PROJ_MAIN,
                    scaled_cols=NA_WIDTH, scale=LOG2_E * float(NA_HEAD_DIM) ** -0.5, name="in_proj")
        q_cat = q_b_proj(qkv, g_q_a[l], w_q_b[l], pos, invf_q)
        k_cat, v_mla = kv_b_proj(qkv, g_kv_a[l], w_kv_b[l], u, w_in[l, :, IN_PROJ_MAIN:], pos, invf_k)
        o_mla = mla_attention(q_cat, k_cat, v_mla, batch, seq)
        o_na = na_attention(qkv, na_bias_table(na_rpb[l]), batch, seq)
        o = mm2(o_na, o_mla, w_out, l, BF16, bm=2048, bn=512, name="out_proj")

        mem_n = norm_cast(mem2, g_mem_in[l])
        kv_mem = mm(mem_n, w_mem_kv, l, BF16, bm=1024, bn=512, name="mem_kv_proj")
        h, u = mem_block(h, o, g_mix[l, 1], g_mem_attn[l, 0], w_mem_q[l], kv_mem, w_mem_o[l],
                         g_mem_attn[l, 1], g_ffn2[l, 0], batch, seq)

        f = ffn(u, ffn2_w_gate, ffn2_w_up, ffn2_w_down, l)
        if l + 1 < depth:
            h = resid_final(h, f, g_ffn2[l, 1], FFN_RES_WEIGHT, g_final[l])
            u = norm_cast(h, g_ffn1[l + 1, 0])
        else:
            out = resid_final(h, f, g_ffn2[l, 1], FFN_RES_WEIGHT, g_final[l])
    return out.reshape(batch, seq, d)
```

```python
import functools

import jax
import jax.numpy as jnp
from jax import lax
from jax.experimental import pallas as pl
from jax.experimental.pallas import tpu as pltpu

F32 = jnp.float32
BF16 = jnp.bfloat16

GRID_W = 64
NA_HEADS = 16
NA_HEAD_DIM = 128
NA_KH = 8
NA_KW = 16
MLA_HEADS = 16
MLA_Q_RANK = 1024
MLA_KV_RANK = 512
MLA_NOPE_DIM = 128
MLA_ROPE_DIM = 64
MLA_V_DIM = 128
MLA_QK_PAD = 256
MLA_VT_ROWS = MLA_V_DIM + 16
LOG2_E = 1.4426950408889634
ROPE_THETA = 10000.0
MEM_HEADS = 4
MEM_HEAD_DIM = 128
FFN_RES_WEIGHT = 0.5
NORM_EPS = 1e-6
NEG_INF = -1e30
NA_WIDTH = NA_HEADS * NA_HEAD_DIM
MLA_WIDTH = MLA_HEADS * MLA_V_DIM
IN_PROJ_MAIN = 3 * NA_WIDTH + MLA_Q_RANK + MLA_KV_RANK
LANES = 128
MIB = 1024 * 1024
VMEM_SMALL_MIB = 32
VMEM_MIB = 56
VMEM_MAX_MIB = 60


def _cparams(n_axes, vmem_mib):
    return pltpu.CompilerParams(dimension_semantics=("parallel",) * n_axes,
                                vmem_limit_bytes=vmem_mib * MIB)


def _rms(x, g):
    return x * lax.rsqrt(jnp.mean(x * x, axis=-1, keepdims=True) + NORM_EPS) * g


def _row(v):
    return v.reshape(1, -1).astype(F32)


def _norm_cast_kernel(x_ref, g_ref, o_ref):
    o_ref[...] = _rms(x_ref[...], g_ref[...]).astype(o_ref.dtype)


def norm_cast(x, g, bm=512):
    m, d = x.shape
    bm = min(bm, m)
    return pl.pallas_call(
        _norm_cast_kernel,
        grid=(m // bm,),
        in_specs=[pl.BlockSpec((bm, d), lambda i: (i, 0)), pl.BlockSpec((1, d), lambda i: (0, 0))],
        out_specs=pl.BlockSpec((bm, d), lambda i: (i, 0)),
        out_shape=jax.ShapeDtypeStruct((m, d), BF16),
        compiler_params=_cparams(1, VMEM_SMALL_MIB),
        name="norm_cast",
    )(x, _row(g))


def _resid_norm_kernel(h_ref, f_ref, gp_ref, gn_ref, ho_ref, uo_ref, *, weight):
    h = h_ref[...] + weight * _rms(f_ref[...].astype(F32), gp_ref[...])
    ho_ref[...] = h
    uo_ref[...] = _rms(h, gn_ref[...]).astype(uo_ref.dtype)


def resid_norm(h, f, g_post, weight, g_next, bm=512):
    m, d = h.shape
    bm = min(bm, m)
    row = pl.BlockSpec((bm, d), lambda i: (i, 0))
    vec = pl.BlockSpec((1, d), lambda i: (0, 0))
    return pl.pallas_call(
        functools.partial(_resid_norm_kernel, weight=weight),
        grid=(m // bm,),
        in_specs=[row, row, vec, vec],
        out_specs=[row, row],
        out_shape=[jax.ShapeDtypeStruct((m, d), F32), jax.ShapeDtypeStruct((m, d), BF16)],
        compiler_params=_cparams(1, VMEM_MAX_MIB),
        name="resid_norm",
    )(h, f, _row(g_post), _row(g_next))


def _resid_final_kernel(h_ref, f_ref, gp_ref, gn_ref, o_ref, *, weight):
    h = h_ref[...] + weight * _rms(f_ref[...].astype(F32), gp_ref[...])
    o_ref[...] = _rms(h, gn_ref[...])


def resid_final(h, f, g_post, weight, g_final, bm=512):
    m, d = h.shape
    bm = min(bm, m)
    row = pl.BlockSpec((bm, d), lambda i: (i, 0))
    vec = pl.BlockSpec((1, d), lambda i: (0, 0))
    return pl.pallas_call(
        functools.partial(_resid_final_kernel, weight=weight),
        grid=(m // bm,),
        in_specs=[row, row, vec, vec],
        out_specs=row,
        out_shape=jax.ShapeDtypeStruct((m, d), F32),
        compiler_params=_cparams(1, VMEM_MIB),
        name="resid_final",
    )(h, f, _row(g_post), _row(g_final))


def _x_spec(bm, k):
    return pl.BlockSpec((bm, k), lambda i, j: (i, 0))


def _x_half_specs(bm, k):
    assert bm % 2 == 0
    return (pl.BlockSpec((bm // 2, k), lambda i, j: (2 * i, 0), pipeline_mode=pl.Buffered(1)),
            pl.BlockSpec((bm // 2, k), lambda i, j: (2 * i + 1, 0)))


def _w_spec(k, bn, layer):
    return pl.BlockSpec((None, k, bn), lambda i, j: (layer, 0, j))


def _mm_kernel(x_ref, w_ref, o_ref):
    o_ref[...] = jnp.dot(x_ref[...], w_ref[...].astype(BF16),
                         preferred_element_type=F32).astype(o_ref.dtype)


def mm(x, w, layer, out_dtype, bm, bn, n_cols=None, vmem_mib=VMEM_MIB, name="mm"):
    m, k = x.shape
    n = w.shape[2] if n_cols is None else n_cols
    bm, bn = min(bm, m), min(bn, n)
    assert m % bm == 0 and n % bn == 0 and w.shape[1] == k
    return pl.pallas_call(
        _mm_kernel,
        grid=(m // bm, n // bn),
        in_specs=[_x_spec(bm, k), _w_spec(k, bn, layer)],
        out_specs=pl.BlockSpec((bm, bn), lambda i, j: (i, j)),
        out_shape=jax.ShapeDtypeStruct((m, n), out_dtype),
        compiler_params=_cparams(2, vmem_mib),
        name=name,
    )(x, w)


def _mm_nt_kernel(*refs, scaled_blocks, scale):
    *x_refs, wt_ref, o_ref = refs
    wt = wt_ref[...].astype(BF16)
    r0 = 0
    for x_ref in x_refs:
        acc = lax.dot_general(x_ref[...], wt, (((1,), (1,)), ((), ())), preferred_element_type=F32)
        if scaled_blocks:
            acc = acc * jnp.where(pl.program_id(1) < scaled_blocks, scale, 1.0)
        o_ref[r0:r0 + x_ref.shape[0], :] = acc.astype(o_ref.dtype)
        r0 += x_ref.shape[0]


def mm_nt(x, wt, layer, out_dtype, bm, bn, n_cols=None, scaled_cols=0, scale=1.0, vmem_mib=VMEM_MIB,
          name="mm_nt"):
    m, k = x.shape
    n = wt.shape[1] if n_cols is None else n_cols
    bm, bn = min(bm, m), min(bn, n)
    assert m % bm == 0 and n % bn == 0 and wt.shape[2] == k and scaled_cols % bn == 0
    return pl.pallas_call(
        functools.partial(_mm_nt_kernel, scaled_blocks=scaled_cols // bn, scale=scale),
        grid=(m // bm, n // bn),
        in_specs=[*_x_half_specs(bm, k), pl.BlockSpec((None, bn, k), lambda i, j: (layer, j, 0))],
        out_specs=pl.BlockSpec((bm, bn), lambda i, j: (i, j)),
        out_shape=jax.ShapeDtypeStruct((m, n), out_dtype),
        compiler_params=_cparams(2, vmem_mib),
        name=name,
    )(x, x, wt)


def _mm2_kernel(xa_top_ref, xa_bot_ref, xb_top_ref, xb_bot_ref, w_ref, o_ref):
    ka = xa_top_ref.shape[1]
    w = w_ref[...].astype(BF16)
    r0 = 0
    for xa_ref, xb_ref in ((xa_top_ref, xb_top_ref), (xa_bot_ref, xb_bot_ref)):
        acc = jnp.dot(xa_ref[...], w[:ka], preferred_element_type=F32)
        acc = acc + jnp.dot(xb_ref[...], w[ka:], preferred_element_type=F32)
        o_ref[r0:r0 + xa_ref.shape[0], :] = acc.astype(o_ref.dtype)
        r0 += xa_ref.shape[0]


def mm2(xa, xb, w, layer, out_dtype, bm, bn, vmem_mib=VMEM_MIB, name="mm2"):
    m, ka = xa.shape
    kb = xb.shape[1]
    n = w.shape[2]
    bm, bn = min(bm, m), min(bn, n)
    assert m % bm == 0 and n % bn == 0 and w.shape[1] == ka + kb
    return pl.pallas_call(
        _mm2_kernel,
        grid=(m // bm, n // bn),
        in_specs=[*_x_half_specs(bm, ka), *_x_half_specs(bm, kb), _w_spec(ka + kb, bn, layer)],
        out_specs=pl.BlockSpec((bm, bn), lambda i, j: (i, j)),
        out_shape=jax.ShapeDtypeStruct((m, n), out_dtype),
        compiler_params=_cparams(2, vmem_mib),
        name=name,
    )(xa, xa, xb, xb, w)


def _gateup_kernel(xa_ref, xb_ref, wg_ref, wu_ref, wd_ref, o_ref, wdo_ref):
    wg = wg_ref[...].astype(BF16)
    wu = wu_ref[...].astype(BF16)
    half = xa_ref.shape[0]
    for x_ref, r0 in ((xa_ref, 0), (xb_ref, half)):
        x = x_ref[...]
        g = jnp.dot(x, wg, preferred_element_type=F32)
        u = jnp.dot(x, wu, preferred_element_type=F32)
        o_ref[r0:r0 + half, :] = (g * jax.nn.sigmoid(g) * u).astype(o_ref.dtype)
    wdo_ref[...] = wd_ref[...].astype(wdo_ref.dtype)


def gateup(x, w_gate, w_up, w_down, layer, bm=2048, bf=256):
    m, k = x.shape
    f = w_gate.shape[2]
    d_out = w_down.shape[2]
    bm, bf = min(bm, m), min(bf, f)
    assert m % bm == 0 and f % bf == 0 and w_down.shape[1] == f
    n_j = f // bf
    slab = f // ((m // bm) * n_j)
    assert slab * (m // bm) * n_j == f and slab % 16 == 0
    return pl.pallas_call(
        _gateup_kernel,
        grid=(m // bm, n_j),
        in_specs=[*_x_half_specs(bm, k), _w_spec(k, bf, layer), _w_spec(k, bf, layer),
                  pl.BlockSpec((None, slab, d_out), lambda i, j: (layer, i * n_j + j, 0))],
        out_specs=[pl.BlockSpec((bm, bf), lambda i, j: (i, j)),
                   pl.BlockSpec((None, slab, d_out), lambda i, j: (0, i * n_j + j, 0))],
        out_shape=[jax.ShapeDtypeStruct((m, f), BF16), jax.ShapeDtypeStruct((1, f, d_out), BF16)],
        compiler_params=_cparams(2, VMEM_MIB),
        name="ffn_gateup",
    )(x, x, w_gate, w_up, w_down)


def ffn(u, w_gate, w_up, w_down, layer):
    hidden, w_down_bf16 = gateup(u, w_gate, w_up, w_down, layer)
    return mm(hidden, w_down_bf16, 0, BF16, bm=512, bn=512, name="ffn_down")


def _rope_tables(pos_ref, invf_ref, width):
    ang = pos_ref[...].astype(F32) * invf_ref[...]
    lane = lax.broadcasted_iota(jnp.int32, ang.shape, 1)
    first_half = (lane & (MLA_ROPE_DIM - 1)) < MLA_ROPE_DIM // 2
    valid = lane < width
    sin = jnp.sin(ang)
    cos_t = jnp.where(valid, jnp.cos(ang), 0.0)
    sin_t = jnp.where(valid, jnp.where(first_half, -sin, sin), 0.0)
    return cos_t, sin_t, first_half


def _rope_apply(t, cos_t, sin_t, first_half):
    half = MLA_ROPE_DIM // 2
    partner = jnp.where(first_half, pltpu.roll(t, LANES - half, 1), pltpu.roll(t, half, 1))
    return t * cos_t + partner * sin_t


def _inv_freq_lanes(width):
    half = MLA_ROPE_DIM // 2
    inv_freq = 1.0 / (ROPE_THETA ** (jnp.arange(half, dtype=F32) / half))
    return jnp.concatenate([jnp.tile(inv_freq, width // half), jnp.zeros((LANES - width,), F32)]).reshape(1, LANES)


def _qb_kernel(cq_ref, g_ref, wn_ref, wp_ref, pos_ref, invf_ref, o_ref):
    cn = _rms(cq_ref[...].astype(F32), g_ref[...]).astype(BF16)
    qn = jnp.dot(cn, wn_ref[...], preferred_element_type=F32)
    qp = jnp.dot(cn, wp_ref[...], preferred_element_type=F32)
    cos_t, sin_t, first_half = _rope_tables(pos_ref, invf_ref, LANES)
    low = lax.broadcasted_iota(jnp.int32, cos_t.shape, 1) < MLA_ROPE_DIM
    for pair in range(MLA_HEADS // 2):
        r = _rope_apply(qp[:, pair * LANES:(pair + 1) * LANES], cos_t, sin_t, first_half)
        pe = (jnp.where(low, r, 0.0), jnp.where(low, pltpu.roll(r, LANES - MLA_ROPE_DIM, 1), 0.0))
        for k in range(2):
            h = 2 * pair + k
            c0 = h * MLA_QK_PAD
            o_ref[:, c0:c0 + MLA_NOPE_DIM] = qn[:, h * MLA_NOPE_DIM:(h + 1) * MLA_NOPE_DIM].astype(o_ref.dtype)
            o_ref[:, c0 + MLA_NOPE_DIM:c0 + MLA_QK_PAD] = pe[k].astype(o_ref.dtype)


def q_b_proj(qkv, g_q_a, w_q_b, pos, invf, bm=512):
    m = qkv.shape[0]
    bm = min(bm, m)
    h, dq = MLA_HEADS, MLA_NOPE_DIM + MLA_ROPE_DIM
    w = w_q_b.reshape(MLA_Q_RANK, h, dq) * (LOG2_E * float(dq) ** -0.5)
    w_nope = w[:, :, :MLA_NOPE_DIM].reshape(MLA_Q_RANK, h * MLA_NOPE_DIM).astype(BF16)
    w_pe = w[:, :, MLA_NOPE_DIM:].reshape(MLA_Q_RANK, h * MLA_ROPE_DIM).astype(BF16)
    cq_block = (3 * NA_WIDTH) // MLA_Q_RANK
    return pl.pallas_call(
        _qb_kernel,
        grid=(m // bm,),
        in_specs=[pl.BlockSpec((bm, MLA_Q_RANK), lambda i: (i, cq_block)),
                  pl.BlockSpec((1, MLA_Q_RANK), lambda i: (0, 0)),
                  pl.BlockSpec((MLA_Q_RANK, h * MLA_NOPE_DIM), lambda i: (0, 0)),
                  pl.BlockSpec((MLA_Q_RANK, h * MLA_ROPE_DIM), lambda i: (0, 0)),
                  pl.BlockSpec((bm, 1), lambda i: (i, 0)),
                  pl.BlockSpec((1, LANES), lambda i: (0, 0))],
        out_specs=pl.BlockSpec((bm, h * MLA_QK_PAD), lambda i: (i, 0)),
        out_shape=jax.ShapeDtypeStruct((m, h * MLA_QK_PAD), BF16),
        compiler_params=_cparams(1, VMEM_MIB),
        name="mla_q_proj",
    )(qkv, _row(g_q_a), w_nope, w_pe, pos, invf)


def _kvb_kernel(ckv_ref, g_ref, wk_ref, wvt_ref, u_ref, wt_ref, pos_ref, invf_ref, k_ref, vt_ref):
    cn = _rms(ckv_ref[...].astype(F32), g_ref[...]).astype(BF16)
    kn = jnp.dot(cn, wk_ref[...], preferred_element_type=F32)
    kr = jnp.dot(u_ref[...], wt_ref[...], preferred_element_type=F32)
    cos_t, sin_t, first_half = _rope_tables(pos_ref, invf_ref, MLA_ROPE_DIM)
    kpe = _rope_apply(kr, cos_t, sin_t, first_half).astype(k_ref.dtype)
    for h in range(MLA_HEADS):
        k_ref[:, h * MLA_QK_PAD:h * MLA_QK_PAD + MLA_NOPE_DIM] = (
            kn[:, h * MLA_NOPE_DIM:(h + 1) * MLA_NOPE_DIM].astype(k_ref.dtype))
        k_ref[:, h * MLA_QK_PAD + MLA_NOPE_DIM:(h + 1) * MLA_QK_PAD] = kpe
    vt = lax.dot_general(wvt_ref[...], cn, (((1,), (1,)), ((), ())), preferred_element_type=F32)
    ones = jnp.ones((MLA_VT_ROWS - MLA_V_DIM, vt.shape[1]), vt_ref.dtype)
    for h in range(MLA_HEADS):
        r0 = h * MLA_VT_ROWS
        vt_ref[r0:r0 + MLA_V_DIM, :] = vt[h * MLA_V_DIM:(h + 1) * MLA_V_DIM].astype(vt_ref.dtype)
        vt_ref[r0 + MLA_V_DIM:r0 + MLA_VT_ROWS, :] = ones


def kv_b_proj(qkv, g_kv_a, w_kv_b, u, w_rope_in, pos, invf, bm=512):
    m, d = u.shape
    bm = min(bm, m)
    h = MLA_HEADS
    ckv_block = (3 * NA_WIDTH + MLA_Q_RANK) // MLA_KV_RANK
    wt = jnp.pad(w_rope_in, ((0, 0), (0, LANES - MLA_ROPE_DIM))).astype(BF16)
    w3 = w_kv_b.reshape(MLA_KV_RANK, h, MLA_NOPE_DIM + MLA_V_DIM)
    w_k = w3[:, :, :MLA_NOPE_DIM].reshape(MLA_KV_RANK, h * MLA_NOPE_DIM).astype(BF16)
    w_vt = w3[:, :, MLA_NOPE_DIM:].reshape(MLA_KV_RANK, h * MLA_V_DIM).T.astype(BF16)
    return pl.pallas_call(
        _kvb_kernel,
        grid=(m // bm,),
        in_specs=[pl.BlockSpec((bm, MLA_KV_RANK), lambda i: (i, ckv_block)),
                  pl.BlockSpec((1, MLA_KV_RANK), lambda i: (0, 0)),
                  pl.BlockSpec((MLA_KV_RANK, h * MLA_NOPE_DIM), lambda i: (0, 0)),
                  pl.BlockSpec((h * MLA_V_DIM, MLA_KV_RANK), lambda i: (0, 0)),
                  pl.BlockSpec((bm, d), lambda i: (i, 0)),
                  pl.BlockSpec((d, LANES), lambda i: (0, 0)),
                  pl.BlockSpec((bm, 1), lambda i: (i, 0)),
                  pl.BlockSpec((1, LANES), lambda i: (0, 0))],
        out_specs=[pl.BlockSpec((bm, h * MLA_QK_PAD), lambda i: (i, 0)),
                   pl.BlockSpec((h * MLA_VT_ROWS, bm), lambda i: (0, i))],
        out_shape=[jax.ShapeDtypeStruct((m, h * MLA_QK_PAD), BF16),
                   jax.ShapeDtypeStruct((h * MLA_VT_ROWS, m), BF16)],
        compiler_params=_cparams(1, VMEM_MIB),
        name="mla_kv_proj",
    )(qkv, _row(g_kv_a), w_k, w_vt, u, wt, pos, invf)


def _mla_kernel(q_ref, qn_ref, k_ref, kn_ref, vt_ref, o_ref, s_scr, *, heads, bq):
    def scores(q_tile_ref, row0, keys_ref, slot, h):
        qk = slice(h * MLA_QK_PAD, (h + 1) * MLA_QK_PAD)
        s_scr[slot, h] = lax.dot_general(keys_ref[:, qk], q_tile_ref[row0:row0 + bq, qk],
                                         (((1,), (1,)), ((), ())), preferred_element_type=F32)

    def values(slot, row0, h):
        s = s_scr[slot, h]
        p = jnp.exp2(s - jnp.max(s, axis=0, keepdims=True)).astype(BF16)
        ot = jnp.dot(vt_ref[h * MLA_VT_ROWS:(h + 1) * MLA_VT_ROWS, :], p,
                     preferred_element_type=F32)
        o = ot[:MLA_V_DIM] / ot[MLA_V_DIM:MLA_V_DIM + 1]
        o_ref[row0:row0 + bq, h * MLA_V_DIM:(h + 1) * MLA_V_DIM] = o.T.astype(o_ref.dtype)

    first_step = (pl.program_id(0) == 0) & (pl.program_id(1) == 0) & (pl.program_id(2) == 0)

    @pl.when(first_step)
    def _():
        for h in range(heads):
            scores(q_ref, 0, k_ref, 0, h)

    for h in range(heads):
        scores(q_ref, bq, k_ref, 1, h)
        values(0, 0, h)
    for h in range(heads):
        scores(qn_ref, 0, kn_ref, 0, h)
        values(1, bq, h)


def mla_attention(q, k, vt, batch, seq, bq=512, heads_per_step=2):
    bq = min(bq, seq // 2)
    pairs = seq // (2 * bq)
    g = heads_per_step
    groups = MLA_HEADS // g
    n_steps = batch * groups * pairs

    def next_step(b, h, i):
        flat = jnp.minimum((b * groups + h) * pairs + i + 1, n_steps - 1)
        return flat // (groups * pairs), (flat // pairs) % groups, flat % pairs

    def q_next_map(b, h, i):
        nb, nh, ni = next_step(b, h, i)
        return (nb * pairs + ni) * 2, nh

    def k_next_map(b, h, i):
        nb, nh, _ = next_step(b, h, i)
        return nb, nh

    return pl.pallas_call(
        functools.partial(_mla_kernel, heads=g, bq=bq),
        grid=(batch, groups, pairs),
        in_specs=[pl.BlockSpec((2 * bq, g * MLA_QK_PAD), lambda b, h, i: (b * pairs + i, h)),
                  pl.BlockSpec((bq, g * MLA_QK_PAD), q_next_map),
                  pl.BlockSpec((seq, g * MLA_QK_PAD), lambda b, h, i: (b, h)),
                  pl.BlockSpec((seq, g * MLA_QK_PAD), k_next_map),
                  pl.BlockSpec((g * MLA_VT_ROWS, seq), lambda b, h, i: (h, b))],
        out_specs=pl.BlockSpec((2 * bq, g * MLA_V_DIM), lambda b, h, i: (b * pairs + i, h)),
        out_shape=jax.ShapeDtypeStruct((batch * seq, MLA_WIDTH), BF16),
        scratch_shapes=[pltpu.VMEM((2, g, seq, bq), F32)],
        compiler_params=pltpu.CompilerParams(dimension_semantics=("arbitrary",) * 3,
                                             vmem_limit_bytes=VMEM_MIB * MIB),
        name="mla_attention",
    )(q, q, k, k, vt)


NA_BIAS_ROWS = 2 * NA_KH - 1
NA_BIAS_COLS = 2 * NA_KW - 1
NA_BIAS_PAIRS = NA_BIAS_ROWS - 1


def _na_bias_kernel(rpb_ref, o_ref):
    row = lax.broadcasted_iota(jnp.int32, (GRID_W, LANES), 0)
    lane = lax.broadcasted_iota(jnp.int32, (GRID_W, LANES), 1)
    kc = lane & (GRID_W - 1)
    left = lane < GRID_W
    cstart = jnp.clip(row - NA_KW // 2, 0, GRID_W - NA_KW)
    in_window = (kc >= cstart) & (kc < cstart + NA_KW)
    toeplitz = []
    for d in range(NA_BIAS_ROWS):
        vec = jnp.broadcast_to(rpb_ref[d:d + 1, :], (GRID_W, LANES))
        toeplitz.append(pltpu.roll(vec, LANES - (NA_KW - 1), 1, stride=1, stride_axis=0))
    for d in range(NA_BIAS_PAIRS):
        tile = jnp.where(left, toeplitz[d], pltpu.roll(toeplitz[d + 1], GRID_W, 1))
        o_ref[0, d] = jnp.where(in_window, tile * LOG2_E, NEG_INF)


def na_bias_table(rpb):
    rpb_lanes = jnp.pad(rpb.astype(F32), ((0, 0), (0, 0), (0, LANES - NA_BIAS_COLS)))
    return pl.pallas_call(
        _na_bias_kernel,
        grid=(NA_HEADS,),
        in_specs=[pl.BlockSpec((None, NA_BIAS_ROWS, LANES), lambda h: (h, 0, 0))],
        out_specs=pl.BlockSpec((1, NA_BIAS_PAIRS, GRID_W, LANES), lambda h: (h, 0, 0, 0)),
        out_shape=jax.ShapeDtypeStruct((NA_HEADS, NA_BIAS_PAIRS, GRID_W, LANES), F32),
        compiler_params=_cparams(1, VMEM_SMALL_MIB),
        name="na_bias_table",
    )(rpb_lanes)


def _na_kernel(q_ref, k_ref, v_ref, bias_ref, o_ref, s_scr, *, rows, heads, rows_per_step):
    nk = NA_KH * GRID_W
    n_steps = rows // rows_per_step

    def row_geometry(step, rr):
        r = step * rows_per_step + rr
        rs = jnp.clip(r - NA_KH // 2, 0, rows - NA_KH)
        d0 = rs - r + (NA_KH - 1)
        return pl.multiple_of(r * GRID_W, GRID_W), pl.multiple_of(rs * GRID_W, GRID_W), d0

    def score_stage(step, slot):
        for rr in range(rows_per_step):
            q0, k0, d0 = row_geometry(step, rr)
            for h in range(heads):
                cols = slice(h * NA_HEAD_DIM, (h + 1) * NA_HEAD_DIM)
                q = q_ref[pl.ds(q0, GRID_W), cols]
                k = k_ref[pl.ds(k0, nk), cols]
                s = lax.dot_general(q, k, (((1,), (1,)), ((), ())), preferred_element_type=F32)
                bias = jnp.concatenate([bias_ref[h, d0 + 2 * p] for p in range(NA_KH // 2)], axis=1)
                s_scr[slot, rr * heads + h] = s + bias

    def value_stage(step, slot):
        for rr in range(rows_per_step):
            q0, k0, _ = row_geometry(step, rr)
            for h in range(heads):
                cols = slice(h * NA_HEAD_DIM, (h + 1) * NA_HEAD_DIM)
                s = s_scr[slot, rr * heads + h]
                p = jnp.exp2(s - jnp.max(s, axis=-1, keepdims=True))
                l = jnp.sum(p, axis=-1, keepdims=True)
                o = jnp.dot(p.astype(BF16), v_ref[pl.ds(k0, nk), cols], preferred_element_type=F32)
                o_ref[pl.ds(q0, GRID_W), cols] = (o / l).astype(o_ref.dtype)

    score_stage(0, 0)

    def body(u, carry):
        score_stage(2 * u + 1, 1)
        value_stage(2 * u, 0)
        score_stage(2 * u + 2, 0)
        value_stage(2 * u + 1, 1)
        return carry

    lax.fori_loop(0, n_steps // 2 - 1, body, 0)
    score_stage(n_steps - 1, 1)
    value_stage(n_steps - 2, 0)
    value_stage(n_steps - 1, 1)


def na_attention(qkv, bias, batch, seq, heads_per_step=4, rows_per_step=2):
    g = heads_per_step
    w = g * NA_HEAD_DIM
    groups = NA_HEADS // g
    rows = seq // GRID_W
    assert rows % rows_per_step == 0 and rows >= NA_KH
    return pl.pallas_call(
        functools.partial(_na_kernel, rows=rows, heads=g, rows_per_step=rows_per_step),
        grid=(batch, groups),
        in_specs=[pl.BlockSpec((seq, w), lambda b, j: (b, j)),
                  pl.BlockSpec((seq, w), lambda b, j: (b, groups + j)),
                  pl.BlockSpec((seq, w), lambda b, j: (b, 2 * groups + j)),
                  pl.BlockSpec((g, NA_BIAS_PAIRS, GRID_W, LANES), lambda b, j: (j, 0, 0, 0))],
        out_specs=pl.BlockSpec((seq, w), lambda b, j: (b, j)),
        out_shape=jax.ShapeDtypeStruct((batch * seq, NA_WIDTH), BF16),
        scratch_shapes=[pltpu.VMEM((2, rows_per_step * g, GRID_W, NA_KH * GRID_W), F32)],
        compiler_params=_cparams(2, VMEM_MIB),
        name="na_attention",
    )(qkv, qkv, qkv, bias)


def _mem_block_kernel(h_ref, o_ref, g_mix_ref, g_pre_ref, wq_ref, kv_ref, wo_ref, g_post_ref, g_next_ref,
                      ho_ref, uo_ref, *, scale):
    chunk = h_ref.shape[0] // 2
    for r0 in (0, chunk):
        rows = slice(r0, r0 + chunk)
        h1 = h_ref[rows, :] + _rms(o_ref[rows, :].astype(F32), g_mix_ref[...])
        u = _rms(h1, g_pre_ref[...]).astype(BF16)
        q = (jnp.dot(u, wq_ref[...], preferred_element_type=F32) * scale).astype(BF16)
        outs = []
        for hd in range(MEM_HEADS):
            qh = q[:, hd * MEM_HEAD_DIM:(hd + 1) * MEM_HEAD_DIM]
            k = kv_ref[:, 2 * hd * MEM_HEAD_DIM:(2 * hd + 1) * MEM_HEAD_DIM]
            v = kv_ref[:, (2 * hd + 1) * MEM_HEAD_DIM:(2 * hd + 2) * MEM_HEAD_DIM]
            s = lax.dot_general(qh, k, (((1,), (1,)), ((), ())), preferred_element_type=F32)
            p = jnp.exp2(s - jnp.max(s, axis=-1, keepdims=True))
            l = jnp.sum(p, axis=-1, keepdims=True)
            outs.append((jnp.dot(p.astype(BF16), v, preferred_element_type=F32) / l).astype(BF16))
        a = jnp.dot(jnp.concatenate(outs, axis=1), wo_ref[...], preferred_element_type=F32)
        h2 = h1 + _rms(a, g_post_ref[...])
        ho_ref[rows, :] = h2
        uo_ref[rows, :] = _rms(h2, g_next_ref[...]).astype(uo_ref.dtype)


def mem_block(h, o, g_mix_post, g_pre, w_q, kv, w_o, g_post, g_next, batch, seq, bq=256):
    mem_len = kv.shape[0] // batch
    d = h.shape[1]
    bq = min(bq, seq)
    nq = seq // bq
    width = MEM_HEADS * MEM_HEAD_DIM
    row = pl.BlockSpec((bq, d), lambda b, i: (b * nq + i, 0))
    vec = pl.BlockSpec((1, d), lambda b, i: (0, 0))
    once = pl.Buffered(1)
    return pl.pallas_call(
        functools.partial(_mem_block_kernel, scale=LOG2_E * float(MEM_HEAD_DIM) ** -0.5),
        grid=(batch, nq),
        in_specs=[row, row, vec, vec,
                  pl.BlockSpec((d, width), lambda b, i: (0, 0), pipeline_mode=once),
                  pl.BlockSpec((mem_len, 2 * width), lambda b, i: (b, 0)),
                  pl.BlockSpec((width, d), lambda b, i: (0, 0), pipeline_mode=once),
                  vec, vec],
        out_specs=[row, row],
        out_shape=[jax.ShapeDtypeStruct((batch * seq, d), F32), jax.ShapeDtypeStruct((batch * seq, d), BF16)],
        compiler_params=_cparams(2, VMEM_MIB),
        name="mem_block",
    )(h, o, _row(g_mix_post), _row(g_pre), w_q.astype(BF16), kv, w_o.astype(BF16), _row(g_post), _row(g_next))


def kernel(x, mem, positions, ffn1_w_gate, ffn1_w_up, ffn1_w_down, g_ffn1, w_in, g_q_a, w_q_b, g_kv_a, w_kv_b, na_rpb, w_out, g_mix, g_mem_in, w_mem_q, w_mem_kv, w_mem_o, g_mem_attn, ffn2_w_gate, ffn2_w_up, ffn2_w_down, g_ffn2, g_final):
    batch, seq, d = x.shape
    m = batch * seq
    depth = ffn1_w_gate.shape[0]
    pos = positions.reshape(m, 1).astype(jnp.int32)
    invf_q, invf_k = _inv_freq_lanes(LANES), _inv_freq_lanes(MLA_ROPE_DIM)
    mem2 = mem.reshape(-1, d)

    h = x.reshape(m, d)
    u = norm_cast(h, g_ffn1[0, 0])
    out = None
    for l in range(depth):
        f = ffn(u, ffn1_w_gate, ffn1_w_up, ffn1_w_down, l)
        h, u = resid_norm(h, f, g_ffn1[l, 1], FFN_RES_WEIGHT, g_mix[l, 0])

        qkv = mm_nt(u, jnp.swapaxes(w_in, 1, 2), l, BF16, bm=2048, bn=512, n_cols=IN_PROJ_MAIN,
                    scaled_cols=NA_WIDTH, scale=LOG2_E * float(NA_HEAD_DIM) ** -0.5, name="in_proj")
        q_cat = q_b_proj(qkv, g_q_a[l], w_q_b[l], pos, invf_q)
        k_cat, v_mla = kv_b_proj(qkv, g_kv_a[l], w_kv_b[l], u, w_in[l, :, IN_PROJ_MAIN:], pos, invf_k)
        o_mla = mla_attention(q_cat, k_cat, v_mla, batch, seq)
        o_na = na_attention(qkv, na_bias_table(na_rpb[l]), batch, seq)
        o = mm2(o_na, o_mla, w_out, l, BF16, bm=2048, bn=512, name="out_proj")

        mem_n = norm_cast(mem2, g_mem_in[l])
        kv_mem = mm(mem_n, w_mem_kv, l, BF16, bm=1024, bn=512, name="mem_kv_proj")
        h, u = mem_block(h, o, g_mix[l, 1], g_mem_attn[l, 0], w_mem_q[l], kv_mem, w_mem_o[l],
                         g_mem_attn[l, 1], g_ffn2[l, 0], batch, seq)

        f = ffn(u, ffn2_w_gate, ffn2_w_up, ffn2_w_down, l)
        if l + 1 < depth:
            h = resid_final(h, f, g_ffn2[l, 1], FFN_RES_WEIGHT, g_final[l])
            u = norm_cast(h, g_ffn1[l + 1, 0])
        else:
            out = resid_final(h, f, g_ffn2[l, 1], FFN_RES_WEIGHT, g_final[l])
    return out.reshape(batch, seq, d)
```
